```python
import math
import jax, jax.numpy as jnp
from jax import lax
import numpy as np

D_MODEL = 1024
BATCH = 2
SEQ = 8192
DEPTH = 2

CHUNK = 64
Q_BLOCK = 128
MIX_W = 512
N_BRANCH = 3
EPS = 1e-6
FOX_HEADS = 8
FOX_HD = MIX_W // FOX_HEADS
GLA_HEADS = 4
GLA_DK = 64
GLA_DV = MIX_W // GLA_HEADS
GLA_RANK = 16
GLA_GATE_NORM = 16.0
SSM_HEADS = 8
SSM_HD = MIX_W // SSM_HEADS
SSM_GROUPS = 2
SSM_STATE = 64
SSM_CONV = 4
SSM_CONV_DIM = MIX_W + 2 * SSM_GROUPS * SSM_STATE
N_EGROUPS = 4
EXP_PER_GROUP = 4
N_EXPERTS = N_EGROUPS * EXP_PER_GROUP
TOP_K = 2
D_EXPERT = 512
FOX_COLS = 3 * MIX_W + FOX_HEADS
GLA_COLS = 2 * GLA_HEADS * GLA_DK + 2 * MIX_W + GLA_RANK
SSM_COLS = MIX_W + SSM_CONV_DIM + SSM_HEADS
GATE_COLS = N_BRANCH * D_MODEL
IN_COLS = FOX_COLS + GLA_COLS + SSM_COLS + GATE_COLS

kernel_name = "hybrid_fox_gla_ssd_hiermoe_block"


def rms_norm(x, gain):
    xf = x.astype(jnp.float32)
    y = xf * lax.rsqrt(jnp.mean(xf * xf, axis=-1, keepdims=True) + EPS)
    return (y * gain.astype(jnp.float32)).astype(x.dtype)


def _to_chunks(t):
    b, s, h = t.shape[:3]
    rest = t.shape[3:]
    t = t.reshape((b, s // CHUNK, CHUNK, h) + rest)
    perm = (0, 3, 1, 2) + tuple(range(4, t.ndim))
    return jnp.transpose(t, perm).astype(jnp.float32)


def _scan_chunk_states(decay, upd):
    dec = jnp.moveaxis(decay, 2, 0)
    up = jnp.moveaxis(upd, 2, 0)

    def step(state, inp):
        d, u = inp
        return d * state + u, state

    _, prev = lax.scan(step, jnp.zeros(up.shape[1:], up.dtype), (dec, up))
    return jnp.moveaxis(prev, 0, 2)


def forgetting_attention(q, k, v, f_logit, q_gain, k_gain):
    b, s, h, hd = q.shape
    q = rms_norm(q, q_gain) * (hd ** -0.5)
    k = rms_norm(k, k_gain)
    c = jnp.cumsum(jax.nn.log_sigmoid(f_logit.astype(jnp.float32)), axis=1)
    c = jnp.transpose(c, (0, 2, 1))
    qh = jnp.transpose(q, (0, 2, 1, 3))
    kh = jnp.transpose(k, (0, 2, 1, 3))
    vh = jnp.transpose(v, (0, 2, 1, 3))
    nqb = s // Q_BLOCK
    qb = jnp.transpose(qh.reshape(b, h, nqb, Q_BLOCK, hd), (2, 0, 1, 3, 4))
    cb = jnp.transpose(c.reshape(b, h, nqb, Q_BLOCK), (2, 0, 1, 3))
    qpos = jnp.arange(nqb)[:, None] * Q_BLOCK + jnp.arange(Q_BLOCK)[None, :]
    kpos = jnp.arange(s)

    def block(args):
        qi, ci, pi = args
        sc = jnp.einsum('bhqd,bhkd->bhqk', qi, kh).astype(jnp.float32)
        sc = sc + ci[..., None] - c[:, :, None, :]
        sc = jnp.where(pi[:, None] >= kpos[None, :], sc, -jnp.inf)
        p = jax.nn.softmax(sc, axis=-1)
        return jnp.einsum('bhqk,bhkd->bhqd', p.astype(vh.dtype), vh)

    o = lax.map(block, (qb, cb, qpos))
    return jnp.transpose(o, (1, 0, 3, 2, 4)).reshape(b, s, h * hd)


def gla_chunked(q, k, v, log_a):
    b, s, h, dk = q.shape
    dv = v.shape[-1]
    qc = _to_chunks(q) * (dk ** -0.5)
    kc = _to_chunks(k)
    vc = _to_chunks(v)
    g = _to_chunks(log_a)
    bcum = jnp.cumsum(g, axis=3)
    b_last = bcum[:, :, :, -1:, :]
    q_dec = qc * jnp.exp(bcum)
    k_dec = kc * jnp.exp(-bcum)
    causal = jnp.tril(jnp.ones((CHUNK, CHUNK), dtype=bool))
    attn = jnp.where(causal, jnp.einsum('bhctd,bhcsd->bhcts', q_dec, k_dec), 0.0)
    o_intra = jnp.einsum('bhcts,bhcsv->bhctv', attn, vc)
    kv = jnp.einsum('bhcsd,bhcsv->bhcdv', kc * jnp.exp(b_last - bcum), vc)
    prev = _scan_chunk_states(jnp.exp(b_last[:, :, :, 0, :])[..., None], kv)
    o = o_intra + jnp.einsum('bhctd,bhcdv->bhctv', q_dec, prev)
    return jnp.transpose(o, (0, 2, 3, 1, 4)).reshape(b, s, h, dv).astype(q.dtype)


def causal_depthwise_conv(u, w, bias):
    ch = u.shape[-1]
    out = lax.conv_general_dilated(u, w[:, None, :].astype(u.dtype), window_strides=(1,),
                                   padding=[(w.shape[0] - 1, 0)],
                                   dimension_numbers=('NWC', 'WIO', 'NWC'),
                                   feature_group_count=ch)
    return out + bias


def ssd_chunked(x, dt, a, bm, cm):
    hpg = SSM_HEADS // SSM_GROUPS
    bc = _to_chunks(jnp.repeat(bm, hpg, axis=2))
    cc = _to_chunks(jnp.repeat(cm, hpg, axis=2))
    xc = _to_chunks(x)
    dtc = _to_chunks(dt)
    acs = jnp.cumsum(dtc * a[:, None, None], axis=-1)
    causal = jnp.tril(jnp.ones((CHUNK, CHUNK), dtype=bool))
    decay = jnp.exp(jnp.where(causal, acs[..., :, None] - acs[..., None, :], -jnp.inf))
    xdt = xc * dtc[..., None]
    scores = jnp.einsum('bhctn,bhcsn->bhcts', cc, bc) * decay
    y_intra = jnp.einsum('bhcts,bhcsp->bhctp', scores, xdt)
    to_end = jnp.exp(acs[..., -1:] - acs)
    states = jnp.einsum('bhcsn,bhcsp->bhcnp', bc * to_end[..., None], xdt)
    prev = _scan_chunk_states(jnp.exp(acs[..., -1])[..., None, None], states)
    y = y_intra + jnp.einsum('bhctn,bhcnp->bhctp', cc * jnp.exp(acs)[..., None], prev)
    b, h, nc, c, p = y.shape
    return jnp.transpose(y, (0, 2, 3, 1, 4)).reshape(b, nc * c, h, p)


def mamba2_branch(z, xbc, dt_raw, conv_w, conv_b, dt_bias, a_log, d_skip, out_norm):
    b, s, _ = z.shape
    xbc = jax.nn.silu(causal_depthwise_conv(xbc, conv_w, conv_b))
    xs, bm, cm = jnp.split(xbc, [MIX_W, MIX_W + SSM_GROUPS * SSM_STATE], axis=-1)
    xs_h = xs.reshape(b, s, SSM_HEADS, SSM_HD)
    dt = jax.nn.softplus((dt_raw + dt_bias).astype(jnp.float32))
    a = -jnp.exp(a_log.astype(jnp.float32))
    y = ssd_chunked(xs_h, dt, a, bm.reshape(b, s, SSM_GROUPS, SSM_STATE),
                    cm.reshape(b, s, SSM_GROUPS, SSM_STATE))
    y = y + d_skip.astype(jnp.float32)[:, None] * xs_h.astype(jnp.float32)
    y = y.reshape(b, s, MIX_W).astype(z.dtype) * jax.nn.silu(z)
    y = rms_norm(y.reshape(b, s, SSM_GROUPS, MIX_W // SSM_GROUPS),
                 out_norm.reshape(SSM_GROUPS, MIX_W // SSM_GROUPS))
    return y.reshape(b, s, MIX_W)


def hierarchical_moe(h, w_router_grp, b_router_grp, w_router_exp, b_router_exp,
                     w_exp_gate, w_exp_up, w_exp_down):
    b, s, d = h.shape
    t = h.reshape(b * s, d)
    g_prob = jax.nn.softmax((t @ w_router_grp + b_router_grp).astype(jnp.float32), axis=-1)
    g_w, g_sel = lax.top_k(g_prob, 1)
    e_logits = (t @ w_router_exp + b_router_exp).astype(jnp.float32)
    e_logits = e_logits.reshape(-1, N_EGROUPS, EXP_PER_GROUP)
    e_in = jnp.einsum('tg,tge->te', jax.nn.one_hot(g_sel[:, 0], N_EGROUPS, dtype=jnp.float32), e_logits)
    top_w, top_i = lax.top_k(jax.nn.softmax(e_in, axis=-1), TOP_K)
    top_w = top_w / jnp.sum(top_w, axis=-1, keepdims=True) * g_w
    expert_id = g_sel * EXP_PER_GROUP + top_i
    combine = jnp.sum(jax.nn.one_hot(expert_id, N_EXPERTS, dtype=jnp.float32) * top_w[..., None], axis=1)
    combine = combine.astype(t.dtype)
    out = jnp.zeros_like(t)
    for e in range(N_EXPERTS):
        hid = jax.nn.silu(t @ w_exp_gate[e]) * (t @ w_exp_up[e])
        out = out + combine[:, e:e + 1] * (hid @ w_exp_down[e])
    return out.reshape(b, s, d)


def _layer(x, norm_mix, w_in, fox_f_bias, fox_q_norm, fox_k_norm, gla_w_lr, gla_b_gate,
           gla_out_norm, ssm_conv_w, ssm_conv_b, ssm_dt_bias, ssm_a_log, ssm_d, ssm_out_norm,
           w_branch, b_branch_gate, w_out, norm_ffn, w_router_grp, b_router_grp,
           w_router_exp, b_router_exp, w_exp_gate, w_exp_up, w_exp_down):
    b, s, _ = x.shape
    h = rms_norm(x, norm_mix)
    proj = h @ w_in
    sizes = [MIX_W, MIX_W, MIX_W, FOX_HEADS,
             GLA_HEADS * GLA_DK, GLA_HEADS * GLA_DK, MIX_W, MIX_W, GLA_RANK,
             MIX_W, SSM_CONV_DIM, SSM_HEADS, GATE_COLS]
    offsets = [int(o) for o in np.cumsum(sizes)[:-1]]
    (fq, fk, fv, ff, gq, gk, gv, gr, glr, sz, sxbc, sdt, gate_pre) = jnp.split(proj, offsets, axis=-1)
    o_a = forgetting_attention(fq.reshape(b, s, FOX_HEADS, FOX_HD), fk.reshape(b, s, FOX_HEADS, FOX_HD),
                               fv.reshape(b, s, FOX_HEADS, FOX_HD), ff + fox_f_bias,
                               fox_q_norm, fox_k_norm)
    log_a = jax.nn.log_sigmoid((glr @ gla_w_lr + gla_b_gate).astype(jnp.float32)) / GLA_GATE_NORM
    o_b = gla_chunked(gq.reshape(b, s, GLA_HEADS, GLA_DK), gk.reshape(b, s, GLA_HEADS, GLA_DK),
                      gv.reshape(b, s, GLA_HEADS, GLA_DV), log_a.reshape(b, s, GLA_HEADS, GLA_DK))
    o_b = rms_norm(o_b, gla_out_norm).reshape(b, s, MIX_W) * jax.nn.silu(gr)
    o_c = mamba2_branch(sz, sxbc, sdt, ssm_conv_w, ssm_conv_b, ssm_dt_bias, ssm_a_log, ssm_d, ssm_out_norm)
    branches = jnp.stack([o_a, o_b, o_c], axis=2)
    proj_b = jnp.einsum('bsrc,rcd->bsrd', branches, w_branch)
    gates = jax.nn.sigmoid(gate_pre.reshape(b, s, N_BRANCH, D_MODEL) + b_branch_gate)
    x = x + jnp.sum(gates * proj_b, axis=2) @ w_out
    x = x + hierarchical_moe(rms_norm(x, norm_ffn), w_router_grp, b_router_grp, w_router_exp,
                             b_router_exp, w_exp_gate, w_exp_up, w_exp_down)
    return x


def setup_inputs(seed: int = 0) -> dict:
    key = jax.random.key(seed)
    ks = jax.random.split(key, 32)
    L, D = DEPTH, D_MODEL
    f32 = jnp.float32

    def nrm(k, shape, scale):
        return jax.random.normal(k, shape, f32) * scale

    dt = jnp.exp(jax.random.uniform(ks[12], (L, SSM_HEADS), f32) * (math.log(0.1) - math.log(0.001))
                 + math.log(0.001))
    return {
        "x": nrm(ks[0], (BATCH, SEQ, D), 1.0),
        "norm_mix": 1.0 + nrm(ks[1], (L, D), 0.05),
        "w_in": nrm(ks[2], (L, D, IN_COLS), D ** -0.5),
        "fox_f_bias": 2.0 + nrm(ks[3], (L, FOX_HEADS), 0.5),
        "fox_q_norm": 1.0 + nrm(ks[4], (L, FOX_HD), 0.05),
        "fox_k_norm": 1.0 + nrm(ks[5], (L, FOX_HD), 0.05),
        "gla_w_lr": nrm(ks[6], (L, GLA_RANK, GLA_HEADS * GLA_DK), GLA_RANK ** -0.5),
        "gla_b_gate": nrm(ks[7], (L, GLA_HEADS * GLA_DK), 0.1),
        "gla_out_norm": 1.0 + nrm(ks[8], (L, GLA_DV), 0.05),
        "ssm_conv_w": nrm(ks[9], (L, SSM_CONV, SSM_CONV_DIM), SSM_CONV ** -0.5),
        "ssm_conv_b": nrm(ks[10], (L, SSM_CONV_DIM), 0.02),
        "ssm_dt_bias": dt + jnp.log(-jnp.expm1(-dt)),
        "ssm_a_log": jnp.log(jax.random.uniform(ks[11], (L, SSM_HEADS), f32, 1.0, 16.0)),
        "ssm_d": 1.0 + nrm(ks[13], (L, SSM_HEADS), 0.05),
        "ssm_out_norm": 1.0 + nrm(ks[14], (L, MIX_W), 0.05),
        "w_branch": nrm(ks[15], (L, N_BRANCH, MIX_W, D), MIX_W ** -0.5),
        "b_branch_gate": nrm(ks[16], (L, N_BRANCH, D), 0.1),
        "w_out": nrm(ks[17], (L, D, D), D ** -0.5),
        "norm_ffn": 1.0 + nrm(ks[18], (L, D), 0.05),
        "w_router_grp": nrm(ks[19], (L, D, N_EGROUPS), D ** -0.5),
        "b_router_grp": nrm(ks[20], (L, N_EGROUPS), 0.01),
        "w_router_exp": nrm(ks[21], (L, D, N_EXPERTS), D ** -0.5),
        "b_router_exp": nrm(ks[22], (L, N_EXPERTS), 0.01),
        "w_exp_gate": nrm(ks[23], (L, N_EXPERTS, D, D_EXPERT), D ** -0.5),
        "w_exp_up": nrm(ks[24], (L, N_EXPERTS, D, D_EXPERT), D ** -0.5),
        "w_exp_down": nrm(ks[25], (L, N_EXPERTS, D_EXPERT, D), D_EXPERT ** -0.5),
    }


def reference(x, norm_mix, w_in, fox_f_bias, fox_q_norm, fox_k_norm, gla_w_lr, gla_b_gate,
              gla_out_norm, ssm_conv_w, ssm_conv_b, ssm_dt_bias, ssm_a_log, ssm_d, ssm_out_norm,
              w_branch, b_branch_gate, w_out, norm_ffn, w_router_grp, b_router_grp,
              w_router_exp, b_router_exp, w_exp_gate, w_exp_up, w_exp_down):
    for l in range(DEPTH):
        x = _layer(x, norm_mix[l], w_in[l], fox_f_bias[l], fox_q_norm[l], fox_k_norm[l],
                   gla_w_lr[l], gla_b_gate[l], gla_out_norm[l], ssm_conv_w[l], ssm_conv_b[l],
                   ssm_dt_bias[l], ssm_a_log[l], ssm_d[l], ssm_out_norm[l], w_branch[l],
                   b_branch_gate[l], w_out[l], norm_ffn[l], w_router_grp[l], b_router_grp[l],
                   w_router_exp[l], b_router_exp[l], w_exp_gate[l], w_exp_up[l], w_exp_down[l])
    return x
```

```python
import functools

import numpy as np
import jax
import jax.numpy as jnp
from jax import lax
from jax.experimental import pallas as pl
from jax.experimental.pallas import tpu as pltpu

F32 = jnp.float32
BF16 = jnp.bfloat16

D_MODEL = 1024
MIX_W = 512
EPS = 1e-6
FOX_HEADS = 8
FOX_HD = 64
FOX_PAIRS = FOX_HEADS // 2
GLA_HEADS = 4
GLA_DK = 64
GLA_DV = 128
GLA_RANK = 16
GLA_GATE_NORM = 16.0
GLA_CHUNK = 64
SSM_HEADS = 8
SSM_HD = 64
SSM_GROUPS = 2
SSM_STATE = 64
SSM_CONV = 4
N_EGROUPS = 4
EXP_PER_GROUP = 4
N_EXPERTS = 16
D_EXPERT = 512

LANES = 128
NEG_BIG = -1e30
VMEM_LIMIT = 56 * 1024 * 1024

GATE_OFF, FQ_OFF, FK_OFF, FV_OFF = 0, 3072, 3584, 4096
GV_OFF, GR_OFF, SZ_OFF, SX_OFF = 4608, 5120, 5632, 6144
GQ_OFF, GK_OFF, SB_OFF, SC_OFF = 6656, 6912, 7168, 7296
BIG_COLS = 7424
FF_LANE, GLR_LANE, SDT_LANE = 0, 8, 24
_O_FQ, _O_FK, _O_FV, _O_FF = 0, 512, 1024, 1536
_O_GQ, _O_GK, _O_GV, _O_GR, _O_GLR = 1544, 1800, 2056, 2568, 3080
_O_SZ, _O_SX, _O_SB, _O_SC, _O_SDT, _O_GATE = 3096, 3608, 4120, 4248, 4376, 4384
RG_LANE, RE_LANE = 0, 4


def _split2(x):
    hi = x.astype(BF16)
    lo = (x - hi.astype(F32)).astype(BF16)
    return hi, lo


def _split3(x):
    x1 = x.astype(BF16)
    r = x - x1.astype(F32)
    x2 = r.astype(BF16)
    x3 = (r - x2.astype(F32)).astype(BF16)
    return x1, x2, x3


def _dot(a, b):
    return jnp.dot(a, b, preferred_element_type=F32)


def _dot_nt(a, b):
    return lax.dot_general(a, b, (((1,), (1,)), ((), ())), preferred_element_type=F32)


def _dot_tn(a, b):
    return lax.dot_general(a, b, (((0,), (0,)), ((), ())), preferred_element_type=F32)


def _dot3_left(m_bf16, x_f32):
    x1, x2, x3 = _split3(x_f32)
    return _dot(m_bf16, x1) + _dot(m_bf16, x2) + _dot(m_bf16, x3)


def _dot3_right(x_f32, m_bf16):
    x1, x2, x3 = _split3(x_f32)
    return _dot(x1, m_bf16) + _dot(x2, m_bf16) + _dot(x3, m_bf16)


def _dot_f32w(x_f32, w_hi, w_lo):
    x_hi, x_lo = _split2(x_f32)
    return _dot(x_hi, w_hi) + _dot(x_lo, w_hi) + _dot(x_hi, w_lo)


def _shr(x, pow2):
    return jnp.right_shift(x, pow2.bit_length() - 1)


def _log_sigmoid(x):
    return jnp.minimum(x, 0.0) - jnp.log1p(jnp.exp(-jnp.abs(x)))


def _softplus(x):
    return jnp.maximum(x, 0.0) + jnp.log1p(jnp.exp(-jnp.abs(x)))


def _silu(x):
    return x * (1.0 / (1.0 + jnp.exp(-x)))


def _sigmoid(x):
    return 1.0 / (1.0 + jnp.exp(-x))


def _params(sem):
    return pltpu.CompilerParams(dimension_semantics=sem, vmem_limit_bytes=VMEM_LIMIT)


def _inproj_kernel(x_ref, g_ref, w_ref, wsh_ref, wsl_ref, big_ref, small_ref, *, tn):
    x = x_ref[...]
    ms = jnp.mean(x * x, axis=-1, keepdims=True)
    h = x * lax.rsqrt(ms + EPS) * g_ref[...]
    hb = h.astype(BF16)
    for c in range(BIG_COLS // tn):
        cols = slice(c * tn, (c + 1) * tn)
        big_ref[:, cols] = _dot(hb, w_ref[:, cols]).astype(BF16)
    h_lo = (h - hb.astype(F32)).astype(BF16)
    wsh = wsh_ref[...]
    small_ref[...] = _dot(hb, wsh) + _dot(h_lo, wsh) + _dot(hb, wsl_ref[...])


def _inproj(x2, gain, w_big, ws_hi, ws_lo, *, tm=512, tn=256):
    t = x2.shape[0]
    return pl.pallas_call(
        functools.partial(_inproj_kernel, tn=tn),
        grid=(t // tm,),
        in_specs=[
            pl.BlockSpec((tm, D_MODEL), lambda i: (i, 0)),
            pl.BlockSpec((1, D_MODEL), lambda i: (0, 0)),
            pl.BlockSpec((D_MODEL, BIG_COLS), lambda i: (0, 0), pipeline_mode=pl.Buffered(1)),
            pl.BlockSpec((D_MODEL, LANES), lambda i: (0, 0)),
            pl.BlockSpec((D_MODEL, LANES), lambda i: (0, 0)),
        ],
        out_specs=[
            pl.BlockSpec((tm, BIG_COLS), lambda i: (i, 0)),
            pl.BlockSpec((tm, LANES), lambda i: (i, 0)),
        ],
        out_shape=[
            jax.ShapeDtypeStruct((t, BIG_COLS), BF16),
            jax.ShapeDtypeStruct((t, LANES), F32),
        ],
        compiler_params=_params(("arbitrary",)),
        name="inproj",
    )(x2, gain, w_big, ws_hi, ws_lo)


def _fox_consts(ts):
    ltri = np.tril(np.ones((ts, ts), np.float32))
    hsum = np.kron(np.eye(FOX_HEADS, dtype=np.float32), np.ones((FOX_HD, FOX_HD), np.float32))
    sq = np.zeros((3, LANES, MIX_W), np.float32)
    sk = np.zeros((3, LANES, MIX_W), np.float32)
    oneq = np.zeros((1, MIX_W), np.float32)
    onek = np.zeros((1, MIX_W), np.float32)
    for h in range(FOX_HEADS):
        base = LANES * (h // 2) + 6 * (h % 2)
        for j in range(3):
            sq[j, FF_LANE + h, base + j] = 1.0
            sk[j, FF_LANE + h, base + 3 + j] = -1.0
            oneq[0, base + 3 + j] = 1.0
            onek[0, base + j] = 1.0
    return (jnp.asarray(ltri, BF16), jnp.asarray(hsum, BF16), jnp.asarray(sq, BF16),
            jnp.asarray(sk, BF16), jnp.asarray(oneq), jnp.asarray(onek))


def _fox_prep_kernel(fq_ref, fk_ref, small_ref, fbias_ref, qg_ref, kg_ref, ltri_ref, hsum_ref,
                     sq_ref, sk_ref, oneq_ref, onek_ref, qf_ref, kf_ref, carry_ref, *, ts):
    @pl.when(pl.program_id(1) == 0)
    def _():
        carry_ref[...] = jnp.zeros_like(carry_ref)

    lane = lax.broadcasted_iota(jnp.int32, (ts, LANES), 1)
    f = small_ref[0] + fbias_ref[...]
    ls = jnp.where(lane < FOX_HEADS, _log_sigmoid(f), 0.0)
    c = _dot3_left(ltri_ref[...], ls) + carry_ref[0:1, :]
    carry_ref[...] = jnp.broadcast_to(c[ts - 1:ts, :], carry_ref.shape)
    c1, c2, c3 = _split3(c)
    qaug = _dot(c1, sq_ref[0]) + _dot(c2, sq_ref[1]) + _dot(c3, sq_ref[2]) + oneq_ref[...]
    kaug = _dot(c1, sk_ref[0]) + _dot(c2, sk_ref[1]) + _dot(c3, sk_ref[2]) + onek_ref[...]

    hsum = hsum_ref[...]

    def head_norm(xb, gain):
        x = xb.astype(F32)
        s_hi, s_lo = _split2(x * x)
        ss = _dot(s_hi, hsum) + _dot(s_lo, hsum)
        return x * lax.rsqrt(ss * (1.0 / FOX_HD) + EPS) * gain

    qn = head_norm(fq_ref[0], qg_ref[...]) * (FOX_HD ** -0.5)
    kn = head_norm(fk_ref[0], kg_ref[...])
    for p in range(FOX_PAIRS):
        src = slice(LANES * p, LANES * (p + 1))
        dst_x = slice(2 * LANES * p, 2 * LANES * p + LANES)
        dst_a = slice(2 * LANES * p + LANES, 2 * LANES * (p + 1))
        qf_ref[0, :, dst_x] = qn[:, src].astype(BF16)
        qf_ref[0, :, dst_a] = qaug[:, src].astype(BF16)
        kf_ref[0, :, dst_x] = kn[:, src].astype(BF16)
        kf_ref[0, :, dst_a] = kaug[:, src].astype(BF16)


def _fox_prep(big3, small3, fbias, qgain, kgain, *, ts=512):
    b, s, _ = big3.shape
    ts = min(ts, s)
    consts = _fox_consts(ts)
    const_specs = [
        pl.BlockSpec((ts, ts), lambda bi, i: (0, 0)),
        pl.BlockSpec((MIX_W, MIX_W), lambda bi, i: (0, 0)),
        pl.BlockSpec((3, LANES, MIX_W), lambda bi, i: (0, 0, 0)),
        pl.BlockSpec((3, LANES, MIX_W), lambda bi, i: (0, 0, 0)),
        pl.BlockSpec((1, MIX_W), lambda bi, i: (0, 0)),
        pl.BlockSpec((1, MIX_W), lambda bi, i: (0, 0)),
    ]
    return pl.pallas_call(
        functools.partial(_fox_prep_kernel, ts=ts),
        grid=(b, s // ts),
        in_specs=[
            pl.BlockSpec((1, ts, MIX_W), lambda bi, i: (bi, i, FQ_OFF // MIX_W)),
            pl.BlockSpec((1, ts, MIX_W), lambda bi, i: (bi, i, FK_OFF // MIX_W)),
            pl.BlockSpec((1, ts, LANES), lambda bi, i: (bi, i, 0)),
            pl.BlockSpec((1, LANES), lambda bi, i: (0, 0)),
            pl.BlockSpec((1, MIX_W), lambda bi, i: (0, 0)),
            pl.BlockSpec((1, MIX_W), lambda bi, i: (0, 0)),
        ] + const_specs,
        out_specs=[
            pl.BlockSpec((1, ts, 2 * MIX_W), lambda bi, i: (bi, i, 0)),
            pl.BlockSpec((1, ts, 2 * MIX_W), lambda bi, i: (bi, i, 0)),
        ],
        out_shape=[
            jax.ShapeDtypeStruct((b, s, 2 * MIX_W), BF16),
            jax.ShapeDtypeStruct((b, s, 2 * MIX_W), BF16),
        ],
        scratch_shapes=[pltpu.VMEM((8, LANES), F32)],
        compiler_params=_params(("arbitrary", "arbitrary")),
        name="fox_prep",
    )(big3, big3, small3, fbias, qgain, kgain, *consts)


def _fox_attn_kernel(q_ref, k_ref, v_ref, o_ref, m_ref, l_ref, acc_ref, *, tq):
    i = pl.program_id(2)
    q = q_ref[0]
    qlane = lax.broadcasted_iota(jnp.int32, (1, 2 * LANES), 1)
    in_a = (qlane < FOX_HD) | ((qlane >= LANES) & (qlane < LANES + 6))
    in_b = ((qlane >= FOX_HD) & (qlane < LANES)) | ((qlane >= LANES + 6) & (qlane < LANES + 12))
    zero = jnp.zeros_like(q)
    q_heads = (jnp.where(in_a, q, zero), jnp.where(in_b, q, zero))
    vlane = lax.broadcasted_iota(jnp.int32, (1, LANES), 1)
    first_head = vlane < FOX_HD

    m_ref[...] = jnp.full_like(m_ref, NEG_BIG)
    l_ref[...] = jnp.zeros_like(l_ref)
    acc_ref[...] = jnp.zeros_like(acc_ref)

    def step(j, diag):
        row0 = pl.multiple_of(j * tq, tq)
        k = k_ref[0, pl.ds(row0, tq), :]
        v = v_ref[0, pl.ds(row0, tq), :]
        vzero = jnp.zeros_like(v)
        v_heads = (jnp.where(first_head, v, vzero), jnp.where(first_head, vzero, v))
        if diag:
            r = lax.broadcasted_iota(jnp.int32, (tq, tq), 0)
            c = lax.broadcasted_iota(jnp.int32, (tq, tq), 1)
            keep = c <= r
        pv = None
        alphas = []
        for a in range(2):
            s = _dot_nt(q_heads[a], k)
            if diag:
                s = jnp.where(keep, s, NEG_BIG)
            m_prev = m_ref[a]
            m_new = jnp.maximum(m_prev, jnp.max(s, axis=-1, keepdims=True))
            p = jnp.exp(s - m_new)
            alpha = jnp.exp(m_prev - m_new)
            l_ref[a] = alpha * l_ref[a] + jnp.sum(p, axis=-1, keepdims=True)
            m_ref[a] = m_new
            alphas.append(alpha)
            contrib = _dot(p.astype(BF16), v_heads[a])
            pv = contrib if pv is None else pv + contrib
        alpha2 = jnp.where(first_head, alphas[0], alphas[1])
        acc_ref[...] = alpha2 * acc_ref[...] + pv

    def body(j, carry):
        step(j, False)
        return carry

    lax.fori_loop(0, i, body, 0)
    step(i, True)
    inv = jnp.where(first_head, 1.0 / l_ref[0], 1.0 / l_ref[1])
    o_ref[0] = (acc_ref[...] * inv).astype(BF16)


def _fox_attn(qf, kf, big3, *, tq=512):
    b, s, _ = qf.shape
    tq = min(tq, s)
    return pl.pallas_call(
        functools.partial(_fox_attn_kernel, tq=tq),
        grid=(b, FOX_PAIRS, s // tq),
        in_specs=[
            pl.BlockSpec((1, tq, 2 * LANES), lambda bi, p, i: (bi, i, p)),
            pl.BlockSpec((1, s, 2 * LANES), lambda bi, p, i: (bi, 0, p)),
            pl.BlockSpec((1, s, LANES), lambda bi, p, i: (bi, 0, FV_OFF // LANES + p)),
        ],
        out_specs=pl.BlockSpec((1, tq, LANES), lambda bi, p, i: (bi, i, p)),
        out_shape=jax.ShapeDtypeStruct((b, s, MIX_W), BF16),
        scratch_shapes=[
            pltpu.VMEM((2, tq, 1), F32),
            pltpu.VMEM((2, tq, 1), F32),
            pltpu.VMEM((tq, LANES), F32),
        ],
        compiler_params=_params(("arbitrary", "arbitrary", "arbitrary")),
        name="fox_attn",
    )(qf, kf, big3)


def _gla_consts(r):
    idx = np.arange(r)
    same = (idx[:, None] // GLA_CHUNK) == (idx[None, :] // GLA_CHUNK)
    lblk = (same & (idx[None, :] <= idx[:, None])).astype(np.float32)
    ablk = same.astype(np.float32)
    return jnp.asarray(lblk, BF16), jnp.asarray(ablk, BF16)


def _gla_kernel(q_ref, k_ref, v_ref, r_ref, small_ref, wlh_ref, wll_ref, bg_ref, lblk_ref,
                ablk_ref, gain_ref, o_ref, st_ref, oacc_ref, *, r):
    @pl.when(pl.program_id(1) == 0)
    def _():
        st_ref[...] = jnp.zeros_like(st_ref)

    kw = GLA_HEADS * GLA_DK
    gate = _dot_f32w(small_ref[0], wlh_ref[...], wll_ref[...]) + bg_ref[...]
    log_a = _log_sigmoid(gate) * (1.0 / GLA_GATE_NORM)
    a1, a2, a3 = _split3(log_a)
    lblk = lblk_ref[...]
    ablk = ablk_ref[...]
    bcum = _dot(lblk, a1) + _dot(lblk, a2) + _dot(lblk, a3)
    btot = _dot(ablk, a1) + _dot(ablk, a2) + _dot(ablk, a3)
    q = q_ref[0].astype(F32) * (GLA_DK ** -0.5)
    k = k_ref[0].astype(F32)
    q_dec = (q * jnp.exp(bcum)).astype(BF16)
    k_dec = (k * jnp.exp(-bcum)).astype(BF16)
    k_end = (k * jnp.exp(btot - bcum)).astype(BF16)
    d_tot = jnp.exp(btot)
    v = v_ref[0]

    row = lax.broadcasted_iota(jnp.int32, (r, r), 0)
    col = lax.broadcasted_iota(jnp.int32, (r, r), 1)
    keep = (_shr(row, GLA_CHUNK) == _shr(col, GLA_CHUNK)) & (col <= row)
    klane = lax.broadcasted_iota(jnp.int32, (1, kw), 1)
    qzero = jnp.zeros_like(q_dec)
    for h in range(GLA_HEADS):
        in_h = (klane >= GLA_DK * h) & (klane < GLA_DK * (h + 1))
        att = _dot_nt(jnp.where(in_h, q_dec, qzero), k_dec)
        att = jnp.where(keep, att, 0.0).astype(BF16)
        vcols = slice(GLA_DV * h, GLA_DV * (h + 1))
        oacc_ref[:, vcols] = _dot(att, v[:, vcols])

    srow = lax.broadcasted_iota(jnp.int32, (MIX_W, kw), 0)
    scol = lax.broadcasted_iota(jnp.int32, (MIX_W, kw), 1)
    same_head = _shr(srow, GLA_DV) == _shr(scol, GLA_DK)
    for c in range(r // GLA_CHUNK):
        rows = slice(GLA_CHUNK * c, GLA_CHUNK * (c + 1))
        st = st_ref[...]
        oacc_ref[rows, :] += _dot_nt(q_dec[rows], st.astype(BF16))
        kv_t = _dot_tn(v[rows], k_end[rows])
        st_ref[...] = d_tot[GLA_CHUNK * c:GLA_CHUNK * c + 1, :] * st + jnp.where(same_head, kv_t, 0.0)

    gain = gain_ref[...]
    gr = r_ref[0].astype(F32)
    for h in range(GLA_HEADS):
        vcols = slice(GLA_DV * h, GLA_DV * (h + 1))
        o = oacc_ref[:, vcols]
        ms = jnp.mean(o * o, axis=-1, keepdims=True)
        o_ref[0, :, vcols] = (o * lax.rsqrt(ms + EPS) * gain * _silu(gr[:, vcols])).astype(BF16)


def _gla(big3, small3, wl_hi, wl_lo, bgate, gain, *, r=256):
    b, s, _ = big3.shape
    r = min(r, s)
    kw = GLA_HEADS * GLA_DK
    lblk, ablk = _gla_consts(r)
    return pl.pallas_call(
        functools.partial(_gla_kernel, r=r),
        grid=(b, s // r),
        in_specs=[
            pl.BlockSpec((1, r, kw), lambda bi, i: (bi, i, GQ_OFF // kw)),
            pl.BlockSpec((1, r, kw), lambda bi, i: (bi, i, GK_OFF // kw)),
            pl.BlockSpec((1, r, MIX_W), lambda bi, i: (bi, i, GV_OFF // MIX_W)),
            pl.BlockSpec((1, r, MIX_W), lambda bi, i: (bi, i, GR_OFF // MIX_W)),
            pl.BlockSpec((1, r, LANES), lambda bi, i: (bi, i, 0)),
            pl.BlockSpec((LANES, kw), lambda bi, i: (0, 0)),
            pl.BlockSpec((LANES, kw), lambda bi, i: (0, 0)),
            pl.BlockSpec((1, kw), lambda bi, i: (0, 0)),
            pl.BlockSpec((r, r), lambda bi, i: (0, 0)),
            pl.BlockSpec((r, r), lambda bi, i: (0, 0)),
            pl.BlockSpec((1, GLA_DV), lambda bi, i: (0, 0)),
        ],
        out_specs=pl.BlockSpec((1, r, MIX_W), lambda bi, i: (bi, i, 0)),
        out_shape=jax.ShapeDtypeStruct((b, s, MIX_W), BF16),
        scratch_shapes=[pltpu.VMEM((MIX_W, kw), F32), pltpu.VMEM((r, MIX_W), F32)],
        compiler_params=_params(("arbitrary", "arbitrary")),
        name="gla",
    )(big3, big3, big3, big3, small3, wl_hi, wl_lo, bgate, lblk, ablk, gain)


SSD_HALO = 16
SSD_CONV_W = MIX_W + 2 * SSM_GROUPS * SSM_STATE


def _ssd_consts(r):
    expand = np.zeros((LANES, MIX_W), np.float32)
    for h in range(SSM_HEADS):
        expand[SDT_LANE + h, SSM_HD * h:SSM_HD * (h + 1)] = 1.0
    rep = np.zeros((LANES, MIX_W), np.float32)
    for h in range(SSM_HEADS):
        g = h // (SSM_HEADS // SSM_GROUPS)
        for n in range(SSM_STATE):
            rep[SSM_STATE * g + n, SSM_HD * h + n] = 1.0
    ltri = np.tril(np.ones((r, r), np.float32))
    return (jnp.asarray(expand, BF16), jnp.asarray(rep, BF16), jnp.asarray(ltri, BF16),
            jnp.asarray(ltri.T, BF16))


def _ssd_kernel(z_ref, x_ref, xp_ref, b_ref, bp_ref, c_ref, cp_ref, small_ref, cw_ref, cb_ref,
                dtb_ref, alog_ref, dtbc_ref, alogc_ref, dskip_ref, onorm_ref, expand_ref,
                rep_ref, ltri_ref, utri_ref, o_ref, sw_ref, ext_ref, y_ref, *, r):
    first = pl.program_id(1) == 0

    @pl.when(first)
    def _():
        sw_ref[...] = jnp.zeros_like(sw_ref)

    keep_prev = jnp.where(first, 0.0, 1.0)
    ext_ref[0:SSD_HALO, 0:MIX_W] = xp_ref[0].astype(F32) * keep_prev
    ext_ref[0:SSD_HALO, MIX_W:MIX_W + LANES] = bp_ref[0].astype(F32) * keep_prev
    ext_ref[0:SSD_HALO, MIX_W + LANES:SSD_CONV_W] = cp_ref[0].astype(F32) * keep_prev
    ext_ref[SSD_HALO:, 0:MIX_W] = x_ref[0].astype(F32)
    ext_ref[SSD_HALO:, MIX_W:MIX_W + LANES] = b_ref[0].astype(F32)
    ext_ref[SSD_HALO:, MIX_W + LANES:SSD_CONV_W] = c_ref[0].astype(F32)
    conv = cb_ref[...] + cw_ref[SSM_CONV - 1:SSM_CONV, :] * ext_ref[SSD_HALO:, :]
    for back in range(1, SSM_CONV):
        tap = SSM_CONV - 1 - back
        conv = conv + cw_ref[tap:tap + 1, :] * ext_ref[pl.ds(SSD_HALO - back, r), :]
    xbc = _silu(conv)
    xs = xbc[:, 0:MIX_W]
    bm = xbc[:, MIX_W:MIX_W + LANES].astype(BF16)
    cm = xbc[:, MIX_W + LANES:SSD_CONV_W].astype(BF16)

    sm = small_ref[0]
    dt = _softplus(_dot3_right(sm, expand_ref[...]) + dtb_ref[...])
    a_neg = -jnp.exp(alog_ref[...])
    acs = _dot3_left(ltri_ref[...], dt * a_neg)
    acs_last = acs[r - 1:r, :]
    sm_t = sm.T
    dt_t = _softplus(sm_t[SDT_LANE:SDT_LANE + SSM_HEADS, :] + dtbc_ref[:, 0:1])
    acs_t = _dot3_right(dt_t * (-jnp.exp(alogc_ref[:, 0:1])), utri_ref[...])

    xdt = (xs * dt).astype(BF16)
    row = lax.broadcasted_iota(jnp.int32, (r, r), 0)
    col = lax.broadcasted_iota(jnp.int32, (r, r), 1)
    causal = col <= row
    glane = lax.broadcasted_iota(jnp.int32, (1, LANES), 1)
    first_half = glane < SSM_STATE
    czero = jnp.zeros_like(cm)
    hpg = SSM_HEADS // SSM_GROUPS
    for g in range(SSM_GROUPS):
        cg = jnp.where(first_half if g == 0 else ~first_half, cm, czero)
        cb_g = _dot_nt(cg, bm)
        for pair in range(hpg // 2):
            p = g * (hpg // 2) + pair
            xp = xdt[:, LANES * p:LANES * (p + 1)]
            xzero = jnp.zeros_like(xp)
            acc = None
            for a in range(2):
                h = 2 * p + a
                dmat = acs[:, SSM_HD * h:SSM_HD * h + 1] - acs_t[h:h + 1, :]
                sc = (cb_g * jnp.exp(jnp.where(causal, dmat, NEG_BIG))).astype(BF16)
                xh = jnp.where(first_half if a == 0 else ~first_half, xp, xzero)
                contrib = _dot(sc, xh)
                acc = contrib if acc is None else acc + contrib
            y_ref[:, LANES * p:LANES * (p + 1)] = acc

    rep = rep_ref[...]
    sw = sw_ref[...]
    cw = (_dot(cm, rep) * jnp.exp(acs)).astype(BF16)
    y = y_ref[...] + _dot(cw, sw.astype(BF16)) + dskip_ref[...] * xs
    bw = (_dot(bm, rep) * jnp.exp(acs_last - acs)).astype(BF16)
    upd = _dot_tn(bw, xdt)
    srow = lax.broadcasted_iota(jnp.int32, (MIX_W, MIX_W), 0)
    scol = lax.broadcasted_iota(jnp.int32, (MIX_W, MIX_W), 1)
    same_head = _shr(srow, SSM_STATE) == _shr(scol, SSM_HD)
    sw_ref[...] = sw * jnp.exp(acs_last) + jnp.where(same_head, upd, 0.0)

    y = y * _silu(z_ref[0].astype(F32))
    gw = MIX_W // SSM_GROUPS
    for g in range(SSM_GROUPS):
        cols = slice(gw * g, gw * (g + 1))
        yg = y[:, cols]
        ms = jnp.mean(yg * yg, axis=-1, keepdims=True)
        o_ref[0, :, cols] = (yg * lax.rsqrt(ms + EPS) * onorm_ref[:, cols]).astype(BF16)


def _ssd(big3, small3, conv_w, conv_b, dtb_w, alog_w, dtb_c, alog_c, dskip_w, onorm, *, r=256):
    b, s, _ = big3.shape
    r = min(r, s)
    consts = _ssd_consts(r)
    hb = r // SSD_HALO

    def cur(width, off):
        return pl.BlockSpec((1, r, width), lambda bi, i: (bi, i, off // width))

    def prev(width, off):
        return pl.BlockSpec((1, SSD_HALO, width),
                            lambda bi, i: (bi, jnp.maximum(i * hb - 1, 0), off // width))

    def whole(shape):
        return pl.BlockSpec(shape, lambda bi, i: (0,) * len(shape))

    return pl.pallas_call(
        functools.partial(_ssd_kernel, r=r),
        grid=(b, s // r),
        in_specs=[
            cur(MIX_W, SZ_OFF),
            cur(MIX_W, SX_OFF), prev(MIX_W, SX_OFF),
            cur(LANES, SB_OFF), prev(LANES, SB_OFF),
            cur(LANES, SC_OFF), prev(LANES, SC_OFF),
            pl.BlockSpec((1, r, LANES), lambda bi, i: (bi, i, 0)),
            whole((SSM_CONV, SSD_CONV_W)), whole((1, SSD_CONV_W)),
            whole((1, MIX_W)), whole((1, MIX_W)),
            whole((SSM_HEADS, LANES)), whole((SSM_HEADS, LANES)),
            whole((1, MIX_W)), whole((1, MIX_W)),
            whole((LANES, MIX_W)), whole((LANES, MIX_W)), whole((r, r)), whole((r, r)),
        ],
        out_specs=pl.BlockSpec((1, r, MIX_W), lambda bi, i: (bi, i, 0)),
        out_shape=jax.ShapeDtypeStruct((b, s, MIX_W), BF16),
        scratch_shapes=[
            pltpu.VMEM((MIX_W, MIX_W), F32),
            pltpu.VMEM((r + SSD_HALO, SSD_CONV_W), F32),
            pltpu.VMEM((r, MIX_W), F32),
        ],
        compiler_params=_params(("arbitrary", "arbitrary")),
        name="ssd",
    )(big3, big3, big3, big3, big3, big3, big3, small3, conv_w, conv_b, dtb_w, alog_w,
      dtb_c, alog_c, dskip_w, onorm, *consts)


def _merge_kernel(oa_ref, ob_ref, oc_ref, gate_ref, x_ref, wb_ref, bgate_ref, wo_ref, nffn_ref,
                  wrh_ref, wrl_ref, br_ref, xn_ref, h_ref, comb_ref, *, tm):
    mixed = None
    for ridx, o_ref in enumerate((oa_ref, ob_ref, oc_ref)):
        cols = slice(D_MODEL * ridx, D_MODEL * (ridx + 1))
        gate = _sigmoid(gate_ref[:, cols].astype(F32) + bgate_ref[ridx:ridx + 1, :])
        term = gate * _dot(o_ref[...], wb_ref[ridx])
        mixed = term if mixed is None else mixed + term
    xn = x_ref[...] + _dot(mixed.astype(BF16), wo_ref[...])
    xn_ref[...] = xn
    ms = jnp.mean(xn * xn, axis=-1, keepdims=True)
    h = xn * lax.rsqrt(ms + EPS) * nffn_ref[...]
    h_ref[...] = h.astype(BF16)

    logits = _dot_f32w(h, wrh_ref[...], wrl_ref[...]) + br_ref[...]
    lane = lax.broadcasted_iota(jnp.int32, (tm, LANES), 1)
    lane_f = lane.astype(F32)
    big_lane = float(LANES)
    is_grp = lane < RE_LANE
    gl = jnp.where(is_grp, logits, NEG_BIG)
    gmax = jnp.max(gl, axis=-1, keepdims=True)
    g_w = 1.0 / jnp.sum(jnp.exp(gl - gmax), axis=-1, keepdims=True)
    g_sel = jnp.min(jnp.where(gl == gmax, lane_f, big_lane), axis=-1, keepdims=True)
    grp_of_lane = _shr(jnp.maximum(lane - RE_LANE, 0), EXP_PER_GROUP).astype(F32)
    in_grp = (lane >= RE_LANE) & (lane < RE_LANE + N_EXPERTS) & (grp_of_lane == g_sel)
    el = jnp.where(in_grp, logits, NEG_BIG)
    e1 = jnp.max(el, axis=-1, keepdims=True)
    i1 = jnp.min(jnp.where(in_grp & (el == e1), lane_f, big_lane), axis=-1, keepdims=True)
    rest = in_grp & (lane_f != i1)
    el2 = jnp.where(rest, logits, NEG_BIG)
    e2 = jnp.max(el2, axis=-1, keepdims=True)
    i2 = jnp.min(jnp.where(rest & (el2 == e2), lane_f, big_lane), axis=-1, keepdims=True)
    ratio = jnp.exp(e2 - e1)
    w1 = g_w / (1.0 + ratio)
    w2 = w1 * ratio
    comb_ref[...] = jnp.where(lane_f == i1, w1, 0.0) + jnp.where(lane_f == i2, w2, 0.0)


def _merge(oa, ob, oc, big, x2, wb, bgate, wo, nffn, wr_hi, wr_lo, br, *, tm=512):
    t = x2.shape[0]
    tm = min(tm, t)

    def whole(shape):
        return pl.BlockSpec(shape, lambda i: (0,) * len(shape))

    return pl.pallas_call(
        functools.partial(_merge_kernel, tm=tm),
        grid=(t // tm,),
        in_specs=[
            pl.BlockSpec((tm, MIX_W), lambda i: (i, 0)),
            pl.BlockSpec((tm, MIX_W), lambda i: (i, 0)),
            pl.BlockSpec((tm, MIX_W), lambda i: (i, 0)),
            pl.BlockSpec((tm, 3 * D_MODEL), lambda i: (i, 0)),
            pl.BlockSpec((tm, D_MODEL), lambda i: (i, 0)),
            whole((3, MIX_W, D_MODEL)), whole((3, D_MODEL)), whole((D_MODEL, D_MODEL)),
            whole((1, D_MODEL)), whole((D_MODEL, LANES)), whole((D_MODEL, LANES)),
            whole((1, LANES)),
        ],
        out_specs=[
            pl.BlockSpec((tm, D_MODEL), lambda i: (i, 0)),
            pl.BlockSpec((tm, D_MODEL), lambda i: (i, 0)),
            pl.BlockSpec((tm, LANES), lambda i: (i, 0)),
        ],
        out_shape=[
            jax.ShapeDtypeStruct((t, D_MODEL), F32),
            jax.ShapeDtypeStruct((t, D_MODEL), BF16),
            jax.ShapeDtypeStruct((t, LANES), F32),
        ],
        compiler_params=_params(("arbitrary",)),
        name="merge",
    )(oa, ob, oc, big, x2, wb, bgate, wo, nffn, wr_hi, wr_lo, br)


def _moe_kernel(h_ref, comb_ref, x_ref, wg_ref, wu_ref, wd_ref, o_ref, *, tm):
    e = pl.program_id(1)

    @pl.when(e == 0)
    def _():
        o_ref[...] = x_ref[...]

    lane = lax.broadcasted_iota(jnp.int32, (tm, LANES), 1)
    w = jnp.sum(jnp.where(lane == e + RE_LANE, comb_ref[...], 0.0), axis=-1, keepdims=True)
    h = h_ref[...]
    hid = _silu(_dot(h, wg_ref[0])) * _dot(h, wu_ref[0])
    o_ref[...] += w * _dot(hid.astype(BF16), wd_ref[0])


def _moe(h, comb, xn, wg, wu, wd, *, tm=512):
    t = h.shape[0]
    tm = min(tm, t)
    return pl.pallas_call(
        functools.partial(_moe_kernel, tm=tm),
        grid=(t // tm, N_EXPERTS),
        in_specs=[
            pl.BlockSpec((tm, D_MODEL), lambda i, e: (i, 0)),
            pl.BlockSpec((tm, LANES), lambda i, e: (i, 0)),
            pl.BlockSpec((tm, D_MODEL), lambda i, e: (i, 0)),
            pl.BlockSpec((1, D_MODEL, D_EXPERT), lambda i, e: (e, 0, 0)),
            pl.BlockSpec((1, D_MODEL, D_EXPERT), lambda i, e: (e, 0, 0)),
            pl.BlockSpec((1, D_EXPERT, D_MODEL), lambda i, e: (e, 0, 0)),
        ],
        out_specs=pl.BlockSpec((tm, D_MODEL), lambda i, e: (i, 0)),
        out_shape=jax.ShapeDtypeStruct((t, D_MODEL), F32),
        compiler_params=_params(("arbitrary", "arbitrary")),
        name="moe",
    )(h, comb, xn, wg, wu, wd)


def _cols(w, off, width):
    return w[:, off:off + width]


def _pad_lanes(v, lane0, width=LANES):
    out = jnp.zeros((1, width), F32)
    return out.at[0, lane0:lane0 + v.shape[0]].set(v.astype(F32))


def _layer(x2, b, s, norm_mix, w_in, fox_f_bias, fox_q_norm, fox_k_norm, gla_w_lr, gla_b_gate,
           gla_out_norm, ssm_conv_w, ssm_conv_b, ssm_dt_bias, ssm_a_log, ssm_d, ssm_out_norm,
           w_branch, b_branch_gate, w_out, norm_ffn, w_router_grp, b_router_grp,
           w_router_exp, b_router_exp, w_exp_gate, w_exp_up, w_exp_down):
    t = b * s
    w_big = jnp.concatenate([
        _cols(w_in, _O_GATE, 3 * D_MODEL), _cols(w_in, _O_FQ, MIX_W), _cols(w_in, _O_FK, MIX_W),
        _cols(w_in, _O_FV, MIX_W), _cols(w_in, _O_GV, MIX_W), _cols(w_in, _O_GR, MIX_W),
        _cols(w_in, _O_SZ, MIX_W), _cols(w_in, _O_SX, MIX_W), _cols(w_in, _O_GQ, 256),
        _cols(w_in, _O_GK, 256), _cols(w_in, _O_SB, LANES), _cols(w_in, _O_SC, LANES),
    ], axis=1).astype(BF16)
    w_small = jnp.concatenate([
        _cols(w_in, _O_FF, FOX_HEADS), _cols(w_in, _O_GLR, GLA_RANK), _cols(w_in, _O_SDT, SSM_HEADS),
        jnp.zeros((D_MODEL, LANES - FOX_HEADS - GLA_RANK - SSM_HEADS), F32),
    ], axis=1)
    ws_hi, ws_lo = _split2(w_small)

    big, small = _inproj(x2, norm_mix.reshape(1, D_MODEL), w_big, ws_hi, ws_lo)
    big3 = big.reshape(b, s, BIG_COLS)
    small3 = small.reshape(b, s, LANES)

    qf, kf = _fox_prep(big3, small3, _pad_lanes(fox_f_bias, FF_LANE),
                       jnp.tile(fox_q_norm, FOX_HEADS).reshape(1, MIX_W),
                       jnp.tile(fox_k_norm, FOX_HEADS).reshape(1, MIX_W))
    o_a = _fox_attn(qf, kf, big3)

    wl = jnp.zeros((LANES, GLA_HEADS * GLA_DK), F32).at[GLR_LANE:GLR_LANE + GLA_RANK].set(gla_w_lr)
    wl_hi, wl_lo = _split2(wl)
    o_b = _gla(big3, small3, wl_hi, wl_lo, gla_b_gate.reshape(1, -1),
               gla_out_norm.reshape(1, GLA_DV))

    o_c = _ssd(big3, small3, ssm_conv_w, ssm_conv_b.reshape(1, -1),
               jnp.repeat(ssm_dt_bias, SSM_HD).reshape(1, MIX_W),
               jnp.repeat(ssm_a_log, SSM_HD).reshape(1, MIX_W),
               jnp.broadcast_to(ssm_dt_bias[:, None], (SSM_HEADS, LANES)),
               jnp.broadcast_to(ssm_a_log[:, None], (SSM_HEADS, LANES)),
               jnp.repeat(ssm_d, SSM_HD).reshape(1, MIX_W),
               ssm_out_norm.reshape(1, MIX_W))

    w_r = jnp.concatenate([w_router_grp, w_router_exp,
                           jnp.zeros((D_MODEL, LANES - N_EGROUPS - N_EXPERTS), F32)], axis=1)
    wr_hi, wr_lo = _split2(w_r)
    b_r = jnp.concatenate([b_router_grp, b_router_exp,
                           jnp.zeros((LANES - N_EGROUPS - N_EXPERTS,), F32)]).reshape(1, LANES)
    xn, h, comb = _merge(o_a.reshape(t, MIX_W), o_b.reshape(t, MIX_W), o_c.reshape(t, MIX_W),
                         big, x2, w_branch.astype(BF16), b_branch_gate, w_out.astype(BF16),
                         norm_ffn.reshape(1, D_MODEL), wr_hi, wr_lo, b_r)

    return _moe(h, comb, xn, w_exp_gate.astype(BF16), w_exp_up.astype(BF16),
                w_exp_down.astype(BF16))


def kernel(x, norm_mix, w_in, fox_f_bias, fox_q_norm, fox_k_norm, gla_w_lr, gla_b_gate, gla_out_norm, ssm_conv_w, ssm_conv_b, ssm_dt_bias, ssm_a_log, ssm_d, ssm_out_norm, w_branch, b_branch_gate, w_out, norm_ffn, w_router_grp, b_router_grp, w_router_exp, b_router_exp, w_exp_gate, w_exp_up, w_exp_down):
    b, s, d = x.shape
    x2 = x.reshape(b * s, d)
    per_layer = (norm_mix, w_in, fox_f_bias, fox_q_norm, fox_k_norm, gla_w_lr, gla_b_gate,
                 gla_out_norm, ssm_conv_w, ssm_conv_b, ssm_dt_bias, ssm_a_log, ssm_d,
                 ssm_out_norm, w_branch, b_branch_gate, w_out, norm_ffn, w_router_grp,
                 b_router_grp, w_router_exp, b_router_exp, w_exp_gate, w_exp_up, w_exp_down)
    for l in range(norm_mix.shape[0]):
        x2 = _layer(x2, b, s, *[p[l] for p in per_layer])
    return x2.reshape(b, s, d)
```

```python
import functools

import numpy as np
import jax
import jax.numpy as jnp
from jax import lax
from jax.experimental import pallas as pl
from jax.experimental.pallas import tpu as pltpu

F32 = jnp.float32
BF16 = jnp.bfloat16

D_MODEL = 1024
MIX_W = 512
EPS = 1e-6
FOX_HEADS = 8
FOX_HD = 64
FOX_PAIRS = FOX_HEADS // 2
GLA_HEADS = 4
GLA_DK = 64
GLA_DV = 128
GLA_RANK = 16
GLA_GATE_NORM = 16.0
GLA_CHUNK = 64
SSM_HEADS = 8
SSM_HD = 64
SSM_GROUPS = 2
SSM_STATE = 64
SSM_CONV = 4
N_EGROUPS = 4
EXP_PER_GROUP = 4
N_EXPERTS = 16
D_EXPERT = 512

LANES = 128
NEG_BIG = -1e30
LOG2E = 1.4426950408889634
VMEM_LIMIT = 56 * 1024 * 1024

GATE_OFF, FQ_OFF, FK_OFF, FV_OFF = 0, 3072, 3584, 4096
GV_OFF, GR_OFF, SZ_OFF, SX_OFF = 4608, 5120, 5632, 6144
GQ_OFF, GK_OFF, SB_OFF, SC_OFF = 6656, 6912, 7168, 7296
BIG_COLS = 7424
FF_LANE, GLR_LANE, SDT_LANE = 0, 8, 24
_O_FQ, _O_FK, _O_FV, _O_FF = 0, 512, 1024, 1536
_O_GQ, _O_GK, _O_GV, _O_GR, _O_GLR = 1544, 1800, 2056, 2568, 3080
_O_SZ, _O_SX, _O_SB, _O_SC, _O_SDT, _O_GATE = 3096, 3608, 4120, 4248, 4376, 4384
RG_LANE, RE_LANE = 0, 4


def _split2(x):
    hi = x.astype(BF16)
    lo = (x - hi.astype(F32)).astype(BF16)
    return hi, lo


def _split3(x):
    x1 = x.astype(BF16)
    r = x - x1.astype(F32)
    x2 = r.astype(BF16)
    x3 = (r - x2.astype(F32)).astype(BF16)
    return x1, x2, x3


def _dot(a, b):
    return jnp.dot(a, b, preferred_element_type=F32)


def _dot_nt(a, b):
    return lax.dot_general(a, b, (((1,), (1,)), ((), ())), preferred_element_type=F32)


def _dot_tn(a, b):
    return lax.dot_general(a, b, (((0,), (0,)), ((), ())), preferred_element_type=F32)


def _dot3_left(m_bf16, x_f32):
    x1, x2, x3 = _split3(x_f32)
    return _dot(m_bf16, x1) + _dot(m_bf16, x2) + _dot(m_bf16, x3)


def _dot3_right(x_f32, m_bf16):
    x1, x2, x3 = _split3(x_f32)
    return _dot(x1, m_bf16) + _dot(x2, m_bf16) + _dot(x3, m_bf16)


def _dot_f32w(x_f32, w_hi, w_lo):
    x_hi, x_lo = _split2(x_f32)
    return _dot(x_hi, w_hi) + _dot(x_lo, w_hi) + _dot(x_hi, w_lo)


def _shr(x, pow2):
    return jnp.right_shift(x, pow2.bit_length() - 1)


def _log_sigmoid(x):
    return jnp.minimum(x, 0.0) - jnp.log1p(jnp.exp(-jnp.abs(x)))


def _softplus(x):
    return jnp.maximum(x, 0.0) + jnp.log1p(jnp.exp(-jnp.abs(x)))


def _silu(x):
    return x * (1.0 / (1.0 + jnp.exp(-x)))


def _sigmoid(x):
    return 1.0 / (1.0 + jnp.exp(-x))


def _params(sem):
    return pltpu.CompilerParams(dimension_semantics=sem, vmem_limit_bytes=VMEM_LIMIT)


def _inproj_kernel(x_ref, g_ref, w_ref, wsh_ref, wsl_ref, big_ref, small_ref, *, tn):
    x = x_ref[...]
    ms = jnp.mean(x * x, axis=-1, keepdims=True)
    h = x * lax.rsqrt(ms + EPS) * g_ref[...]
    hb = h.astype(BF16)
    for c in range(BIG_COLS // tn):
        cols = slice(c * tn, (c + 1) * tn)
        big_ref[:, cols] = _dot(hb, w_ref[:, cols]).astype(BF16)
    h_lo = (h - hb.astype(F32)).astype(BF16)
    wsh = wsh_ref[...]
    small_ref[...] = _dot(hb, wsh) + _dot(h_lo, wsh) + _dot(hb, wsl_ref[...])


def _inproj(x2, gain, w_big, ws_hi, ws_lo, *, tm=512, tn=256):
    t = x2.shape[0]
    return pl.pallas_call(
        functools.partial(_inproj_kernel, tn=tn),
        grid=(t // tm,),
        in_specs=[
            pl.BlockSpec((tm, D_MODEL), lambda i: (i, 0)),
            pl.BlockSpec((1, D_MODEL), lambda i: (0, 0)),
            pl.BlockSpec((D_MODEL, BIG_COLS), lambda i: (0, 0), pipeline_mode=pl.Buffered(1)),
            pl.BlockSpec((D_MODEL, LANES), lambda i: (0, 0)),
            pl.BlockSpec((D_MODEL, LANES), lambda i: (0, 0)),
        ],
        out_specs=[
            pl.BlockSpec((tm, BIG_COLS), lambda i: (i, 0)),
            pl.BlockSpec((tm, LANES), lambda i: (i, 0)),
        ],
        out_shape=[
            jax.ShapeDtypeStruct((t, BIG_COLS), BF16),
            jax.ShapeDtypeStruct((t, LANES), F32),
        ],
        compiler_params=_params(("arbitrary",)),
        name="inproj",
    )(x2, gain, w_big, ws_hi, ws_lo)


def _fox_consts(ts):
    ltri = np.tril(np.ones((ts, ts), np.float32))
    hsum = np.kron(np.eye(FOX_HEADS, dtype=np.float32), np.ones((FOX_HD, FOX_HD), np.float32))
    sq = np.zeros((3, LANES, MIX_W), np.float32)
    sk = np.zeros((3, LANES, MIX_W), np.float32)
    oneq = np.zeros((1, MIX_W), np.float32)
    onek = np.zeros((1, MIX_W), np.float32)
    for h in range(FOX_HEADS):
        base = LANES * (h // 2) + 6 * (h % 2)
        for j in range(3):
            sq[j, FF_LANE + h, base + j] = 1.0
            sk[j, FF_LANE + h, base + 3 + j] = -1.0
            oneq[0, base + 3 + j] = 1.0
            onek[0, base + j] = 1.0
    return (jnp.asarray(ltri, BF16), jnp.asarray(hsum, BF16), jnp.asarray(sq, BF16),
            jnp.asarray(sk, BF16), jnp.asarray(oneq), jnp.asarray(onek))


def _fox_prep_kernel(fq_ref, fk_ref, small_ref, fbias_ref, qg_ref, kg_ref, ltri_ref, hsum_ref,
                     sq_ref, sk_ref, oneq_ref, onek_ref, qf_ref, kf_ref, carry_ref, *, ts):
    @pl.when(pl.program_id(1) == 0)
    def _():
        carry_ref[...] = jnp.zeros_like(carry_ref)

    lane = lax.broadcasted_iota(jnp.int32, (ts, LANES), 1)
    f = small_ref[0] + fbias_ref[...]
    ls = jnp.where(lane < FOX_HEADS, _log_sigmoid(f) * LOG2E, 0.0)
    c = _dot3_left(ltri_ref[...], ls) + carry_ref[0:1, :]
    carry_ref[...] = jnp.broadcast_to(c[ts - 1:ts, :], carry_ref.shape)
    c1, c2, c3 = _split3(c)
    qaug = _dot(c1, sq_ref[0]) + _dot(c2, sq_ref[1]) + _dot(c3, sq_ref[2]) + oneq_ref[...]
    kaug = _dot(c1, sk_ref[0]) + _dot(c2, sk_ref[1]) + _dot(c3, sk_ref[2]) + onek_ref[...]

    hsum = hsum_ref[...]

    def head_norm(xb, gain):
        x = xb.astype(F32)
        s_hi, s_lo = _split2(x * x)
        ss = _dot(s_hi, hsum) + _dot(s_lo, hsum)
        return x * lax.rsqrt(ss * (1.0 / FOX_HD) + EPS) * gain

    qn = head_norm(fq_ref[0], qg_ref[...]) * (FOX_HD ** -0.5 * LOG2E)
    kn = head_norm(fk_ref[0], kg_ref[...])
    for p in range(FOX_PAIRS):
        src = slice(LANES * p, LANES * (p + 1))
        dst_x = slice(2 * LANES * p, 2 * LANES * p + LANES)
        dst_a = slice(2 * LANES * p + LANES, 2 * LANES * (p + 1))
        qf_ref[0, :, dst_x] = qn[:, src].astype(BF16)
        qf_ref[0, :, dst_a] = qaug[:, src].astype(BF16)
        kf_ref[0, :, dst_x] = kn[:, src].astype(BF16)
        kf_ref[0, :, dst_a] = kaug[:, src].astype(BF16)


def _fox_prep(big3, small3, fbias, qgain, kgain, *, ts=512):
    b, s, _ = big3.shape
    ts = min(ts, s)
    consts = _fox_consts(ts)
    const_specs = [
        pl.BlockSpec((ts, ts), lambda bi, i: (0, 0)),
        pl.BlockSpec((MIX_W, MIX_W), lambda bi, i: (0, 0)),
        pl.BlockSpec((3, LANES, MIX_W), lambda bi, i: (0, 0, 0)),
        pl.BlockSpec((3, LANES, MIX_W), lambda bi, i: (0, 0, 0)),
        pl.BlockSpec((1, MIX_W), lambda bi, i: (0, 0)),
        pl.BlockSpec((1, MIX_W), lambda bi, i: (0, 0)),
    ]
    return pl.pallas_call(
        functools.partial(_fox_prep_kernel, ts=ts),
        grid=(b, s // ts),
        in_specs=[
            pl.BlockSpec((1, ts, MIX_W), lambda bi, i: (bi, i, FQ_OFF // MIX_W)),
            pl.BlockSpec((1, ts, MIX_W), lambda bi, i: (bi, i, FK_OFF // MIX_W)),
            pl.BlockSpec((1, ts, LANES), lambda bi, i: (bi, i, 0)),
            pl.BlockSpec((1, LANES), lambda bi, i: (0, 0)),
            pl.BlockSpec((1, MIX_W), lambda bi, i: (0, 0)),
            pl.BlockSpec((1, MIX_W), lambda bi, i: (0, 0)),
        ] + const_specs,
        out_specs=[
            pl.BlockSpec((1, ts, 2 * MIX_W), lambda bi, i: (bi, i, 0)),
            pl.BlockSpec((1, ts, 2 * MIX_W), lambda bi, i: (bi, i, 0)),
        ],
        out_shape=[
            jax.ShapeDtypeStruct((b, s, 2 * MIX_W), BF16),
            jax.ShapeDtypeStruct((b, s, 2 * MIX_W), BF16),
        ],
        scratch_shapes=[pltpu.VMEM((8, LANES), F32)],
        compiler_params=_params(("arbitrary", "arbitrary")),
        name="fox_prep",
    )(big3, big3, small3, fbias, qgain, kgain, *consts)


FOX_NOSHIFT_BOUND = 40.0


def _fox_q_heads(q):
    qlane = lax.broadcasted_iota(jnp.int32, (1, 2 * LANES), 1)
    in_a = (qlane < FOX_HD) | ((qlane >= LANES) & (qlane < LANES + 6))
    in_b = ((qlane >= FOX_HD) & (qlane < LANES)) | ((qlane >= LANES + 6) & (qlane < LANES + 12))
    zero = jnp.zeros_like(q)
    return jnp.where(in_a, q, zero), jnp.where(in_b, q, zero)


def _causal_keep(tq):
    r = lax.broadcasted_iota(jnp.int32, (tq, tq), 0)
    c = lax.broadcasted_iota(jnp.int32, (tq, tq), 1)
    return c <= r


def _fox_attn_noshift_kernel(q_ref, k_ref, v_ref, o_ref, acc_ref, *, tq):
    i = pl.program_id(2)
    q_heads = _fox_q_heads(q_ref[0])
    vlane = lax.broadcasted_iota(jnp.int32, (1, LANES), 1)
    first_head = vlane < FOX_HD
    ones_a = jnp.where(vlane == FOX_HD, 1.0, 0.0).astype(BF16)
    ones_b = jnp.where(vlane == 0, 1.0, 0.0).astype(BF16)
    acc_ref[...] = jnp.zeros_like(acc_ref)

    def step(j, diag):
        row0 = pl.multiple_of(j * tq, tq)
        k = k_ref[0, pl.ds(row0, tq), :]
        v = v_ref[0, pl.ds(row0, tq), :]
        v_heads = (jnp.where(first_head, v, ones_a), jnp.where(first_head, ones_b, v))
        for a in range(2):
            s = _dot_nt(q_heads[a], k)
            if diag:
                s = jnp.where(_causal_keep(tq), s, NEG_BIG)
            acc_ref[a] += _dot(jnp.exp2(s).astype(BF16), v_heads[a])

    def body(j, carry):
        step(j, False)
        return carry

    lax.fori_loop(0, i, body, 0)
    step(i, True)
    acc_a = acc_ref[0]
    acc_b = acc_ref[1]
    o = jnp.where(first_head, acc_a * (1.0 / acc_a[:, FOX_HD:FOX_HD + 1]),
                  acc_b * (1.0 / acc_b[:, 0:1]))
    o_ref[0] = o.astype(BF16)


def _fox_attn_online_kernel(q_ref, k_ref, v_ref, o_ref, m_ref, l_ref, acc_ref, *, tq):
    i = pl.program_id(2)
    q_heads = _fox_q_heads(q_ref[0])
    vlane = lax.broadcasted_iota(jnp.int32, (1, LANES), 1)
    first_head = vlane < FOX_HD

    m_ref[...] = jnp.full_like(m_ref, NEG_BIG)
    l_ref[...] = jnp.zeros_like(l_ref)
    acc_ref[...] = jnp.zeros_like(acc_ref)

    def step(j, diag):
        row0 = pl.multiple_of(j * tq, tq)
        k = k_ref[0, pl.ds(row0, tq), :]
        v = v_ref[0, pl.ds(row0, tq), :]
        vzero = jnp.zeros_like(v)
        v_heads = (jnp.where(first_head, v, vzero), jnp.where(first_head, vzero, v))
        pv = None
        alphas = []
        for a in range(2):
            s = _dot_nt(q_heads[a], k)
            if diag:
                s = jnp.where(_causal_keep(tq), s, NEG_BIG)
            m_prev = m_ref[a]
            m_new = jnp.maximum(m_prev, jnp.max(s, axis=-1, keepdims=True))
            p = jnp.exp2(s - m_new)
            alpha = jnp.exp2(m_prev - m_new)
            l_ref[a] = alpha * l_ref[a] + jnp.sum(p, axis=-1, keepdims=True)
            m_ref[a] = m_new
            alphas.append(alpha)
            contrib = _dot(p.astype(BF16), v_heads[a])
            pv = contrib if pv is None else pv + contrib
        alpha2 = jnp.where(first_head, alphas[0], alphas[1])
        acc_ref[...] = alpha2 * acc_ref[...] + pv

    def body(j, carry):
        step(j, False)
        return carry

    lax.fori_loop(0, i, body, 0)
    step(i, True)
    inv = jnp.where(first_head, 1.0 / l_ref[0], 1.0 / l_ref[1])
    o_ref[0] = (acc_ref[...] * inv).astype(BF16)


def _fox_attn_call(body, scratch, name, qf, kf, big3, tq):
    b, s, _ = qf.shape
    return pl.pallas_call(
        functools.partial(body, tq=tq),
        grid=(b, FOX_PAIRS, s // tq),
        in_specs=[
            pl.BlockSpec((1, tq, 2 * LANES), lambda bi, p, i: (bi, i, p)),
            pl.BlockSpec((1, s, 2 * LANES), lambda bi, p, i: (bi, 0, p)),
            pl.BlockSpec((1, s, LANES), lambda bi, p, i: (bi, 0, FV_OFF // LANES + p)),
        ],
        out_specs=pl.BlockSpec((1, tq, LANES), lambda bi, p, i: (bi, i, p)),
        out_shape=jax.ShapeDtypeStruct((b, s, MIX_W), BF16),
        scratch_shapes=scratch,
        compiler_params=_params(("arbitrary", "arbitrary", "arbitrary")),
        name=name,
    )(qf, kf, big3)


def _fox_attn(qf, kf, big3, logit_bound, *, tq=512):
    tq = min(tq, qf.shape[1])

    def noshift():
        return _fox_attn_call(_fox_attn_noshift_kernel, [pltpu.VMEM((2, tq, LANES), F32)],
                              "fox_attn", qf, kf, big3, tq)

    def online():
        scratch = [pltpu.VMEM((2, tq, 1), F32), pltpu.VMEM((2, tq, 1), F32),
                   pltpu.VMEM((tq, LANES), F32)]
        return _fox_attn_call(_fox_attn_online_kernel, scratch, "fox_attn_online", qf, kf, big3, tq)

    return lax.cond(logit_bound < FOX_NOSHIFT_BOUND, noshift, online)


def _gla_consts(r):
    idx = np.arange(r)
    same = (idx[:, None] // GLA_CHUNK) == (idx[None, :] // GLA_CHUNK)
    lblk = (same & (idx[None, :] <= idx[:, None])).astype(np.float32)
    ablk = same.astype(np.float32)
    return jnp.asarray(lblk, BF16), jnp.asarray(ablk, BF16)


def _gla_kernel(q_ref, k_ref, v_ref, r_ref, small_ref, wlh_ref, wll_ref, bg_ref, lblk_ref,
                ablk_ref, gain_ref, o_ref, st_ref, oacc_ref, *, r):
    @pl.when(pl.program_id(1) == 0)
    def _():
        st_ref[...] = jnp.zeros_like(st_ref)

    kw = GLA_HEADS * GLA_DK
    gate = _dot_f32w(small_ref[0], wlh_ref[...], wll_ref[...]) + bg_ref[...]
    log_a = _log_sigmoid(gate) * (1.0 / GLA_GATE_NORM)
    a1, a2, a3 = _split3(log_a)
    lblk = lblk_ref[...]
    ablk = ablk_ref[...]
    bcum = _dot(lblk, a1) + _dot(lblk, a2) + _dot(lblk, a3)
    btot = _dot(ablk, a1) + _dot(ablk, a2) + _dot(ablk, a3)
    q = q_ref[0].astype(F32) * (GLA_DK ** -0.5)
    k = k_ref[0].astype(F32)
    q_dec = (q * jnp.exp(bcum)).astype(BF16)
    k_dec = (k * jnp.exp(-bcum)).astype(BF16)
    k_end = (k * jnp.exp(btot - bcum)).astype(BF16)
    d_tot = jnp.exp(btot)
    v = v_ref[0]

    row = lax.broadcasted_iota(jnp.int32, (r, r), 0)
    col = lax.broadcasted_iota(jnp.int32, (r, r), 1)
    keep = (_shr(row, GLA_CHUNK) == _shr(col, GLA_CHUNK)) & (col <= row)
    klane = lax.broadcasted_iota(jnp.int32, (1, kw), 1)
    qzero = jnp.zeros_like(q_dec)
    for h in range(GLA_HEADS):
        in_h = (klane >= GLA_DK * h) & (klane < GLA_DK * (h + 1))
        att = _dot_nt(jnp.where(in_h, q_dec, qzero), k_dec)
        att = jnp.where(keep, att, 0.0).astype(BF16)
        vcols = slice(GLA_DV * h, GLA_DV * (h + 1))
        oacc_ref[:, vcols] = _dot(att, v[:, vcols])

    srow = lax.broadcasted_iota(jnp.int32, (MIX_W, kw), 0)
    scol = lax.broadcasted_iota(jnp.int32, (MIX_W, kw), 1)
    same_head = _shr(srow, GLA_DV) == _shr(scol, GLA_DK)
    for c in range(r // GLA_CHUNK):
        rows = slice(GLA_CHUNK * c, GLA_CHUNK * (c + 1))
        st = st_ref[...]
        oacc_ref[rows, :] += _dot_nt(q_dec[rows], st.astype(BF16))
        kv_t = _dot_tn(v[rows], k_end[rows])
        st_ref[...] = d_tot[GLA_CHUNK * c:GLA_CHUNK * c + 1, :] * st + jnp.where(same_head, kv_t, 0.0)

    gain = gain_ref[...]
    gr = r_ref[0].astype(F32)
    for h in range(GLA_HEADS):
        vcols = slice(GLA_DV * h, GLA_DV * (h + 1))
        o = oacc_ref[:, vcols]
        ms = jnp.mean(o * o, axis=-1, keepdims=True)
        o_ref[0, :, vcols] = (o * lax.rsqrt(ms + EPS) * gain * _silu(gr[:, vcols])).astype(BF16)


def _gla(big3, small3, wl_hi, wl_lo, bgate, gain, *, r=256):
    b, s, _ = big3.shape
    r = min(r, s)
    kw = GLA_HEADS * GLA_DK
    lblk, ablk = _gla_consts(r)
    return pl.pallas_call(
        functools.partial(_gla_kernel, r=r),
        grid=(b, s // r),
        in_specs=[
            pl.BlockSpec((1, r, kw), lambda bi, i: (bi, i, GQ_OFF // kw)),
            pl.BlockSpec((1, r, kw), lambda bi, i: (bi, i, GK_OFF // kw)),
            pl.BlockSpec((1, r, MIX_W), lambda bi, i: (bi, i, GV_OFF // MIX_W)),
            pl.BlockSpec((1, r, MIX_W), lambda bi, i: (bi, i, GR_OFF // MIX_W)),
            pl.BlockSpec((1, r, LANES), lambda bi, i: (bi, i, 0)),
            pl.BlockSpec((LANES, kw), lambda bi, i: (0, 0)),
            pl.BlockSpec((LANES, kw), lambda bi, i: (0, 0)),
            pl.BlockSpec((1, kw), lambda bi, i: (0, 0)),
            pl.BlockSpec((r, r), lambda bi, i: (0, 0)),
            pl.BlockSpec((r, r), lambda bi, i: (0, 0)),
            pl.BlockSpec((1, GLA_DV), lambda bi, i: (0, 0)),
        ],
        out_specs=pl.BlockSpec((1, r, MIX_W), lambda bi, i: (bi, i, 0)),
        out_shape=jax.ShapeDtypeStruct((b, s, MIX_W), BF16),
        scratch_shapes=[pltpu.VMEM((MIX_W, kw), F32), pltpu.VMEM((r, MIX_W), F32)],
        compiler_params=_params(("arbitrary", "arbitrary")),
        name="gla",
    )(big3, big3, big3, big3, small3, wl_hi, wl_lo, bgate, lblk, ablk, gain)


SSD_HALO = 16
SSD_CONV_W = MIX_W + 2 * SSM_GROUPS * SSM_STATE


def _ssd_consts(r):
    expand = np.zeros((LANES, MIX_W), np.float32)
    for h in range(SSM_HEADS):
        expand[SDT_LANE + h, SSM_HD * h:SSM_HD * (h + 1)] = 1.0
    rep = np.zeros((LANES, MIX_W), np.float32)
    for h in range(SSM_HEADS):
        g = h // (SSM_HEADS // SSM_GROUPS)
        for n in range(SSM_STATE):
            rep[SSM_STATE * g + n, SSM_HD * h + n] = 1.0
    ltri = np.tril(np.ones((r, r), np.float32))
    return (jnp.asarray(expand, BF16), jnp.asarray(rep, BF16), jnp.asarray(ltri, BF16),
            jnp.asarray(ltri.T, BF16))


def _ssd_kernel(z_ref, x_ref, xp_ref, b_ref, bp_ref, c_ref, cp_ref, small_ref, cw_ref, cb_ref,
                dtb_ref, alog_ref, dtbc_ref, alogc_ref, dskip_ref, onorm_ref, expand_ref,
                rep_ref, ltri_ref, utri_ref, o_ref, sw_ref, ext_ref, y_ref, *, r):
    first = pl.program_id(1) == 0

    @pl.when(first)
    def _():
        sw_ref[...] = jnp.zeros_like(sw_ref)

    keep_prev = jnp.where(first, 0.0, 1.0)
    ext_ref[0:SSD_HALO, 0:MIX_W] = xp_ref[0].astype(F32) * keep_prev
    ext_ref[0:SSD_HALO, MIX_W:MIX_W + LANES] = bp_ref[0].astype(F32) * keep_prev
    ext_ref[0:SSD_HALO, MIX_W + LANES:SSD_CONV_W] = cp_ref[0].astype(F32) * keep_prev
    ext_ref[SSD_HALO:, 0:MIX_W] = x_ref[0].astype(F32)
    ext_ref[SSD_HALO:, MIX_W:MIX_W + LANES] = b_ref[0].astype(F32)
    ext_ref[SSD_HALO:, MIX_W + LANES:SSD_CONV_W] = c_ref[0].astype(F32)
    conv = cb_ref[...] + cw_ref[SSM_CONV - 1:SSM_CONV, :] * ext_ref[SSD_HALO:, :]
    for back in range(1, SSM_CONV):
        tap = SSM_CONV - 1 - back
        conv = conv + cw_ref[tap:tap + 1, :] * ext_ref[pl.ds(SSD_HALO - back, r), :]
    xbc = _silu(conv)
    xs = xbc[:, 0:MIX_W]
    bm = xbc[:, MIX_W:MIX_W + LANES].astype(BF16)
    cm = xbc[:, MIX_W + LANES:SSD_CONV_W].astype(BF16)

    sm = small_ref[0]
    dt = _softplus(_dot3_right(sm, expand_ref[...]) + dtb_ref[...])
    a_neg = -jnp.exp(alog_ref[...])
    acs = _dot3_left(ltri_ref[...], dt * a_neg)
    acs_last = acs[r - 1:r, :]
    sm_t = sm.T
    dt_t = _softplus(sm_t[SDT_LANE:SDT_LANE + SSM_HEADS, :] + dtbc_ref[:, 0:1])
    acs_t = _dot3_right(dt_t * (-jnp.exp(alogc_ref[:, 0:1])), utri_ref[...])

    xdt = (xs * dt).astype(BF16)
    row = lax.broadcasted_iota(jnp.int32, (r, r), 0)
    col = lax.broadcasted_iota(jnp.int32, (r, r), 1)
    causal = col <= row
    glane = lax.broadcasted_iota(jnp.int32, (1, LANES), 1)
    first_half = glane < SSM_STATE
    czero = jnp.zeros_like(cm)
    hpg = SSM_HEADS // SSM_GROUPS
    for g in range(SSM_GROUPS):
        cg = jnp.where(first_half if g == 0 else ~first_half, cm, czero)
        cb_g = _dot_nt(cg, bm)
        for pair in range(hpg // 2):
            p = g * (hpg // 2) + pair
            xp = xdt[:, LANES * p:LANES * (p + 1)]
            xzero = jnp.zeros_like(xp)
            acc = None
            for a in range(2):
                h = 2 * p + a
                dmat = acs[:, SSM_HD * h:SSM_HD * h + 1] - acs_t[h:h + 1, :]
                sc = (cb_g * jnp.exp(jnp.where(causal, dmat, NEG_BIG))).astype(BF16)
                xh = jnp.where(first_half if a == 0 else ~first_half, xp, xzero)
                contrib = _dot(sc, xh)
                acc = contrib if acc is None else acc + contrib
            y_ref[:, LANES * p:LANES * (p + 1)] = acc

    rep = rep_ref[...]
    sw = sw_ref[...]
    cw = (_dot(cm, rep) * jnp.exp(acs)).astype(BF16)
    y = y_ref[...] + _dot(cw, sw.astype(BF16)) + dskip_ref[...] * xs
    bw = (_dot(bm, rep) * jnp.exp(acs_last - acs)).astype(BF16)
    upd = _dot_tn(bw, xdt)
    srow = lax.broadcasted_iota(jnp.int32, (MIX_W, MIX_W), 0)
    scol = lax.broadcasted_iota(jnp.int32, (MIX_W, MIX_W), 1)
    same_head = _shr(srow, SSM_STATE) == _shr(scol, SSM_HD)
    sw_ref[...] = sw * jnp.exp(acs_last) + jnp.where(same_head, upd, 0.0)

    y = y * _silu(z_ref[0].astype(F32))
    gw = MIX_W // SSM_GROUPS
    for g in range(SSM_GROUPS):
        cols = slice(gw * g, gw * (g + 1))
        yg = y[:, cols]
        ms = jnp.mean(yg * yg, axis=-1, keepdims=True)
        o_ref[0, :, cols] = (yg * lax.rsqrt(ms + EPS) * onorm_ref[:, cols]).astype(BF16)


def _ssd(big3, small3, conv_w, conv_b, dtb_w, alog_w, dtb_c, alog_c, dskip_w, onorm, *, r=256):
    b, s, _ = big3.shape
    r = min(r, s)
    consts = _ssd_consts(r)
    hb = r // SSD_HALO

    def cur(width, off):
        return pl.BlockSpec((1, r, width), lambda bi, i: (bi, i, off // width))

    def prev(width, off):
        return pl.BlockSpec((1, SSD_HALO, width),
                            lambda bi, i: (bi, jnp.maximum(i * hb - 1, 0), off // width))

    def whole(shape):
        return pl.BlockSpec(shape, lambda bi, i: (0,) * len(shape))

    return pl.pallas_call(
        functools.partial(_ssd_kernel, r=r),
        grid=(b, s // r),
        in_specs=[
            cur(MIX_W, SZ_OFF),
            cur(MIX_W, SX_OFF), prev(MIX_W, SX_OFF),
            cur(LANES, SB_OFF), prev(LANES, SB_OFF),
            cur(LANES, SC_OFF), prev(LANES, SC_OFF),
            pl.BlockSpec((1, r, LANES), lambda bi, i: (bi, i, 0)),
            whole((SSM_CONV, SSD_CONV_W)), whole((1, SSD_CONV_W)),
            whole((1, MIX_W)), whole((1, MIX_W)),
            whole((SSM_HEADS, LANES)), whole((SSM_HEADS, LANES)),
            whole((1, MIX_W)), whole((1, MIX_W)),
            whole((LANES, MIX_W)), whole((LANES, MIX_W)), whole((r, r)), whole((r, r)),
        ],
        out_specs=pl.BlockSpec((1, r, MIX_W), lambda bi, i: (bi, i, 0)),
        out_shape=jax.ShapeDtypeStruct((b, s, MIX_W), BF16),
        scratch_shapes=[
            pltpu.VMEM((MIX_W, MIX_W), F32),
            pltpu.VMEM((r + SSD_HALO, SSD_CONV_W), F32),
            pltpu.VMEM((r, MIX_W), F32),
        ],
        compiler_params=_params(("arbitrary", "arbitrary")),
        name="ssd",
    )(big3, big3, big3, big3, big3, big3, big3, small3, conv_w, conv_b, dtb_w, alog_w,
      dtb_c, alog_c, dskip_w, onorm, *consts)


def _merge_kernel(oa_ref, ob_ref, oc_ref, gate_ref, x_ref, wb_ref, bgate_ref, wo_ref, nffn_ref,
                  wrh_ref, wrl_ref, br_ref, xn_ref, h_ref, comb_ref, *, tm):
    mixed = None
    for ridx, o_ref in enumerate((oa_ref, ob_ref, oc_ref)):
        cols = slice(D_MODEL * ridx, D_MODEL * (ridx + 1))
        gate = _sigmoid(gate_ref[:, cols].astype(F32) + bgate_ref[ridx:ridx + 1, :])
        term = gate * _dot(o_ref[...], wb_ref[ridx])
        mixed = term if mixed is None else mixed + term
    xn = x_ref[...] + _dot(mixed.astype(BF16), wo_ref[...])
    xn_ref[...] = xn
    ms = jnp.mean(xn * xn, axis=-1, keepdims=True)
    h = xn * lax.rsqrt(ms + EPS) * nffn_ref[...]
    h_ref[...] = h.astype(BF16)

    logits = _dot_f32w(h, wrh_ref[...], wrl_ref[...]) + br_ref[...]
    lane = lax.broadcasted_iota(jnp.int32, (tm, LANES), 1)
    lane_f = lane.astype(F32)
    big_lane = float(LANES)
    is_grp = lane < RE_LANE
    gl = jnp.where(is_grp, logits, NEG_BIG)
    gmax = jnp.max(gl, axis=-1, keepdims=True)
    g_w = 1.0 / jnp.sum(jnp.exp(gl - gmax), axis=-1, keepdims=True)
    g_sel = jnp.min(jnp.where(gl == gmax, lane_f, big_lane), axis=-1, keepdims=True)
    grp_of_lane = _shr(jnp.maximum(lane - RE_LANE, 0), EXP_PER_GROUP).astype(F32)
    in_grp = (lane >= RE_LANE) & (lane < RE_LANE + N_EXPERTS) & (grp_of_lane == g_sel)
    el = jnp.where(in_grp, logits, NEG_BIG)
    e1 = jnp.max(el, axis=-1, keepdims=True)
    i1 = jnp.min(jnp.where(in_grp & (el == e1), lane_f, big_lane), axis=-1, keepdims=True)
    rest = in_grp & (lane_f != i1)
    el2 = jnp.where(rest, logits, NEG_BIG)
    e2 = jnp.max(el2, axis=-1, keepdims=True)
    i2 = jnp.min(jnp.where(rest & (el2 == e2), lane_f, big_lane), axis=-1, keepdims=True)
    ratio = jnp.exp(e2 - e1)
    w1 = g_w / (1.0 + ratio)
    w2 = w1 * ratio
    comb_ref[...] = jnp.where(lane_f == i1, w1, 0.0) + jnp.where(lane_f == i2, w2, 0.0)


def _merge(oa, ob, oc, big, x2, wb, bgate, wo, nffn, wr_hi, wr_lo, br, *, tm=512):
    t = x2.shape[0]
    tm = min(tm, t)

    def whole(shape):
        return pl.BlockSpec(shape, lambda i: (0,) * len(shape))

    return pl.pallas_call(
        functools.partial(_merge_kernel, tm=tm),
        grid=(t // tm,),
        in_specs=[
            pl.BlockSpec((tm, MIX_W), lambda i: (i, 0)),
            pl.BlockSpec((tm, MIX_W), lambda i: (i, 0)),
            pl.BlockSpec((tm, MIX_W), lambda i: (i, 0)),
            pl.BlockSpec((tm, 3 * D_MODEL), lambda i: (i, 0)),
            pl.BlockSpec((tm, D_MODEL), lambda i: (i, 0)),
            whole((3, MIX_W, D_MODEL)), whole((3, D_MODEL)), whole((D_MODEL, D_MODEL)),
            whole((1, D_MODEL)), whole((D_MODEL, LANES)), whole((D_MODEL, LANES)),
            whole((1, LANES)),
        ],
        out_specs=[
            pl.BlockSpec((tm, D_MODEL), lambda i: (i, 0)),
            pl.BlockSpec((tm, D_MODEL), lambda i: (i, 0)),
            pl.BlockSpec((tm, LANES), lambda i: (i, 0)),
        ],
        out_shape=[
            jax.ShapeDtypeStruct((t, D_MODEL), F32),
            jax.ShapeDtypeStruct((t, D_MODEL), BF16),
            jax.ShapeDtypeStruct((t, LANES), F32),
        ],
        compiler_params=_params(("arbitrary",)),
        name="merge",
    )(oa, ob, oc, big, x2, wb, bgate, wo, nffn, wr_hi, wr_lo, br)


def _moe_kernel(h_ref, comb_ref, x_ref, wg_ref, wu_ref, wd_ref, o_ref, *, tm):
    e = pl.program_id(1)

    @pl.when(e == 0)
    def _():
        o_ref[...] = x_ref[...]

    lane = lax.broadcasted_iota(jnp.int32, (tm, LANES), 1)
    w = jnp.sum(jnp.where(lane == e + RE_LANE, comb_ref[...], 0.0), axis=-1, keepdims=True)
    h = h_ref[...]
    hid = _silu(_dot(h, wg_ref[0])) * _dot(h, wu_ref[0])
    o_ref[...] += w * _dot(hid.astype(BF16), wd_ref[0])


def _moe(h, comb, xn, wg, wu, wd, *, tm=512):
    t = h.shape[0]
    tm = min(tm, t)
    return pl.pallas_call(
        functools.partial(_moe_kernel, tm=tm),
        grid=(t // tm, N_EXPERTS),
        in_specs=[
            pl.BlockSpec((tm, D_MODEL), lambda i, e: (i, 0)),
            pl.BlockSpec((tm, LANES), lambda i, e: (i, 0)),
            pl.BlockSpec((tm, D_MODEL), lambda i, e: (i, 0)),
            pl.BlockSpec((1, D_MODEL, D_EXPERT), lambda i, e: (e, 0, 0)),
            pl.BlockSpec((1, D_MODEL, D_EXPERT), lambda i, e: (e, 0, 0)),
            pl.BlockSpec((1, D_EXPERT, D_MODEL), lambda i, e: (e, 0, 0)),
        ],
        out_specs=pl.BlockSpec((tm, D_MODEL), lambda i, e: (i, 0)),
        out_shape=jax.ShapeDtypeStruct((t, D_MODEL), F32),
        compiler_params=_params(("arbitrary", "arbitrary")),
        name="moe",
    )(h, comb, xn, wg, wu, wd)


def _cols(w, off, width):
    return w[:, off:off + width]


def _pad_lanes(v, lane0, width=LANES):
    out = jnp.zeros((1, width), F32)
    return out.at[0, lane0:lane0 + v.shape[0]].set(v.astype(F32))


def _layer(x2, b, s, norm_mix, w_in, fox_f_bias, fox_q_norm, fox_k_norm, gla_w_lr, gla_b_gate,
           gla_out_norm, ssm_conv_w, ssm_conv_b, ssm_dt_bias, ssm_a_log, ssm_d, ssm_out_norm,
           w_branch, b_branch_gate, w_out, norm_ffn, w_router_grp, b_router_grp,
           w_router_exp, b_router_exp, w_exp_gate, w_exp_up, w_exp_down):
    t = b * s
    w_big = jnp.concatenate([
        _cols(w_in, _O_GATE, 3 * D_MODEL), _cols(w_in, _O_FQ, MIX_W), _cols(w_in, _O_FK, MIX_W),
        _cols(w_in, _O_FV, MIX_W), _cols(w_in, _O_GV, MIX_W), _cols(w_in, _O_GR, MIX_W),
        _cols(w_in, _O_SZ, MIX_W), _cols(w_in, _O_SX, MIX_W), _cols(w_in, _O_GQ, 256),
        _cols(w_in, _O_GK, 256), _cols(w_in, _O_SB, LANES), _cols(w_in, _O_SC, LANES),
    ], axis=1).astype(BF16)
    w_small = jnp.concatenate([
        _cols(w_in, _O_FF, FOX_HEADS), _cols(w_in, _O_GLR, GLA_RANK), _cols(w_in, _O_SDT, SSM_HEADS),
        jnp.zeros((D_MODEL, LANES - FOX_HEADS - GLA_RANK - SSM_HEADS), F32),
    ], axis=1)
    ws_hi, ws_lo = _split2(w_small)

    big, small = _inproj(x2, norm_mix.reshape(1, D_MODEL), w_big, ws_hi, ws_lo)
    big3 = big.reshape(b, s, BIG_COLS)
    small3 = small.reshape(b, s, LANES)

    qf, kf = _fox_prep(big3, small3, _pad_lanes(fox_f_bias, FF_LANE),
                       jnp.tile(fox_q_norm, FOX_HEADS).reshape(1, MIX_W),
                       jnp.tile(fox_k_norm, FOX_HEADS).reshape(1, MIX_W))
    logit_bound = ((FOX_HD ** 0.5) * 1.01 * jnp.max(jnp.abs(fox_q_norm))
                   * jnp.max(jnp.abs(fox_k_norm)))
    o_a = _fox_attn(qf, kf, big3, logit_bound)

    wl = jnp.zeros((LANES, GLA_HEADS * GLA_DK), F32).at[GLR_LANE:GLR_LANE + GLA_RANK].set(gla_w_lr)
    wl_hi, wl_lo = _split2(wl)
    o_b = _gla(big3, small3, wl_hi, wl_lo, gla_b_gate.reshape(1, -1),
               gla_out_norm.reshape(1, GLA_DV))

    o_c = _ssd(big3, small3, ssm_conv_w, ssm_conv_b.reshape(1, -1),
               jnp.repeat(ssm_dt_bias, SSM_HD).reshape(1, MIX_W),
               jnp.repeat(ssm_a_log, SSM_HD).reshape(1, MIX_W),
               jnp.broadcast_to(ssm_dt_bias[:, None], (SSM_HEADS, LANES)),
               jnp.broadcast_to(ssm_a_log[:, None], (SSM_HEADS, LANES)),
               jnp.repeat(ssm_d, SSM_HD).reshape(1, MIX_W),
               ssm_out_norm.reshape(1, MIX_W))

    w_r = jnp.concatenate([w_router_grp, w_router_exp,
                           jnp.zeros((D_MODEL, LANES - N_EGROUPS - N_EXPERTS), F32)], axis=1)
    wr_hi, wr_lo = _split2(w_r)
    b_r = jnp.concatenate([b_router_grp, b_router_exp,
                           jnp.zeros((LANES - N_EGROUPS - N_EXPERTS,), F32)]).reshape(1, LANES)
    xn, h, comb = _merge(o_a.reshape(t, MIX_W), o_b.reshape(t, MIX_W), o_c.reshape(t, MIX_W),
                         big, x2, w_branch.astype(BF16), b_branch_gate, w_out.astype(BF16),
                         norm_ffn.reshape(1, D_MODEL), wr_hi, wr_lo, b_r)

    return _moe(h, comb, xn, w_exp_gate.astype(BF16), w_exp_up.astype(BF16),
                w_exp_down.astype(BF16))


def kernel(x, norm_mix, w_in, fox_f_bias, fox_q_norm, fox_k_norm, gla_w_lr, gla_b_gate, gla_out_norm, ssm_conv_w, ssm_conv_b, ssm_dt_bias, ssm_a_log, ssm_d, ssm_out_norm, w_branch, b_branch_gate, w_out, norm_ffn, w_router_grp, b_router_grp, w_router_exp, b_router_exp, w_exp_gate, w_exp_up, w_exp_down):
    b, s, d = x.shape
    x2 = x.reshape(b * s, d)
    per_layer = (norm_mix, w_in, fox_f_bias, fox_q_norm, fox_k_norm, gla_w_lr, gla_b_gate,
                 gla_out_norm, ssm_conv_w, ssm_conv_b, ssm_dt_bias, ssm_a_log, ssm_d,
                 ssm_out_norm, w_branch, b_branch_gate, w_out, norm_ffn, w_router_grp,
                 b_router_grp, w_router_exp, b_router_exp, w_exp_gate, w_exp_up, w_exp_down)
    for l in range(norm_mix.shape[0]):
        x2 = _layer(x2, b, s, *[p[l] for p in per_layer])
    return x2.reshape(b, s, d)
```

```python
import functools

import numpy as np
import jax
import jax.numpy as jnp
from jax import lax
from jax.experimental import pallas as pl
from jax.experimental.pallas import tpu as pltpu

F32 = jnp.float32
BF16 = jnp.bfloat16

D_MODEL = 1024
MIX_W = 512
EPS = 1e-6
FOX_HEADS = 8
FOX_HD = 64
FOX_PAIRS = FOX_HEADS // 2
GLA_HEADS = 4
GLA_DK = 64
GLA_DV = 128
GLA_RANK = 16
GLA_GATE_NORM = 16.0
GLA_CHUNK = 64
SSM_HEADS = 8
SSM_HD = 64
SSM_GROUPS = 2
SSM_STATE = 64
SSM_CONV = 4
N_EGROUPS = 4
EXP_PER_GROUP = 4
N_EXPERTS = 16
PAIRS_PER_GROUP = 6
D_EXPERT = 512

LANES = 128
NEG_BIG = -1e30
LOG2E = 1.4426950408889634
VMEM_LIMIT = 56 * 1024 * 1024

GATE_OFF, FQ_OFF, FK_OFF, FV_OFF = 0, 3072, 3584, 4096
GV_OFF, GR_OFF, SZ_OFF, SX_OFF = 4608, 5120, 5632, 6144
GQ_OFF, GK_OFF, SB_OFF, SC_OFF = 6656, 6912, 7168, 7296
BIG_COLS = 7424
FF_LANE, GLR_LANE, SDT_LANE = 0, 8, 24
_O_FQ, _O_FK, _O_FV, _O_FF = 0, 512, 1024, 1536
_O_GQ, _O_GK, _O_GV, _O_GR, _O_GLR = 1544, 1800, 2056, 2568, 3080
_O_SZ, _O_SX, _O_SB, _O_SC, _O_SDT, _O_GATE = 3096, 3608, 4120, 4248, 4376, 4384
RG_LANE, RE_LANE = 0, 4
XE_COLS = D_MODEL + LANES


def _split2(x):
    hi = x.astype(BF16)
    lo = (x - hi.astype(F32)).astype(BF16)
    return hi, lo


def _split3(x):
    x1 = x.astype(BF16)
    r = x - x1.astype(F32)
    x2 = r.astype(BF16)
    x3 = (r - x2.astype(F32)).astype(BF16)
    return x1, x2, x3


def _dot(a, b):
    return jnp.dot(a, b, preferred_element_type=F32)


def _dot_nt(a, b):
    return lax.dot_general(a, b, (((1,), (1,)), ((), ())), preferred_element_type=F32)


def _dot_tn(a, b):
    return lax.dot_general(a, b, (((0,), (0,)), ((), ())), preferred_element_type=F32)


def _dot3_left(m_bf16, x_f32):
    x1, x2, x3 = _split3(x_f32)
    return _dot(m_bf16, x1) + _dot(m_bf16, x2) + _dot(m_bf16, x3)


def _dot3_right(x_f32, m_bf16):
    x1, x2, x3 = _split3(x_f32)
    return _dot(x1, m_bf16) + _dot(x2, m_bf16) + _dot(x3, m_bf16)


def _dot_f32w(x_f32, w_hi, w_lo):
    x_hi, x_lo = _split2(x_f32)
    return _dot(x_hi, w_hi) + _dot(x_lo, w_hi) + _dot(x_hi, w_lo)


def _shr(x, pow2):
    return jnp.right_shift(x, pow2.bit_length() - 1)


def _log_sigmoid(x):
    return jnp.minimum(x, 0.0) - jnp.log1p(jnp.exp(-jnp.abs(x)))


def _softplus(x):
    return jnp.maximum(x, 0.0) + jnp.log1p(jnp.exp(-jnp.abs(x)))


def _silu(x):
    return x * (1.0 / (1.0 + jnp.exp(-x)))


def _sigmoid(x):
    return 1.0 / (1.0 + jnp.exp(-x))


def _params(sem):
    return pltpu.CompilerParams(dimension_semantics=sem, vmem_limit_bytes=VMEM_LIMIT)


def _inproj_kernel(x_ref, g_ref, w_ref, wsh_ref, wsl_ref, big_ref, small_ref, *, tn):
    x = x_ref[...]
    ms = jnp.mean(x * x, axis=-1, keepdims=True)
    h = x * lax.rsqrt(ms + EPS) * g_ref[...]
    hb = h.astype(BF16)
    for c in range(BIG_COLS // tn):
        cols = slice(c * tn, (c + 1) * tn)
        big_ref[:, cols] = _dot(hb, w_ref[:, cols]).astype(BF16)
    h_lo = (h - hb.astype(F32)).astype(BF16)
    wsh = wsh_ref[...]
    small_ref[...] = _dot(hb, wsh) + _dot(h_lo, wsh) + _dot(hb, wsl_ref[...])


def _inproj(x2, gain, w_big, ws_hi, ws_lo, *, tm=512, tn=256):
    t = x2.shape[0]
    return pl.pallas_call(
        functools.partial(_inproj_kernel, tn=tn),
        grid=(t // tm,),
        in_specs=[
            pl.BlockSpec((tm, D_MODEL), lambda i: (i, 0)),
            pl.BlockSpec((1, D_MODEL), lambda i: (0, 0)),
            pl.BlockSpec((D_MODEL, BIG_COLS), lambda i: (0, 0), pipeline_mode=pl.Buffered(1)),
            pl.BlockSpec((D_MODEL, LANES), lambda i: (0, 0)),
            pl.BlockSpec((D_MODEL, LANES), lambda i: (0, 0)),
        ],
        out_specs=[
            pl.BlockSpec((tm, BIG_COLS), lambda i: (i, 0)),
            pl.BlockSpec((tm, LANES), lambda i: (i, 0)),
        ],
        out_shape=[
            jax.ShapeDtypeStruct((t, BIG_COLS), BF16),
            jax.ShapeDtypeStruct((t, LANES), F32),
        ],
        compiler_params=_params(("arbitrary",)),
        name="inproj",
    )(x2, gain, w_big, ws_hi, ws_lo)


def _fox_consts(ts):
    ltri = np.tril(np.ones((ts, ts), np.float32))
    hsum = np.kron(np.eye(FOX_HEADS, dtype=np.float32), np.ones((FOX_HD, FOX_HD), np.float32))
    sq = np.zeros((3, LANES, MIX_W), np.float32)
    sk = np.zeros((3, LANES, MIX_W), np.float32)
    oneq = np.zeros((1, MIX_W), np.float32)
    onek = np.zeros((1, MIX_W), np.float32)
    for h in range(FOX_HEADS):
        base = LANES * (h // 2) + 6 * (h % 2)
        for j in range(3):
            sq[j, FF_LANE + h, base + j] = 1.0
            sk[j, FF_LANE + h, base + 3 + j] = -1.0
            oneq[0, base + 3 + j] = 1.0
            onek[0, base + j] = 1.0
    return (jnp.asarray(ltri, BF16), jnp.asarray(hsum, BF16), jnp.asarray(sq, BF16),
            jnp.asarray(sk, BF16), jnp.asarray(oneq), jnp.asarray(onek))


def _fox_prep_kernel(fq_ref, fk_ref, small_ref, fbias_ref, qg_ref, kg_ref, ltri_ref, hsum_ref,
                     sq_ref, sk_ref, oneq_ref, onek_ref, qf_ref, kf_ref, carry_ref, *, ts):
    @pl.when(pl.program_id(1) == 0)
    def _():
        carry_ref[...] = jnp.zeros_like(carry_ref)

    lane = lax.broadcasted_iota(jnp.int32, (ts, LANES), 1)
    f = small_ref[0] + fbias_ref[...]
    ls = jnp.where(lane < FOX_HEADS, _log_sigmoid(f) * LOG2E, 0.0)
    c = _dot3_left(ltri_ref[...], ls) + carry_ref[0:1, :]
    carry_ref[...] = jnp.broadcast_to(c[ts - 1:ts, :], carry_ref.shape)
    c1, c2, c3 = _split3(c)
    qaug = _dot(c1, sq_ref[0]) + _dot(c2, sq_ref[1]) + _dot(c3, sq_ref[2]) + oneq_ref[...]
    kaug = _dot(c1, sk_ref[0]) + _dot(c2, sk_ref[1]) + _dot(c3, sk_ref[2]) + onek_ref[...]

    hsum = hsum_ref[...]

    def head_norm(xb, gain):
        x = xb.astype(F32)
        s_hi, s_lo = _split2(x * x)
        ss = _dot(s_hi, hsum) + _dot(s_lo, hsum)
        return x * lax.rsqrt(ss * (1.0 / FOX_HD) + EPS) * gain

    qn = head_norm(fq_ref[0], qg_ref[...]) * (FOX_HD ** -0.5 * LOG2E)
    kn = head_norm(fk_ref[0], kg_ref[...])
    for p in range(FOX_PAIRS):
        src = slice(LANES * p, LANES * (p + 1))
        dst_x = slice(2 * LANES * p, 2 * LANES * p + LANES)
        dst_a = slice(2 * LANES * p + LANES, 2 * LANES * (p + 1))
        qf_ref[0, :, dst_x] = qn[:, src].astype(BF16)
        qf_ref[0, :, dst_a] = qaug[:, src].astype(BF16)
        kf_ref[0, :, dst_x] = kn[:, src].astype(BF16)
        kf_ref[0, :, dst_a] = kaug[:, src].astype(BF16)


def _fox_prep(big3, small3, fbias, qgain, kgain, *, ts=512):
    b, s, _ = big3.shape
    ts = min(ts, s)
    consts = _fox_consts(ts)
    const_specs = [
        pl.BlockSpec((ts, ts), lambda bi, i: (0, 0)),
        pl.BlockSpec((MIX_W, MIX_W), lambda bi, i: (0, 0)),
        pl.BlockSpec((3, LANES, MIX_W), lambda bi, i: (0, 0, 0)),
        pl.BlockSpec((3, LANES, MIX_W), lambda bi, i: (0, 0, 0)),
        pl.BlockSpec((1, MIX_W), lambda bi, i: (0, 0)),
        pl.BlockSpec((1, MIX_W), lambda bi, i: (0, 0)),
    ]
    return pl.pallas_call(
        functools.partial(_fox_prep_kernel, ts=ts),
        grid=(b, s // ts),
        in_specs=[
            pl.BlockSpec((1, ts, MIX_W), lambda bi, i: (bi, i, FQ_OFF // MIX_W)),
            pl.BlockSpec((1, ts, MIX_W), lambda bi, i: (bi, i, FK_OFF // MIX_W)),
            pl.BlockSpec((1, ts, LANES), lambda bi, i: (bi, i, 0)),
            pl.BlockSpec((1, LANES), lambda bi, i: (0, 0)),
            pl.BlockSpec((1, MIX_W), lambda bi, i: (0, 0)),
            pl.BlockSpec((1, MIX_W), lambda bi, i: (0, 0)),
        ] + const_specs,
        out_specs=[
            pl.BlockSpec((1, ts, 2 * MIX_W), lambda bi, i: (bi, i, 0)),
            pl.BlockSpec((1, ts, 2 * MIX_W), lambda bi, i: (bi, i, 0)),
        ],
        out_shape=[
            jax.ShapeDtypeStruct((b, s, 2 * MIX_W), BF16),
            jax.ShapeDtypeStruct((b, s, 2 * MIX_W), BF16),
        ],
        scratch_shapes=[pltpu.VMEM((8, LANES), F32)],
        compiler_params=_params(("arbitrary", "arbitrary")),
        name="fox_prep",
    )(big3, big3, small3, fbias, qgain, kgain, *consts)


FOX_NOSHIFT_BOUND = 40.0


def _fox_q_heads(q):
    qlane = lax.broadcasted_iota(jnp.int32, (1, 2 * LANES), 1)
    in_a = (qlane < FOX_HD) | ((qlane >= LANES) & (qlane < LANES + 6))
    in_b = ((qlane >= FOX_HD) & (qlane < LANES)) | ((qlane >= LANES + 6) & (qlane < LANES + 12))
    zero = jnp.zeros_like(q)
    return jnp.where(in_a, q, zero), jnp.where(in_b, q, zero)


def _causal_keep(tq):
    r = lax.broadcasted_iota(jnp.int32, (tq, tq), 0)
    c = lax.broadcasted_iota(jnp.int32, (tq, tq), 1)
    return c <= r


def _fox_attn_noshift_kernel(q_ref, k_ref, v_ref, o_ref, acc_ref, *, tq):
    i = pl.program_id(2)
    q_heads = _fox_q_heads(q_ref[0])
    vlane = lax.broadcasted_iota(jnp.int32, (1, LANES), 1)
    first_head = vlane < FOX_HD
    ones_a = jnp.where(vlane == FOX_HD, 1.0, 0.0).astype(BF16)
    ones_b = jnp.where(vlane == 0, 1.0, 0.0).astype(BF16)
    acc_ref[...] = jnp.zeros_like(acc_ref)

    def step(j, diag):
        row0 = pl.multiple_of(j * tq, tq)
        k = k_ref[0, pl.ds(row0, tq), :]
        v = v_ref[0, pl.ds(row0, tq), :]
        v_heads = (jnp.where(first_head, v, ones_a), jnp.where(first_head, ones_b, v))
        for a in range(2):
            s = _dot_nt(q_heads[a], k)
            if diag:
                s = jnp.where(_causal_keep(tq), s, NEG_BIG)
            acc_ref[a] += _dot(jnp.exp2(s).astype(BF16), v_heads[a])

    def body(j, carry):
        step(j, False)
        return carry

    lax.fori_loop(0, i, body, 0)
    step(i, True)
    acc_a = acc_ref[0]
    acc_b = acc_ref[1]
    o = jnp.where(first_head, acc_a * (1.0 / acc_a[:, FOX_HD:FOX_HD + 1]),
                  acc_b * (1.0 / acc_b[:, 0:1]))
    o_ref[0] = o.astype(BF16)


def _fox_attn_online_kernel(q_ref, k_ref, v_ref, o_ref, m_ref, l_ref, acc_ref, *, tq):
    i = pl.program_id(2)
    q_heads = _fox_q_heads(q_ref[0])
    vlane = lax.broadcasted_iota(jnp.int32, (1, LANES), 1)
    first_head = vlane < FOX_HD

    m_ref[...] = jnp.full_like(m_ref, NEG_BIG)
    l_ref[...] = jnp.zeros_like(l_ref)
    acc_ref[...] = jnp.zeros_like(acc_ref)

    def step(j, diag):
        row0 = pl.multiple_of(j * tq, tq)
        k = k_ref[0, pl.ds(row0, tq), :]
        v = v_ref[0, pl.ds(row0, tq), :]
        vzero = jnp.zeros_like(v)
        v_heads = (jnp.where(first_head, v, vzero), jnp.where(first_head, vzero, v))
        pv = None
        alphas = []
        for a in range(2):
            s = _dot_nt(q_heads[a], k)
            if diag:
                s = jnp.where(_causal_keep(tq), s, NEG_BIG)
            m_prev = m_ref[a]
            m_new = jnp.maximum(m_prev, jnp.max(s, axis=-1, keepdims=True))
            p = jnp.exp2(s - m_new)
            alpha = jnp.exp2(m_prev - m_new)
            l_ref[a] = alpha * l_ref[a] + jnp.sum(p, axis=-1, keepdims=True)
            m_ref[a] = m_new
            alphas.append(alpha)
            contrib = _dot(p.astype(BF16), v_heads[a])
            pv = contrib if pv is None else pv + contrib
        alpha2 = jnp.where(first_head, alphas[0], alphas[1])
        acc_ref[...] = alpha2 * acc_ref[...] + pv

    def body(j, carry):
        step(j, False)
        return carry

    lax.fori_loop(0, i, body, 0)
    step(i, True)
    inv = jnp.where(first_head, 1.0 / l_ref[0], 1.0 / l_ref[1])
    o_ref[0] = (acc_ref[...] * inv).astype(BF16)


def _fox_attn_call(body, scratch, name, qf, kf, big3, tq):
    b, s, _ = qf.shape
    return pl.pallas_call(
        functools.partial(body, tq=tq),
        grid=(b, FOX_PAIRS, s // tq),
        in_specs=[
            pl.BlockSpec((1, tq, 2 * LANES), lambda bi, p, i: (bi, i, p)),
            pl.BlockSpec((1, s, 2 * LANES), lambda bi, p, i: (bi, 0, p)),
            pl.BlockSpec((1, s, LANES), lambda bi, p, i: (bi, 0, FV_OFF // LANES + p)),
        ],
        out_specs=pl.BlockSpec((1, tq, LANES), lambda bi, p, i: (bi, i, p)),
        out_shape=jax.ShapeDtypeStruct((b, s, MIX_W), BF16),
        scratch_shapes=scratch,
        compiler_params=_params(("arbitrary", "arbitrary", "arbitrary")),
        name=name,
    )(qf, kf, big3)


def _fox_attn(qf, kf, big3, logit_bound, *, tq=512):
    tq = min(tq, qf.shape[1])

    def noshift():
        return _fox_attn_call(_fox_attn_noshift_kernel, [pltpu.VMEM((2, tq, LANES), F32)],
                              "fox_attn", qf, kf, big3, tq)

    def online():
        scratch = [pltpu.VMEM((2, tq, 1), F32), pltpu.VMEM((2, tq, 1), F32),
                   pltpu.VMEM((tq, LANES), F32)]
        return _fox_attn_call(_fox_attn_online_kernel, scratch, "fox_attn_online", qf, kf, big3, tq)

    return lax.cond(logit_bound < FOX_NOSHIFT_BOUND, noshift, online)


def _gla_consts(r):
    idx = np.arange(r)
    same = (idx[:, None] // GLA_CHUNK) == (idx[None, :] // GLA_CHUNK)
    lblk = (same & (idx[None, :] <= idx[:, None])).astype(np.float32)
    ablk = same.astype(np.float32)
    return jnp.asarray(lblk, BF16), jnp.asarray(ablk, BF16)


def _gla_kernel(q_ref, k_ref, v_ref, r_ref, small_ref, wlh_ref, wll_ref, bg_ref, lblk_ref,
                ablk_ref, gain_ref, o_ref, st_ref, oacc_ref, *, r):
    @pl.when(pl.program_id(1) == 0)
    def _():
        st_ref[...] = jnp.zeros_like(st_ref)

    kw = GLA_HEADS * GLA_DK
    gate = _dot_f32w(small_ref[0], wlh_ref[...], wll_ref[...]) + bg_ref[...]
    log_a = _log_sigmoid(gate) * (1.0 / GLA_GATE_NORM)
    a1, a2, a3 = _split3(log_a)
    lblk = lblk_ref[...]
    ablk = ablk_ref[...]
    bcum = _dot(lblk, a1) + _dot(lblk, a2) + _dot(lblk, a3)
    btot = _dot(ablk, a1) + _dot(ablk, a2) + _dot(ablk, a3)
    q = q_ref[0].astype(F32) * (GLA_DK ** -0.5)
    k = k_ref[0].astype(F32)
    q_dec = (q * jnp.exp(bcum)).astype(BF16)
    k_dec = (k * jnp.exp(-bcum)).astype(BF16)
    k_end = (k * jnp.exp(btot - bcum)).astype(BF16)
    d_tot = jnp.exp(btot)
    v = v_ref[0]

    row = lax.broadcasted_iota(jnp.int32, (r, r), 0)
    col = lax.broadcasted_iota(jnp.int32, (r, r), 1)
    keep = (_shr(row, GLA_CHUNK) == _shr(col, GLA_CHUNK)) & (col <= row)
    klane = lax.broadcasted_iota(jnp.int32, (1, kw), 1)
    qzero = jnp.zeros_like(q_dec)
    for h in range(GLA_HEADS):
        in_h = (klane >= GLA_DK * h) & (klane < GLA_DK * (h + 1))
        att = _dot_nt(jnp.where(in_h, q_dec, qzero), k_dec)
        att = jnp.where(keep, att, 0.0).astype(BF16)
        vcols = slice(GLA_DV * h, GLA_DV * (h + 1))
        oacc_ref[:, vcols] = _dot(att, v[:, vcols])

    srow = lax.broadcasted_iota(jnp.int32, (MIX_W, kw), 0)
    scol = lax.broadcasted_iota(jnp.int32, (MIX_W, kw), 1)
    same_head = _shr(srow, GLA_DV) == _shr(scol, GLA_DK)
    for c in range(r // GLA_CHUNK):
        rows = slice(GLA_CHUNK * c, GLA_CHUNK * (c + 1))
        st = st_ref[...]
        oacc_ref[rows, :] += _dot_nt(q_dec[rows], st.astype(BF16))
        kv_t = _dot_tn(v[rows], k_end[rows])
        st_ref[...] = d_tot[GLA_CHUNK * c:GLA_CHUNK * c + 1, :] * st + jnp.where(same_head, kv_t, 0.0)

    gain = gain_ref[...]
    gr = r_ref[0].astype(F32)
    for h in range(GLA_HEADS):
        vcols = slice(GLA_DV * h, GLA_DV * (h + 1))
        o = oacc_ref[:, vcols]
        ms = jnp.mean(o * o, axis=-1, keepdims=True)
        o_ref[0, :, vcols] = (o * lax.rsqrt(ms + EPS) * gain * _silu(gr[:, vcols])).astype(BF16)


def _gla(big3, small3, wl_hi, wl_lo, bgate, gain, *, r=256):
    b, s, _ = big3.shape
    r = min(r, s)
    kw = GLA_HEADS * GLA_DK
    lblk, ablk = _gla_consts(r)
    return pl.pallas_call(
        functools.partial(_gla_kernel, r=r),
        grid=(b, s // r),
        in_specs=[
            pl.BlockSpec((1, r, kw), lambda bi, i: (bi, i, GQ_OFF // kw)),
            pl.BlockSpec((1, r, kw), lambda bi, i: (bi, i, GK_OFF // kw)),
            pl.BlockSpec((1, r, MIX_W), lambda bi, i: (bi, i, GV_OFF // MIX_W)),
            pl.BlockSpec((1, r, MIX_W), lambda bi, i: (bi, i, GR_OFF // MIX_W)),
            pl.BlockSpec((1, r, LANES), lambda bi, i: (bi, i, 0)),
            pl.BlockSpec((LANES, kw), lambda bi, i: (0, 0)),
            pl.BlockSpec((LANES, kw), lambda bi, i: (0, 0)),
            pl.BlockSpec((1, kw), lambda bi, i: (0, 0)),
            pl.BlockSpec((r, r), lambda bi, i: (0, 0)),
            pl.BlockSpec((r, r), lambda bi, i: (0, 0)),
            pl.BlockSpec((1, GLA_DV), lambda bi, i: (0, 0)),
        ],
        out_specs=pl.BlockSpec((1, r, MIX_W), lambda bi, i: (bi, i, 0)),
        out_shape=jax.ShapeDtypeStruct((b, s, MIX_W), BF16),
        scratch_shapes=[pltpu.VMEM((MIX_W, kw), F32), pltpu.VMEM((r, MIX_W), F32)],
        compiler_params=_params(("arbitrary", "arbitrary")),
        name="gla",
    )(big3, big3, big3, big3, small3, wl_hi, wl_lo, bgate, lblk, ablk, gain)


SSD_HALO = 16
SSD_CONV_W = MIX_W + 2 * SSM_GROUPS * SSM_STATE


def _ssd_consts(r):
    expand = np.zeros((LANES, MIX_W), np.float32)
    for h in range(SSM_HEADS):
        expand[SDT_LANE + h, SSM_HD * h:SSM_HD * (h + 1)] = 1.0
    rep = np.zeros((LANES, MIX_W), np.float32)
    for h in range(SSM_HEADS):
        g = h // (SSM_HEADS // SSM_GROUPS)
        for n in range(SSM_STATE):
            rep[SSM_STATE * g + n, SSM_HD * h + n] = 1.0
    ltri = np.tril(np.ones((r, r), np.float32))
    return (jnp.asarray(expand, BF16), jnp.asarray(rep, BF16), jnp.asarray(ltri, BF16),
            jnp.asarray(ltri.T, BF16))


def _ssd_kernel(z_ref, x_ref, xp_ref, b_ref, bp_ref, c_ref, cp_ref, small_ref, cw_ref, cb_ref,
                dtb_ref, alog_ref, dtbc_ref, alogc_ref, dskip_ref, onorm_ref, expand_ref,
                rep_ref, ltri_ref, utri_ref, o_ref, sw_ref, ext_ref, y_ref, *, r):
    first = pl.program_id(1) == 0

    @pl.when(first)
    def _():
        sw_ref[...] = jnp.zeros_like(sw_ref)

    keep_prev = jnp.where(first, 0.0, 1.0)
    ext_ref[0:SSD_HALO, 0:MIX_W] = xp_ref[0].astype(F32) * keep_prev
    ext_ref[0:SSD_HALO, MIX_W:MIX_W + LANES] = bp_ref[0].astype(F32) * keep_prev
    ext_ref[0:SSD_HALO, MIX_W + LANES:SSD_CONV_W] = cp_ref[0].astype(F32) * keep_prev
    ext_ref[SSD_HALO:, 0:MIX_W] = x_ref[0].astype(F32)
    ext_ref[SSD_HALO:, MIX_W:MIX_W + LANES] = b_ref[0].astype(F32)
    ext_ref[SSD_HALO:, MIX_W + LANES:SSD_CONV_W] = c_ref[0].astype(F32)
    conv = cb_ref[...] + cw_ref[SSM_CONV - 1:SSM_CONV, :] * ext_ref[SSD_HALO:, :]
    for back in range(1, SSM_CONV):
        tap = SSM_CONV - 1 - back
        conv = conv + cw_ref[tap:tap + 1, :] * ext_ref[pl.ds(SSD_HALO - back, r), :]
    xbc = _silu(conv)
    xs = xbc[:, 0:MIX_W]
    bm = xbc[:, MIX_W:MIX_W + LANES].astype(BF16)
    cm = xbc[:, MIX_W + LANES:SSD_CONV_W].astype(BF16)

    sm = small_ref[0]
    dt = _softplus(_dot3_right(sm, expand_ref[...]) + dtb_ref[...])
    a_neg = -jnp.exp(alog_ref[...])
    acs = _dot3_left(ltri_ref[...], dt * a_neg)
    acs_last = acs[r - 1:r, :]
    sm_t = sm.T
    dt_t = _softplus(sm_t[SDT_LANE:SDT_LANE + SSM_HEADS, :] + dtbc_ref[:, 0:1])
    acs_t = _dot3_right(dt_t * (-jnp.exp(alogc_ref[:, 0:1])), utri_ref[...])

    xdt = (xs * dt).astype(BF16)
    row = lax.broadcasted_iota(jnp.int32, (r, r), 0)
    col = lax.broadcasted_iota(jnp.int32, (r, r), 1)
    causal = col <= row
    glane = lax.broadcasted_iota(jnp.int32, (1, LANES), 1)
    first_half = glane < SSM_STATE
    czero = jnp.zeros_like(cm)
    hpg = SSM_HEADS // SSM_GROUPS
    for g in range(SSM_GROUPS):
        cg = jnp.where(first_half if g == 0 else ~first_half, cm, czero)
        cb_g = _dot_nt(cg, bm)
        for pair in range(hpg // 2):
            p = g * (hpg // 2) + pair
            xp = xdt[:, LANES * p:LANES * (p + 1)]
            xzero = jnp.zeros_like(xp)
            acc = None
            for a in range(2):
                h = 2 * p + a
                dmat = acs[:, SSM_HD * h:SSM_HD * h + 1] - acs_t[h:h + 1, :]
                sc = (cb_g * jnp.exp(jnp.where(causal, dmat, NEG_BIG))).astype(BF16)
                xh = jnp.where(first_half if a == 0 else ~first_half, xp, xzero)
                contrib = _dot(sc, xh)
                acc = contrib if acc is None else acc + contrib
            y_ref[:, LANES * p:LANES * (p + 1)] = acc

    rep = rep_ref[...]
    sw = sw_ref[...]
    cw = (_dot(cm, rep) * jnp.exp(acs)).astype(BF16)
    y = y_ref[...] + _dot(cw, sw.astype(BF16)) + dskip_ref[...] * xs
    bw = (_dot(bm, rep) * jnp.exp(acs_last - acs)).astype(BF16)
    upd = _dot_tn(bw, xdt)
    srow = lax.broadcasted_iota(jnp.int32, (MIX_W, MIX_W), 0)
    scol = lax.broadcasted_iota(jnp.int32, (MIX_W, MIX_W), 1)
    same_head = _shr(srow, SSM_STATE) == _shr(scol, SSM_HD)
    sw_ref[...] = sw * jnp.exp(acs_last) + jnp.where(same_head, upd, 0.0)

    y = y * _silu(z_ref[0].astype(F32))
    gw = MIX_W // SSM_GROUPS
    for g in range(SSM_GROUPS):
        cols = slice(gw * g, gw * (g + 1))
        yg = y[:, cols]
        ms = jnp.mean(yg * yg, axis=-1, keepdims=True)
        o_ref[0, :, cols] = (yg * lax.rsqrt(ms + EPS) * onorm_ref[:, cols]).astype(BF16)


def _ssd(big3, small3, conv_w, conv_b, dtb_w, alog_w, dtb_c, alog_c, dskip_w, onorm, *, r=256):
    b, s, _ = big3.shape
    r = min(r, s)
    consts = _ssd_consts(r)
    hb = r // SSD_HALO

    def cur(width, off):
        return pl.BlockSpec((1, r, width), lambda bi, i: (bi, i, off // width))

    def prev(width, off):
        return pl.BlockSpec((1, SSD_HALO, width),
                            lambda bi, i: (bi, jnp.maximum(i * hb - 1, 0), off // width))

    def whole(shape):
        return pl.BlockSpec(shape, lambda bi, i: (0,) * len(shape))

    return pl.pallas_call(
        functools.partial(_ssd_kernel, r=r),
        grid=(b, s // r),
        in_specs=[
            cur(MIX_W, SZ_OFF),
            cur(MIX_W, SX_OFF), prev(MIX_W, SX_OFF),
            cur(LANES, SB_OFF), prev(LANES, SB_OFF),
            cur(LANES, SC_OFF), prev(LANES, SC_OFF),
            pl.BlockSpec((1, r, LANES), lambda bi, i: (bi, i, 0)),
            whole((SSM_CONV, SSD_CONV_W)), whole((1, SSD_CONV_W)),
            whole((1, MIX_W)), whole((1, MIX_W)),
            whole((SSM_HEADS, LANES)), whole((SSM_HEADS, LANES)),
            whole((1, MIX_W)), whole((1, MIX_W)),
            whole((LANES, MIX_W)), whole((LANES, MIX_W)), whole((r, r)), whole((r, r)),
        ],
        out_specs=pl.BlockSpec((1, r, MIX_W), lambda bi, i: (bi, i, 0)),
        out_shape=jax.ShapeDtypeStruct((b, s, MIX_W), BF16),
        scratch_shapes=[
            pltpu.VMEM((MIX_W, MIX_W), F32),
            pltpu.VMEM((r + SSD_HALO, SSD_CONV_W), F32),
            pltpu.VMEM((r, MIX_W), F32),
        ],
        compiler_params=_params(("arbitrary", "arbitrary")),
        name="ssd",
    )(big3, big3, big3, big3, big3, big3, big3, small3, conv_w, conv_b, dtb_w, alog_w,
      dtb_c, alog_c, dskip_w, onorm, *consts)


def _merge_kernel(oa_ref, ob_ref, oc_ref, gate_ref, x_ref, wb_ref, bgate_ref, wo_ref, nffn_ref,
                  wrh_ref, wrl_ref, br_ref, lstrict_ref, xe_ref, route_ref, counts_ref, *, tm):
    @pl.when(pl.program_id(0) == 0)
    def _():
        counts_ref[...] = jnp.zeros_like(counts_ref)

    mixed = None
    for ridx, o_ref in enumerate((oa_ref, ob_ref, oc_ref)):
        cols = slice(D_MODEL * ridx, D_MODEL * (ridx + 1))
        gate = _sigmoid(gate_ref[:, cols].astype(F32) + bgate_ref[ridx:ridx + 1, :])
        term = gate * _dot(o_ref[...], wb_ref[ridx])
        mixed = term if mixed is None else mixed + term
    xn = x_ref[...] + _dot(mixed.astype(BF16), wo_ref[...])
    xe_ref[:, 0:D_MODEL] = xn
    ms = jnp.mean(xn * xn, axis=-1, keepdims=True)
    h = xn * lax.rsqrt(ms + EPS) * nffn_ref[...]

    logits = _dot_f32w(h, wrh_ref[...], wrl_ref[...]) + br_ref[...]
    lane = lax.broadcasted_iota(jnp.int32, (tm, LANES), 1)
    lane_f = lane.astype(F32)
    big_lane = float(LANES)
    is_grp = lane < RE_LANE
    gl = jnp.where(is_grp, logits, NEG_BIG)
    gmax = jnp.max(gl, axis=-1, keepdims=True)
    g_w = 1.0 / jnp.sum(jnp.exp(gl - gmax), axis=-1, keepdims=True)
    g_sel = jnp.min(jnp.where(gl == gmax, lane_f, big_lane), axis=-1, keepdims=True)
    grp_of_lane = _shr(jnp.maximum(lane - RE_LANE, 0), EXP_PER_GROUP).astype(F32)
    in_grp = (lane >= RE_LANE) & (lane < RE_LANE + N_EXPERTS) & (grp_of_lane == g_sel)
    el = jnp.where(in_grp, logits, NEG_BIG)
    e1 = jnp.max(el, axis=-1, keepdims=True)
    i1 = jnp.min(jnp.where(in_grp & (el == e1), lane_f, big_lane), axis=-1, keepdims=True)
    rest = in_grp & (lane_f != i1)
    el2 = jnp.where(rest, logits, NEG_BIG)
    e2 = jnp.max(el2, axis=-1, keepdims=True)
    i2 = jnp.min(jnp.where(rest & (el2 == e2), lane_f, big_lane), axis=-1, keepdims=True)
    ratio = jnp.exp(e2 - e1)
    w1 = g_w / (1.0 + ratio)
    w2 = w1 * ratio
    xe_ref[:, D_MODEL:] = jnp.where(lane_f == i1, w1, 0.0) + jnp.where(lane_f == i2, w2, 0.0)

    first_lane = RE_LANE + EXP_PER_GROUP * g_sel
    lo = jnp.minimum(i1, i2) - first_lane
    hi = jnp.maximum(i1, i2) - first_lane
    cls = PAIRS_PER_GROUP * g_sel + lo * (7.0 - lo) * 0.5 + (hi - lo - 1.0)
    is_cls = lane_f == cls
    onehot = jnp.where(is_cls, 1.0, 0.0).astype(BF16)
    before = _dot(lstrict_ref[...], onehot) + counts_ref[0:1, :]
    rank = jnp.sum(jnp.where(is_cls, before, 0.0), axis=-1, keepdims=True)
    route_ref[...] = jnp.where(lane == 0, cls, jnp.where(lane == 1, rank, 0.0))
    total = _dot(jnp.ones((8, tm), BF16), onehot)
    counts_ref[...] = counts_ref[...] + total


def _merge(oa, ob, oc, big, x2, wb, bgate, wo, nffn, wr_hi, wr_lo, br, *, tm=512):
    t = x2.shape[0]
    tm = min(tm, t)
    lstrict = jnp.asarray(np.tril(np.ones((tm, tm), np.float32), -1), BF16)

    def whole(shape):
        return pl.BlockSpec(shape, lambda i: (0,) * len(shape))

    return pl.pallas_call(
        functools.partial(_merge_kernel, tm=tm),
        grid=(t // tm,),
        in_specs=[
            pl.BlockSpec((tm, MIX_W), lambda i: (i, 0)),
            pl.BlockSpec((tm, MIX_W), lambda i: (i, 0)),
            pl.BlockSpec((tm, MIX_W), lambda i: (i, 0)),
            pl.BlockSpec((tm, 3 * D_MODEL), lambda i: (i, 0)),
            pl.BlockSpec((tm, D_MODEL), lambda i: (i, 0)),
            whole((3, MIX_W, D_MODEL)), whole((3, D_MODEL)), whole((D_MODEL, D_MODEL)),
            whole((1, D_MODEL)), whole((D_MODEL, LANES)), whole((D_MODEL, LANES)),
            whole((1, LANES)), whole((tm, tm)),
        ],
        out_specs=[
            pl.BlockSpec((tm, XE_COLS), lambda i: (i, 0)),
            pl.BlockSpec((tm, LANES), lambda i: (i, 0)),
            pl.BlockSpec((8, LANES), lambda i: (0, 0)),
        ],
        out_shape=[
            jax.ShapeDtypeStruct((t, XE_COLS), F32),
            jax.ShapeDtypeStruct((t, LANES), F32),
            jax.ShapeDtypeStruct((8, LANES), F32),
        ],
        compiler_params=_params(("arbitrary",)),
        name="merge",
    )(oa, ob, oc, big, x2, wb, bgate, wo, nffn, wr_hi, wr_lo, br, lstrict)


MOE_TILE = 256
ROW_DMA_TILE = 512


def _row_copy_kernel(pos_ref, src_ref, *rest, tm, scatter):
    dst_ref, sem = rest[-2], rest[-1]
    base = pl.program_id(0) * tm

    def copy(r):
        mine = pl.ds(base + r, 1)
        other = pl.ds(pos_ref[0, 0, r], 1)
        if scatter:
            return pltpu.make_async_copy(src_ref.at[mine], dst_ref.at[other], sem)
        return pltpu.make_async_copy(src_ref.at[other], dst_ref.at[mine], sem)

    def issue(r, carry):
        copy(r).start()
        return carry

    def drain(r, carry):
        copy(r).wait()
        return carry

    lax.fori_loop(0, tm, issue, 0, unroll=8)
    lax.fori_loop(0, tm, drain, 0, unroll=8)


def _row_copy(pos, src, dst_init, out_rows, *, scatter, name):
    t = pos.shape[0]
    tm = min(ROW_DMA_TILE, t)
    width = src.shape[1]
    any_spec = pl.BlockSpec(memory_space=pl.ANY)
    operands = [pos.reshape(t // tm, 1, tm), src]
    in_specs = [pl.BlockSpec((1, 1, tm), lambda i: (i, 0, 0), memory_space=pltpu.SMEM), any_spec]
    aliases = {}
    if dst_init is not None:
        operands.append(dst_init)
        in_specs.append(any_spec)
        aliases = {2: 0}
    return pl.pallas_call(
        functools.partial(_row_copy_kernel, tm=tm, scatter=scatter),
        grid=(t // tm,),
        in_specs=in_specs,
        out_specs=any_spec,
        out_shape=jax.ShapeDtypeStruct((out_rows, width), src.dtype),
        scratch_shapes=[pltpu.SemaphoreType.DMA(())],
        input_output_aliases=aliases,
        compiler_params=_params(("arbitrary",)),
        name=name,
    )(*operands)


def _moe_sorted_kernel(ea_ref, eb_ref, nused_ref, xs_ref, nffn_ref, wga_ref, wua_ref, wda_ref,
                       wgb_ref, wub_ref, wdb_ref, o_ref, *, tm):
    i = pl.program_id(0)

    @pl.when(i < nused_ref[0])
    def _():
        xn = xs_ref[:, 0:D_MODEL]
        comb = xs_ref[:, D_MODEL:]
        ms = jnp.mean(xn * xn, axis=-1, keepdims=True)
        h = (xn * lax.rsqrt(ms + EPS) * nffn_ref[...]).astype(BF16)
        lane = lax.broadcasted_iota(jnp.int32, (tm, LANES), 1)
        out = xn
        for e_ref, wg_ref, wu_ref, wd_ref in ((ea_ref, wga_ref, wua_ref, wda_ref),
                                              (eb_ref, wgb_ref, wub_ref, wdb_ref)):
            w = jnp.sum(jnp.where(lane == e_ref[i] + RE_LANE, comb, 0.0), axis=-1, keepdims=True)
            hid = _silu(_dot(h, wg_ref[0])) * _dot(h, wu_ref[0])
            out = out + w * _dot(hid.astype(BF16), wd_ref[0])
        o_ref[...] = out

    @pl.when(i >= nused_ref[0])
    def _():
        o_ref[...] = jnp.zeros_like(o_ref)


def _moe_sorted(ea, eb, nused, xs, nffn, wg, wu, wd, *, tm):
    n_tiles = xs.shape[0] // tm

    def w_in(which):
        return pl.BlockSpec((1, D_MODEL, D_EXPERT), lambda i, ea, eb, nu: ((ea, eb)[which][i], 0, 0))

    def w_out(which):
        return pl.BlockSpec((1, D_EXPERT, D_MODEL), lambda i, ea, eb, nu: ((ea, eb)[which][i], 0, 0))

    grid_spec = pltpu.PrefetchScalarGridSpec(
        num_scalar_prefetch=3,
        grid=(n_tiles,),
        in_specs=[
            pl.BlockSpec((tm, XE_COLS), lambda i, ea, eb, nu: (i, 0)),
            pl.BlockSpec((1, D_MODEL), lambda i, ea, eb, nu: (0, 0)),
            w_in(0), w_in(0), w_out(0), w_in(1), w_in(1), w_out(1),
        ],
        out_specs=pl.BlockSpec((tm, D_MODEL), lambda i, ea, eb, nu: (i, 0)),
    )
    return pl.pallas_call(
        functools.partial(_moe_sorted_kernel, tm=tm),
        grid_spec=grid_spec,
        out_shape=jax.ShapeDtypeStruct((n_tiles * tm, D_MODEL), F32),
        compiler_params=_params(("arbitrary",)),
        name="moe",
    )(ea, eb, nused, xs, nffn, wg, wu, wd, wg, wu, wd)


_PAIR_LO = np.array([0, 0, 0, 1, 1, 2], np.int32)
_PAIR_HI = np.array([1, 2, 3, 2, 3, 3], np.int32)


def _moe(xe, route, counts, nffn, wg, wu, wd):
    t = xe.shape[0]
    tm = min(MOE_TILE, t)
    n_cls = N_EGROUPS * PAIRS_PER_GROUP
    n_tiles = t // tm + n_cls
    cnt = counts[0, :n_cls].astype(jnp.int32)
    tiles = (cnt + tm - 1) // tm
    tile_end = jnp.cumsum(tiles)
    n_used = tile_end[-1]
    cls_base = (tile_end - tiles) * tm
    tile_cls = jnp.searchsorted(tile_end, jnp.minimum(jnp.arange(n_tiles), n_used - 1), side="right")
    tile_cls = tile_cls.astype(jnp.int32)
    grp, pair = tile_cls // PAIRS_PER_GROUP, tile_cls % PAIRS_PER_GROUP
    ea = EXP_PER_GROUP * grp + jnp.asarray(_PAIR_LO)[pair]
    eb = EXP_PER_GROUP * grp + jnp.asarray(_PAIR_HI)[pair]
    pos = cls_base[route[:, 0].astype(jnp.int32)] + route[:, 1].astype(jnp.int32)

    xs = _row_copy(pos, xe, jnp.zeros((n_tiles * tm, XE_COLS), F32), n_tiles * tm,
                   scatter=True, name="moe_scatter")
    ys = _moe_sorted(ea, eb, n_used.reshape(1), xs, nffn, wg, wu, wd, tm=tm)
    return _row_copy(pos, ys, None, t, scatter=False, name="moe_gather")


def _cols(w, off, width):
    return w[:, off:off + width]


def _pad_lanes(v, lane0, width=LANES):
    out = jnp.zeros((1, width), F32)
    return out.at[0, lane0:lane0 + v.shape[0]].set(v.astype(F32))


def _layer(x2, b, s, norm_mix, w_in, fox_f_bias, fox_q_norm, fox_k_norm, gla_w_lr, gla_b_gate,
           gla_out_norm, ssm_conv_w, ssm_conv_b, ssm_dt_bias, ssm_a_log, ssm_d, ssm_out_norm,
           w_branch, b_branch_gate, w_out, norm_ffn, w_router_grp, b_router_grp,
           w_router_exp, b_router_exp, w_exp_gate, w_exp_up, w_exp_down):
    t = b * s
    w_big = jnp.concatenate([
        _cols(w_in, _O_GATE, 3 * D_MODEL), _cols(w_in, _O_FQ, MIX_W), _cols(w_in, _O_FK, MIX_W),
        _cols(w_in, _O_FV, MIX_W), _cols(w_in, _O_GV, MIX_W), _cols(w_in, _O_GR, MIX_W),
        _cols(w_in, _O_SZ, MIX_W), _cols(w_in, _O_SX, MIX_W), _cols(w_in, _O_GQ, 256),
        _cols(w_in, _O_GK, 256), _cols(w_in, _O_SB, LANES), _cols(w_in, _O_SC, LANES),
    ], axis=1).astype(BF16)
    w_small = jnp.concatenate([
        _cols(w_in, _O_FF, FOX_HEADS), _cols(w_in, _O_GLR, GLA_RANK), _cols(w_in, _O_SDT, SSM_HEADS),
        jnp.zeros((D_MODEL, LANES - FOX_HEADS - GLA_RANK - SSM_HEADS), F32),
    ], axis=1)
    ws_hi, ws_lo = _split2(w_small)

    big, small = _inproj(x2, norm_mix.reshape(1, D_MODEL), w_big, ws_hi, ws_lo)
    big3 = big.reshape(b, s, BIG_COLS)
    small3 = small.reshape(b, s, LANES)

    qf, kf = _fox_prep(big3, small3, _pad_lanes(fox_f_bias, FF_LANE),
                       jnp.tile(fox_q_norm, FOX_HEADS).reshape(1, MIX_W),
                       jnp.tile(fox_k_norm, FOX_HEADS).reshape(1, MIX_W))
    logit_bound = ((FOX_HD ** 0.5) * 1.01 * jnp.max(jnp.abs(fox_q_norm))
                   * jnp.max(jnp.abs(fox_k_norm)))
    o_a = _fox_attn(qf, kf, big3, logit_bound)

    wl = jnp.zeros((LANES, GLA_HEADS * GLA_DK), F32).at[GLR_LANE:GLR_LANE + GLA_RANK].set(gla_w_lr)
    wl_hi, wl_lo = _split2(wl)
    o_b = _gla(big3, small3, wl_hi, wl_lo, gla_b_gate.reshape(1, -1),
               gla_out_norm.reshape(1, GLA_DV))

    o_c = _ssd(big3, small3, ssm_conv_w, ssm_conv_b.reshape(1, -1),
               jnp.repeat(ssm_dt_bias, SSM_HD).reshape(1, MIX_W),
               jnp.repeat(ssm_a_log, SSM_HD).reshape(1, MIX_W),
               jnp.broadcast_to(ssm_dt_bias[:, None], (SSM_HEADS, LANES)),
               jnp.broadcast_to(ssm_a_log[:, None], (SSM_HEADS, LANES)),
               jnp.repeat(ssm_d, SSM_HD).reshape(1, MIX_W),
               ssm_out_norm.reshape(1, MIX_W))

    w_r = jnp.concatenate([w_router_grp, w_router_exp,
                           jnp.zeros((D_MODEL, LANES - N_EGROUPS - N_EXPERTS), F32)], axis=1)
    wr_hi, wr_lo = _split2(w_r)
    b_r = jnp.concatenate([b_router_grp, b_router_exp,
                           jnp.zeros((LANES - N_EGROUPS - N_EXPERTS,), F32)]).reshape(1, LANES)
    nffn = norm_ffn.reshape(1, D_MODEL)
    xe, route, counts = _merge(o_a.reshape(t, MIX_W), o_b.reshape(t, MIX_W), o_c.reshape(t, MIX_W),
                               big, x2, w_branch.astype(BF16), b_branch_gate, w_out.astype(BF16),
                               nffn, wr_hi, wr_lo, b_r)

    return _moe(xe, route, counts, nffn, w_exp_gate.astype(BF16), w_exp_up.astype(BF16),
                w_exp_down.astype(BF16))


def kernel(x, norm_mix, w_in, fox_f_bias, fox_q_norm, fox_k_norm, gla_w_lr, gla_b_gate, gla_out_norm, ssm_conv_w, ssm_conv_b, ssm_dt_bias, ssm_a_log, ssm_d, ssm_out_norm, w_branch, b_branch_gate, w_out, norm_ffn, w_router_grp, b_router_grp, w_router_exp, b_router_exp, w_exp_gate, w_exp_up, w_exp_down):
    b, s, d = x.shape
    x2 = x.reshape(b * s, d)
    per_layer = (norm_mix, w_in, fox_f_bias, fox_q_norm, fox_k_norm, gla_w_lr, gla_b_gate,
                 gla_out_norm, ssm_conv_w, ssm_conv_b, ssm_dt_bias, ssm_a_log, ssm_d,
                 ssm_out_norm, w_branch, b_branch_gate, w_out, norm_ffn, w_router_grp,
                 b_router_grp, w_router_exp, b_router_exp, w_exp_gate, w_exp_up, w_exp_down)
    for l in range(norm_mix.shape[0]):
        x2 = _layer(x2, b, s, *[p[l] for p in per_layer])
    return x2.reshape(b, s, d)
```

```python
import functools

import numpy as np
import jax
import jax.numpy as jnp
from jax import lax
from jax.experimental import pallas as pl
from jax.experimental.pallas import tpu as pltpu

F32 = jnp.float32
BF16 = jnp.bfloat16

D_MODEL = 1024
MIX_W = 512
EPS = 1e-6
FOX_HEADS = 8
FOX_HD = 64
FOX_PAIRS = FOX_HEADS // 2
GLA_HEADS = 4
GLA_DK = 64
GLA_DV = 128
GLA_RANK = 16
GLA_GATE_NORM = 16.0
GLA_CHUNK = 64
SSM_HEADS = 8
SSM_HD = 64
SSM_GROUPS = 2
SSM_STATE = 64
SSM_CONV = 4
N_EGROUPS = 4
EXP_PER_GROUP = 4
N_EXPERTS = 16
PAIRS_PER_GROUP = 6
D_EXPERT = 512

LANES = 128
NEG_BIG = -1e30
LOG2E = 1.4426950408889634
VMEM_LIMIT = 56 * 1024 * 1024

GATE_OFF, FQ_OFF, FK_OFF, FV_OFF = 0, 3072, 3584, 4096
GV_OFF, GR_OFF, SZ_OFF, SX_OFF = 4608, 5120, 5632, 6144
GQ_OFF, GK_OFF, SB_OFF, SC_OFF = 6656, 6912, 7168, 7296
BIG_COLS = 7424
FF_LANE, GLR_LANE, SDT_LANE = 0, 8, 24
_O_FQ, _O_FK, _O_FV, _O_FF = 0, 512, 1024, 1536
_O_GQ, _O_GK, _O_GV, _O_GR, _O_GLR = 1544, 1800, 2056, 2568, 3080
_O_SZ, _O_SX, _O_SB, _O_SC, _O_SDT, _O_GATE = 3096, 3608, 4120, 4248, 4376, 4384
RG_LANE, RE_LANE = 0, 4
XE_COLS = D_MODEL + LANES


def _split2(x):
    hi = x.astype(BF16)
    lo = (x - hi.astype(F32)).astype(BF16)
    return hi, lo


def _split3(x):
    x1 = x.astype(BF16)
    r = x - x1.astype(F32)
    x2 = r.astype(BF16)
    x3 = (r - x2.astype(F32)).astype(BF16)
    return x1, x2, x3


def _dot(a, b):
    return jnp.dot(a, b, preferred_element_type=F32)


def _dot_nt(a, b):
    return lax.dot_general(a, b, (((1,), (1,)), ((), ())), preferred_element_type=F32)


def _dot_tn(a, b):
    return lax.dot_general(a, b, (((0,), (0,)), ((), ())), preferred_element_type=F32)


def _dot3_left(m_bf16, x_f32):
    x1, x2, x3 = _split3(x_f32)
    return _dot(m_bf16, x1) + _dot(m_bf16, x2) + _dot(m_bf16, x3)


def _dot3_right(x_f32, m_bf16):
    x1, x2, x3 = _split3(x_f32)
    return _dot(x1, m_bf16) + _dot(x2, m_bf16) + _dot(x3, m_bf16)


def _dot_f32w(x_f32, w_hi, w_lo):
    x_hi, x_lo = _split2(x_f32)
    return _dot(x_hi, w_hi) + _dot(x_lo, w_hi) + _dot(x_hi, w_lo)


def _shr(x, pow2):
    return jnp.right_shift(x, pow2.bit_length() - 1)


def _log_sigmoid(x):
    return jnp.minimum(x, 0.0) - jnp.log1p(jnp.exp(-jnp.abs(x)))


def _softplus(x):
    return jnp.maximum(x, 0.0) + jnp.log1p(jnp.exp(-jnp.abs(x)))


def _silu(x):
    return x * (1.0 / (1.0 + jnp.exp(-x)))


def _sigmoid(x):
    return 1.0 / (1.0 + jnp.exp(-x))


def _params(sem):
    return pltpu.CompilerParams(dimension_semantics=sem, vmem_limit_bytes=VMEM_LIMIT)


def _inproj_kernel(x_ref, g_ref, w_ref, wsh_ref, wsl_ref, big_ref, small_ref, *, tn):
    x = x_ref[...]
    ms = jnp.mean(x * x, axis=-1, keepdims=True)
    h = x * lax.rsqrt(ms + EPS) * g_ref[...]
    hb = h.astype(BF16)
    for c in range(BIG_COLS // tn):
        cols = slice(c * tn, (c + 1) * tn)
        big_ref[:, cols] = _dot(hb, w_ref[:, cols]).astype(BF16)
    h_lo = (h - hb.astype(F32)).astype(BF16)
    wsh = wsh_ref[...]
    small_ref[...] = _dot(hb, wsh) + _dot(h_lo, wsh) + _dot(hb, wsl_ref[...])


def _inproj(x2, gain, w_big, ws_hi, ws_lo, *, tm=512, tn=256):
    t = x2.shape[0]
    return pl.pallas_call(
        functools.partial(_inproj_kernel, tn=tn),
        grid=(t // tm,),
        in_specs=[
            pl.BlockSpec((tm, D_MODEL), lambda i: (i, 0)),
            pl.BlockSpec((1, D_MODEL), lambda i: (0, 0)),
            pl.BlockSpec((D_MODEL, BIG_COLS), lambda i: (0, 0), pipeline_mode=pl.Buffered(1)),
            pl.BlockSpec((D_MODEL, LANES), lambda i: (0, 0)),
            pl.BlockSpec((D_MODEL, LANES), lambda i: (0, 0)),
        ],
        out_specs=[
            pl.BlockSpec((tm, BIG_COLS), lambda i: (i, 0)),
            pl.BlockSpec((tm, LANES), lambda i: (i, 0)),
        ],
        out_shape=[
            jax.ShapeDtypeStruct((t, BIG_COLS), BF16),
            jax.ShapeDtypeStruct((t, LANES), F32),
        ],
        compiler_params=_params(("arbitrary",)),
        name="inproj",
    )(x2, gain, w_big, ws_hi, ws_lo)


def _fox_consts(ts):
    ltri = np.tril(np.ones((ts, ts), np.float32))
    hsum = np.kron(np.eye(FOX_HEADS, dtype=np.float32), np.ones((FOX_HD, FOX_HD), np.float32))
    sq = np.zeros((3, LANES, MIX_W), np.float32)
    sk = np.zeros((3, LANES, MIX_W), np.float32)
    oneq = np.zeros((1, MIX_W), np.float32)
    onek = np.zeros((1, MIX_W), np.float32)
    for h in range(FOX_HEADS):
        base = LANES * (h // 2) + 6 * (h % 2)
        for j in range(3):
            sq[j, FF_LANE + h, base + j] = 1.0
            sk[j, FF_LANE + h, base + 3 + j] = -1.0
            oneq[0, base + 3 + j] = 1.0
            onek[0, base + j] = 1.0
    return (jnp.asarray(ltri, BF16), jnp.asarray(hsum, BF16), jnp.asarray(sq, BF16),
            jnp.asarray(sk, BF16), jnp.asarray(oneq), jnp.asarray(onek))


def _fox_prep_kernel(fq_ref, fk_ref, fv_ref, small_ref, fbias_ref, qg_ref, kg_ref, ltri_ref,
                     hsum_ref, sq_ref, sk_ref, oneq_ref, onek_ref, qt_ref, kf_ref, vt_ref, carry_ref,
                     *, ts):
    @pl.when(pl.program_id(1) == 0)
    def _():
        carry_ref[...] = jnp.zeros_like(carry_ref)

    lane = lax.broadcasted_iota(jnp.int32, (ts, LANES), 1)
    f = small_ref[0] + fbias_ref[...]
    ls = jnp.where(lane < FOX_HEADS, _log_sigmoid(f) * LOG2E, 0.0)
    c = _dot3_left(ltri_ref[...], ls) + carry_ref[0:1, :]
    carry_ref[...] = jnp.broadcast_to(c[ts - 1:ts, :], carry_ref.shape)
    c1, c2, c3 = _split3(c)
    qaug = _dot(c1, sq_ref[0]) + _dot(c2, sq_ref[1]) + _dot(c3, sq_ref[2]) + oneq_ref[...]
    kaug = _dot(c1, sk_ref[0]) + _dot(c2, sk_ref[1]) + _dot(c3, sk_ref[2]) + onek_ref[...]

    hsum = hsum_ref[...]

    def head_norm(xb, gain):
        x = xb.astype(F32)
        s_hi, s_lo = _split2(x * x)
        ss = _dot(s_hi, hsum) + _dot(s_lo, hsum)
        return x * lax.rsqrt(ss * (1.0 / FOX_HD) + EPS) * gain

    qn = head_norm(fq_ref[0], qg_ref[...]) * (FOX_HD ** -0.5 * LOG2E)
    kn = head_norm(fk_ref[0], kg_ref[...])
    v = fv_ref[0].astype(F32)
    for p in range(FOX_PAIRS):
        src = slice(LANES * p, LANES * (p + 1))
        dst_x = slice(2 * LANES * p, 2 * LANES * p + LANES)
        dst_a = slice(2 * LANES * p + LANES, 2 * LANES * (p + 1))
        kf_ref[0, :, dst_x] = kn[:, src].astype(BF16)
        kf_ref[0, :, dst_a] = kaug[:, src].astype(BF16)
        qt_ref[0, p, 0, 0:LANES, :] = qn[:, src].T.astype(BF16)
        qt_ref[0, p, 0, LANES:2 * LANES, :] = qaug[:, src].T.astype(BF16)
        vt_ref[0, p, 0] = v[:, src].T.astype(BF16)


def _fox_prep(big3, small3, fbias, qgain, kgain, *, ts):
    b, s, _ = big3.shape
    consts = _fox_consts(ts)
    const_specs = [
        pl.BlockSpec((ts, ts), lambda bi, i: (0, 0)),
        pl.BlockSpec((MIX_W, MIX_W), lambda bi, i: (0, 0)),
        pl.BlockSpec((3, LANES, MIX_W), lambda bi, i: (0, 0, 0)),
        pl.BlockSpec((3, LANES, MIX_W), lambda bi, i: (0, 0, 0)),
        pl.BlockSpec((1, MIX_W), lambda bi, i: (0, 0)),
        pl.BlockSpec((1, MIX_W), lambda bi, i: (0, 0)),
    ]
    return pl.pallas_call(
        functools.partial(_fox_prep_kernel, ts=ts),
        grid=(b, s // ts),
        in_specs=[
            pl.BlockSpec((1, ts, MIX_W), lambda bi, i: (bi, i, FQ_OFF // MIX_W)),
            pl.BlockSpec((1, ts, MIX_W), lambda bi, i: (bi, i, FK_OFF // MIX_W)),
            pl.BlockSpec((1, ts, MIX_W), lambda bi, i: (bi, i, FV_OFF // MIX_W)),
            pl.BlockSpec((1, ts, LANES), lambda bi, i: (bi, i, 0)),
            pl.BlockSpec((1, LANES), lambda bi, i: (0, 0)),
            pl.BlockSpec((1, MIX_W), lambda bi, i: (0, 0)),
            pl.BlockSpec((1, MIX_W), lambda bi, i: (0, 0)),
        ] + const_specs,
        out_specs=[
            pl.BlockSpec((1, FOX_PAIRS, 1, 2 * LANES, ts), lambda bi, i: (bi, 0, i, 0, 0)),
            pl.BlockSpec((1, ts, 2 * MIX_W), lambda bi, i: (bi, i, 0)),
            pl.BlockSpec((1, FOX_PAIRS, 1, LANES, ts), lambda bi, i: (bi, 0, i, 0, 0)),
        ],
        out_shape=[
            jax.ShapeDtypeStruct((b, FOX_PAIRS, s // ts, 2 * LANES, ts), BF16),
            jax.ShapeDtypeStruct((b, s, 2 * MIX_W), BF16),
            jax.ShapeDtypeStruct((b, FOX_PAIRS, s // ts, LANES, ts), BF16),
        ],
        scratch_shapes=[pltpu.VMEM((8, LANES), F32)],
        compiler_params=_params(("arbitrary", "arbitrary")),
        name="fox_prep",
    )(big3, big3, big3, small3, fbias, qgain, kgain, *consts)


FOX_NOSHIFT_BOUND = 40.0
FOX_TILE = 512


def _fox_attn_kernel(qt_ref, k_ref, vt_ref, o_ref, acc_ref, l_ref, m_ref, *, tq, online):
    i = pl.program_id(2)
    qt = qt_ref[0, 0, 0]
    qrow = lax.broadcasted_iota(jnp.int32, (2 * LANES, 1), 0)
    in_a = (qrow < FOX_HD) | ((qrow >= LANES) & (qrow < LANES + 6))
    in_b = ((qrow >= FOX_HD) & (qrow < LANES)) | ((qrow >= LANES + 6) & (qrow < LANES + 12))
    zero = jnp.zeros_like(qt)
    qt_heads = (jnp.where(in_a, qt, zero), jnp.where(in_b, qt, zero))

    acc_ref[...] = jnp.zeros_like(acc_ref)
    l_ref[...] = jnp.zeros_like(l_ref)
    if online:
        m_ref[...] = jnp.full_like(m_ref, NEG_BIG)

    def scores(j, a, diag):
        k = k_ref[0, pl.ds(pl.multiple_of(j * tq, tq), tq), :]
        st = _dot(k, qt_heads[a])
        if diag:
            krow = lax.broadcasted_iota(jnp.int32, (tq, tq), 0)
            qcol = lax.broadcasted_iota(jnp.int32, (tq, tq), 1)
            st = jnp.where(krow <= qcol, st, NEG_BIG)
        return st

    def accumulate(j, a, st):
        vt_a = vt_ref[0, 0, j, FOX_HD * a:FOX_HD * (a + 1), :]
        if online:
            m_prev = m_ref[a]
            m_new = jnp.maximum(m_prev, jnp.max(st, axis=0, keepdims=True))
            alpha = jnp.exp2(m_prev - m_new)
            m_ref[a] = m_new
            pt = jnp.exp2(st - m_new)
            l_ref[a] = alpha * l_ref[a] + jnp.sum(pt.reshape(tq // 8, 8, tq), axis=0)
            acc_ref[a] = alpha * acc_ref[a] + _dot(vt_a, pt.astype(BF16))
        else:
            pt = jnp.exp2(st)
            l_ref[a] += jnp.sum(pt.reshape(tq // 8, 8, tq), axis=0)
            acc_ref[a] += _dot(vt_a, pt.astype(BF16))

    def run(units):
        st = scores(*units[0])
        for u, unit in enumerate(units):
            st_next = scores(*units[u + 1]) if u + 1 < len(units) else None
            accumulate(unit[0], unit[1], st)
            st = st_next

    def body(jj, carry):
        run([(2 * jj, 0, False), (2 * jj, 1, False), (2 * jj + 1, 0, False), (2 * jj + 1, 1, False)])
        return carry

    lax.fori_loop(0, i // 2, body, 0)

    @pl.when(i % 2 == 1)
    def _():
        run([(i - 1, 0, False), (i - 1, 1, False), (i, 0, True), (i, 1, True)])

    @pl.when(i % 2 == 0)
    def _():
        run([(i, 0, True), (i, 1, True)])

    halves = [acc_ref[a] * (1.0 / jnp.sum(l_ref[a], axis=0, keepdims=True)) for a in range(2)]
    o_ref[0] = jnp.concatenate(halves, axis=0).T.astype(BF16)


def _fox_attn(qt, kf, vt, logit_bound):
    b, _, nq, _, tq = qt.shape
    s = nq * tq

    def call(online, name):
        return pl.pallas_call(
            functools.partial(_fox_attn_kernel, tq=tq, online=online),
            grid=(b, FOX_PAIRS, nq),
            in_specs=[
                pl.BlockSpec((1, 1, 1, 2 * LANES, tq), lambda bi, p, i: (bi, p, i, 0, 0)),
                pl.BlockSpec((1, s, 2 * LANES), lambda bi, p, i: (bi, 0, p)),
                pl.BlockSpec((1, 1, nq, LANES, tq), lambda bi, p, i: (bi, p, 0, 0, 0)),
            ],
            out_specs=pl.BlockSpec((1, tq, LANES), lambda bi, p, i: (bi, i, p)),
            out_shape=jax.ShapeDtypeStruct((b, s, MIX_W), BF16),
            scratch_shapes=[
                pltpu.VMEM((2, FOX_HD, tq), F32),
                pltpu.VMEM((2, 8, tq), F32),
                pltpu.VMEM((2, 1, tq), F32),
            ],
            compiler_params=_params(("arbitrary", "arbitrary", "arbitrary")),
            name=name,
        )(qt, kf, vt)

    return lax.cond(logit_bound < FOX_NOSHIFT_BOUND,
                    lambda: call(False, "fox_attn"), lambda: call(True, "fox_attn_online"))


def _gla_consts(r):
    idx = np.arange(r)
    same = (idx[:, None] // GLA_CHUNK) == (idx[None, :] // GLA_CHUNK)
    lblk = (same & (idx[None, :] <= idx[:, None])).astype(np.float32)
    ablk = same.astype(np.float32)
    return jnp.asarray(lblk, BF16), jnp.asarray(ablk, BF16)


def _gla_kernel(q_ref, k_ref, v_ref, r_ref, small_ref, wlh_ref, wll_ref, bg_ref, lblk_ref,
                ablk_ref, gain_ref, o_ref, st_ref, oacc_ref, *, r):
    @pl.when(pl.program_id(1) == 0)
    def _():
        st_ref[...] = jnp.zeros_like(st_ref)

    kw = GLA_HEADS * GLA_DK
    gate = _dot_f32w(small_ref[0], wlh_ref[...], wll_ref[...]) + bg_ref[...]
    log_a = _log_sigmoid(gate) * (1.0 / GLA_GATE_NORM)
    a1, a2, a3 = _split3(log_a)
    lblk = lblk_ref[...]
    ablk = ablk_ref[...]
    bcum = _dot(lblk, a1) + _dot(lblk, a2) + _dot(lblk, a3)
    btot = _dot(ablk, a1) + _dot(ablk, a2) + _dot(ablk, a3)
    q = q_ref[0].astype(F32) * (GLA_DK ** -0.5)
    k = k_ref[0].astype(F32)
    q_dec = (q * jnp.exp(bcum)).astype(BF16)
    k_dec = (k * jnp.exp(-bcum)).astype(BF16)
    k_end = (k * jnp.exp(btot - bcum)).astype(BF16)
    d_tot = jnp.exp(btot)
    v = v_ref[0]

    row = lax.broadcasted_iota(jnp.int32, (r, r), 0)
    col = lax.broadcasted_iota(jnp.int32, (r, r), 1)
    keep = (_shr(row, GLA_CHUNK) == _shr(col, GLA_CHUNK)) & (col <= row)
    klane = lax.broadcasted_iota(jnp.int32, (1, kw), 1)
    qzero = jnp.zeros_like(q_dec)
    for h in range(GLA_HEADS):
        in_h = (klane >= GLA_DK * h) & (klane < GLA_DK * (h + 1))
        att = _dot_nt(jnp.where(in_h, q_dec, qzero), k_dec)
        att = jnp.where(keep, att, 0.0).astype(BF16)
        vcols = slice(GLA_DV * h, GLA_DV * (h + 1))
        oacc_ref[:, vcols] = _dot(att, v[:, vcols])

    srow = lax.broadcasted_iota(jnp.int32, (MIX_W, kw), 0)
    scol = lax.broadcasted_iota(jnp.int32, (MIX_W, kw), 1)
    same_head = _shr(srow, GLA_DV) == _shr(scol, GLA_DK)
    for c in range(r // GLA_CHUNK):
        rows = slice(GLA_CHUNK * c, GLA_CHUNK * (c + 1))
        st = st_ref[...]
        oacc_ref[rows, :] += _dot_nt(q_dec[rows], st.astype(BF16))
        kv_t = _dot_tn(v[rows], k_end[rows])
        st_ref[...] = d_tot[GLA_CHUNK * c:GLA_CHUNK * c + 1, :] * st + jnp.where(same_head, kv_t, 0.0)

    gain = gain_ref[...]
    gr = r_ref[0].astype(F32)
    for h in range(GLA_HEADS):
        vcols = slice(GLA_DV * h, GLA_DV * (h + 1))
        o = oacc_ref[:, vcols]
        ms = jnp.mean(o * o, axis=-1, keepdims=True)
        o_ref[0, :, vcols] = (o * lax.rsqrt(ms + EPS) * gain * _silu(gr[:, vcols])).astype(BF16)


def _gla(big3, small3, wl_hi, wl_lo, bgate, gain, *, r=256):
    b, s, _ = big3.shape
    r = min(r, s)
    kw = GLA_HEADS * GLA_DK
    lblk, ablk = _gla_consts(r)
    return pl.pallas_call(
        functools.partial(_gla_kernel, r=r),
        grid=(b, s // r),
        in_specs=[
            pl.BlockSpec((1, r, kw), lambda bi, i: (bi, i, GQ_OFF // kw)),
            pl.BlockSpec((1, r, kw), lambda bi, i: (bi, i, GK_OFF // kw)),
            pl.BlockSpec((1, r, MIX_W), lambda bi, i: (bi, i, GV_OFF // MIX_W)),
            pl.BlockSpec((1, r, MIX_W), lambda bi, i: (bi, i, GR_OFF // MIX_W)),
            pl.BlockSpec((1, r, LANES), lambda bi, i: (bi, i, 0)),
            pl.BlockSpec((LANES, kw), lambda bi, i: (0, 0)),
            pl.BlockSpec((LANES, kw), lambda bi, i: (0, 0)),
            pl.BlockSpec((1, kw), lambda bi, i: (0, 0)),
            pl.BlockSpec((r, r), lambda bi, i: (0, 0)),
            pl.BlockSpec((r, r), lambda bi, i: (0, 0)),
            pl.BlockSpec((1, GLA_DV), lambda bi, i: (0, 0)),
        ],
        out_specs=pl.BlockSpec((1, r, MIX_W), lambda bi, i: (bi, i, 0)),
        out_shape=jax.ShapeDtypeStruct((b, s, MIX_W), BF16),
        scratch_shapes=[pltpu.VMEM((MIX_W, kw), F32), pltpu.VMEM((r, MIX_W), F32)],
        compiler_params=_params(("arbitrary", "arbitrary")),
        name="gla",
    )(big3, big3, big3, big3, small3, wl_hi, wl_lo, bgate, lblk, ablk, gain)


SSD_HALO = 16
SSD_CONV_W = MIX_W + 2 * SSM_GROUPS * SSM_STATE


def _ssd_consts(r):
    expand = np.zeros((LANES, MIX_W), np.float32)
    for h in range(SSM_HEADS):
        expand[SDT_LANE + h, SSM_HD * h:SSM_HD * (h + 1)] = 1.0
    rep = np.zeros((LANES, MIX_W), np.float32)
    for h in range(SSM_HEADS):
        g = h // (SSM_HEADS // SSM_GROUPS)
        for n in range(SSM_STATE):
            rep[SSM_STATE * g + n, SSM_HD * h + n] = 1.0
    ltri = np.tril(np.ones((r, r), np.float32))
    return (jnp.asarray(expand, BF16), jnp.asarray(rep, BF16), jnp.asarray(ltri, BF16),
            jnp.asarray(ltri.T, BF16))


def _ssd_kernel(z_ref, x_ref, xp_ref, b_ref, bp_ref, c_ref, cp_ref, small_ref, cw_ref, cb_ref,
                dtb_ref, alog_ref, dtbc_ref, alogc_ref, dskip_ref, onorm_ref, expand_ref,
                rep_ref, ltri_ref, utri_ref, o_ref, sw_ref, ext_ref, y_ref, *, r):
    first = pl.program_id(1) == 0

    @pl.when(first)
    def _():
        sw_ref[...] = jnp.zeros_like(sw_ref)

    keep_prev = jnp.where(first, 0.0, 1.0)
    ext_ref[0:SSD_HALO, 0:MIX_W] = xp_ref[0].astype(F32) * keep_prev
    ext_ref[0:SSD_HALO, MIX_W:MIX_W + LANES] = bp_ref[0].astype(F32) * keep_prev
    ext_ref[0:SSD_HALO, MIX_W + LANES:SSD_CONV_W] = cp_ref[0].astype(F32) * keep_prev
    ext_ref[SSD_HALO:, 0:MIX_W] = x_ref[0].astype(F32)
    ext_ref[SSD_HALO:, MIX_W:MIX_W + LANES] = b_ref[0].astype(F32)
    ext_ref[SSD_HALO:, MIX_W + LANES:SSD_CONV_W] = c_ref[0].astype(F32)
    conv = cb_ref[...] + cw_ref[SSM_CONV - 1:SSM_CONV, :] * ext_ref[SSD_HALO:, :]
    for back in range(1, SSM_CONV):
        tap = SSM_CONV - 1 - back
        conv = conv + cw_ref[tap:tap + 1, :] * ext_ref[pl.ds(SSD_HALO - back, r), :]
    xbc = _silu(conv)
    xs = xbc[:, 0:MIX_W]
    bm = xbc[:, MIX_W:MIX_W + LANES].astype(BF16)
    cm = xbc[:, MIX_W + LANES:SSD_CONV_W].astype(BF16)

    sm = small_ref[0]
    dt = _softplus(_dot3_right(sm, expand_ref[...]) + dtb_ref[...])
    a_neg = -jnp.exp(alog_ref[...])
    acs = _dot3_left(ltri_ref[...], dt * a_neg)
    acs_last = acs[r - 1:r, :]
    sm_t = sm.T
    dt_t = _softplus(sm_t[SDT_LANE:SDT_LANE + SSM_HEADS, :] + dtbc_ref[:, 0:1])
    acs_t = _dot3_right(dt_t * (-jnp.exp(alogc_ref[:, 0:1])), utri_ref[...])

    xdt = (xs * dt).astype(BF16)
    row = lax.broadcasted_iota(jnp.int32, (r, r), 0)
    col = lax.broadcasted_iota(jnp.int32, (r, r), 1)
    causal = col <= row
    glane = lax.broadcasted_iota(jnp.int32, (1, LANES), 1)
    first_half = glane < SSM_STATE
    czero = jnp.zeros_like(cm)
    hpg = SSM_HEADS // SSM_GROUPS
    for g in range(SSM_GROUPS):
        cg = jnp.where(first_half if g == 0 else ~first_half, cm, czero)
        cb_g = _dot_nt(cg, bm)
        for pair in range(hpg // 2):
            p = g * (hpg // 2) + pair
            xp = xdt[:, LANES * p:LANES * (p + 1)]
            xzero = jnp.zeros_like(xp)
            acc = None
            for a in range(2):
                h = 2 * p + a
                dmat = acs[:, SSM_HD * h:SSM_HD * h + 1] - acs_t[h:h + 1, :]
                sc = (cb_g * jnp.exp(jnp.where(causal, dmat, NEG_BIG))).astype(BF16)
                xh = jnp.where(first_half if a == 0 else ~first_half, xp, xzero)
                contrib = _dot(sc, xh)
                acc = contrib if acc is None else acc + contrib
            y_ref[:, LANES * p:LANES * (p + 1)] = acc

    rep = rep_ref[...]
    sw = sw_ref[...]
    cw = (_dot(cm, rep) * jnp.exp(acs)).astype(BF16)
    y = y_ref[...] + _dot(cw, sw.astype(BF16)) + dskip_ref[...] * xs
    bw = (_dot(bm, rep) * jnp.exp(acs_last - acs)).astype(BF16)
    upd = _dot_tn(bw, xdt)
    srow = lax.broadcasted_iota(jnp.int32, (MIX_W, MIX_W), 0)
    scol = lax.broadcasted_iota(jnp.int32, (MIX_W, MIX_W), 1)
    same_head = _shr(srow, SSM_STATE) == _shr(scol, SSM_HD)
    sw_ref[...] = sw * jnp.exp(acs_last) + jnp.where(same_head, upd, 0.0)

    y = y * _silu(z_ref[0].astype(F32))
    gw = MIX_W // SSM_GROUPS
    for g in range(SSM_GROUPS):
        cols = slice(gw * g, gw * (g + 1))
        yg = y[:, cols]
        ms = jnp.mean(yg * yg, axis=-1, keepdims=True)
        o_ref[0, :, cols] = (yg * lax.rsqrt(ms + EPS) * onorm_ref[:, cols]).astype(BF16)


def _ssd(big3, small3, conv_w, conv_b, dtb_w, alog_w, dtb_c, alog_c, dskip_w, onorm, *, r=256):
    b, s, _ = big3.shape
    r = min(r, s)
    consts = _ssd_consts(r)
    hb = r // SSD_HALO

    def cur(width, off):
        return pl.BlockSpec((1, r, width), lambda bi, i: (bi, i, off // width))

    def prev(width, off):
        return pl.BlockSpec((1, SSD_HALO, width),
                            lambda bi, i: (bi, jnp.maximum(i * hb - 1, 0), off // width))

    def whole(shape):
        return pl.BlockSpec(shape, lambda bi, i: (0,) * len(shape))

    return pl.pallas_call(
        functools.partial(_ssd_kernel, r=r),
        grid=(b, s // r),
        in_specs=[
            cur(MIX_W, SZ_OFF),
            cur(MIX_W, SX_OFF), prev(MIX_W, SX_OFF),
            cur(LANES, SB_OFF), prev(LANES, SB_OFF),
            cur(LANES, SC_OFF), prev(LANES, SC_OFF),
            pl.BlockSpec((1, r, LANES), lambda bi, i: (bi, i, 0)),
            whole((SSM_CONV, SSD_CONV_W)), whole((1, SSD_CONV_W)),
            whole((1, MIX_W)), whole((1, MIX_W)),
            whole((SSM_HEADS, LANES)), whole((SSM_HEADS, LANES)),
            whole((1, MIX_W)), whole((1, MIX_W)),
            whole((LANES, MIX_W)), whole((LANES, MIX_W)), whole((r, r)), whole((r, r)),
        ],
        out_specs=pl.BlockSpec((1, r, MIX_W), lambda bi, i: (bi, i, 0)),
        out_shape=jax.ShapeDtypeStruct((b, s, MIX_W), BF16),
        scratch_shapes=[
            pltpu.VMEM((MIX_W, MIX_W), F32),
            pltpu.VMEM((r + SSD_HALO, SSD_CONV_W), F32),
            pltpu.VMEM((r, MIX_W), F32),
        ],
        compiler_params=_params(("arbitrary", "arbitrary")),
        name="ssd",
    )(big3, big3, big3, big3, big3, big3, big3, small3, conv_w, conv_b, dtb_w, alog_w,
      dtb_c, alog_c, dskip_w, onorm, *consts)


def _merge_kernel(oa_ref, ob_ref, oc_ref, gate_ref, x_ref, wb_ref, bgate_ref, wo_ref, nffn_ref,
                  wrh_ref, wrl_ref, br_ref, lstrict_ref, xe_ref, route_ref, counts_ref, *, tm):
    @pl.when(pl.program_id(0) == 0)
    def _():
        counts_ref[...] = jnp.zeros_like(counts_ref)

    mixed = None
    for ridx, o_ref in enumerate((oa_ref, ob_ref, oc_ref)):
        cols = slice(D_MODEL * ridx, D_MODEL * (ridx + 1))
        gate = _sigmoid(gate_ref[:, cols].astype(F32) + bgate_ref[ridx:ridx + 1, :])
        term = gate * _dot(o_ref[...], wb_ref[ridx])
        mixed = term if mixed is None else mixed + term
    xn = x_ref[...] + _dot(mixed.astype(BF16), wo_ref[...])
    xe_ref[:, 0:D_MODEL] = xn
    ms = jnp.mean(xn * xn, axis=-1, keepdims=True)
    h = xn * lax.rsqrt(ms + EPS) * nffn_ref[...]

    logits = _dot_f32w(h, wrh_ref[...], wrl_ref[...]) + br_ref[...]
    lane = lax.broadcasted_iota(jnp.int32, (tm, LANES), 1)
    lane_f = lane.astype(F32)
    big_lane = float(LANES)
    is_grp = lane < RE_LANE
    gl = jnp.where(is_grp, logits, NEG_BIG)
    gmax = jnp.max(gl, axis=-1, keepdims=True)
    g_w = 1.0 / jnp.sum(jnp.exp(gl - gmax), axis=-1, keepdims=True)
    g_sel = jnp.min(jnp.where(gl == gmax, lane_f, big_lane), axis=-1, keepdims=True)
    grp_of_lane = _shr(jnp.maximum(lane - RE_LANE, 0), EXP_PER_GROUP).astype(F32)
    in_grp = (lane >= RE_LANE) & (lane < RE_LANE + N_EXPERTS) & (grp_of_lane == g_sel)
    el = jnp.where(in_grp, logits, NEG_BIG)
    e1 = jnp.max(el, axis=-1, keepdims=True)
    i1 = jnp.min(jnp.where(in_grp & (el == e1), lane_f, big_lane), axis=-1, keepdims=True)
    rest = in_grp & (lane_f != i1)
    el2 = jnp.where(rest, logits, NEG_BIG)
    e2 = jnp.max(el2, axis=-1, keepdims=True)
    i2 = jnp.min(jnp.where(rest & (el2 == e2), lane_f, big_lane), axis=-1, keepdims=True)
    ratio = jnp.exp(e2 - e1)
    w1 = g_w / (1.0 + ratio)
    w2 = w1 * ratio
    xe_ref[:, D_MODEL:] = jnp.where(lane_f == i1, w1, 0.0) + jnp.where(lane_f == i2, w2, 0.0)

    first_lane = RE_LANE + EXP_PER_GROUP * g_sel
    lo = jnp.minimum(i1, i2) - first_lane
    hi = jnp.maximum(i1, i2) - first_lane
    cls = PAIRS_PER_GROUP * g_sel + lo * (7.0 - lo) * 0.5 + (hi - lo - 1.0)
    is_cls = lane_f == cls
    onehot = jnp.where(is_cls, 1.0, 0.0).astype(BF16)
    before = _dot(lstrict_ref[...], onehot) + counts_ref[0:1, :]
    rank = jnp.sum(jnp.where(is_cls, before, 0.0), axis=-1, keepdims=True)
    route_ref[...] = jnp.where(lane == 0, cls, jnp.where(lane == 1, rank, 0.0))
    total = _dot(jnp.ones((8, tm), BF16), onehot)
    counts_ref[...] = counts_ref[...] + total


def _merge(oa, ob, oc, big, x2, wb, bgate, wo, nffn, wr_hi, wr_lo, br, *, tm=512):
    t = x2.shape[0]
    tm = min(tm, t)
    lstrict = jnp.asarray(np.tril(np.ones((tm, tm), np.float32), -1), BF16)

    def whole(shape):
        return pl.BlockSpec(shape, lambda i: (0,) * len(shape))

    return pl.pallas_call(
        functools.partial(_merge_kernel, tm=tm),
        grid=(t // tm,),
        in_specs=[
            pl.BlockSpec((tm, MIX_W), lambda i: (i, 0)),
            pl.BlockSpec((tm, MIX_W), lambda i: (i, 0)),
            pl.BlockSpec((tm, MIX_W), lambda i: (i, 0)),
            pl.BlockSpec((tm, 3 * D_MODEL), lambda i: (i, 0)),
            pl.BlockSpec((tm, D_MODEL), lambda i: (i, 0)),
            whole((3, MIX_W, D_MODEL)), whole((3, D_MODEL)), whole((D_MODEL, D_MODEL)),
            whole((1, D_MODEL)), whole((D_MODEL, LANES)), whole((D_MODEL, LANES)),
            whole((1, LANES)), whole((tm, tm)),
        ],
        out_specs=[
            pl.BlockSpec((tm, XE_COLS), lambda i: (i, 0)),
            pl.BlockSpec((tm, LANES), lambda i: (i, 0)),
            pl.BlockSpec((8, LANES), lambda i: (0, 0)),
        ],
        out_shape=[
            jax.ShapeDtypeStruct((t, XE_COLS), F32),
            jax.ShapeDtypeStruct((t, LANES), F32),
            jax.ShapeDtypeStruct((8, LANES), F32),
        ],
        compiler_params=_params(("arbitrary",)),
        name="merge",
    )(oa, ob, oc, big, x2, wb, bgate, wo, nffn, wr_hi, wr_lo, br, lstrict)


MOE_TILE = 256
ROW_DMA_TILE = 512


def _row_copy_kernel(pos_ref, src_ref, *rest, tm, scatter):
    dst_ref, sem = rest[-2], rest[-1]

    def copy(r):
        near, far = pl.ds(r, 1), pl.ds(pos_ref[0, 0, r], 1)
        if scatter:
            return pltpu.make_async_copy(src_ref.at[near], dst_ref.at[far], sem)
        return pltpu.make_async_copy(src_ref.at[far], dst_ref.at[near], sem)

    def issue(r, carry):
        copy(r).start()
        return carry

    def drain(r, carry):
        copy(r).wait()
        return carry

    lax.fori_loop(0, tm, issue, 0, unroll=8)
    lax.fori_loop(0, tm, drain, 0, unroll=8)


def _row_copy(pos, src, dst_init, out_rows, *, name):
    t = pos.shape[0]
    tm = min(ROW_DMA_TILE, t)
    width = src.shape[1]
    any_spec = pl.BlockSpec(memory_space=pl.ANY)
    tile_spec = pl.BlockSpec((tm, width), lambda i: (i, 0))
    pos_spec = pl.BlockSpec((1, 1, tm), lambda i: (i, 0, 0), memory_space=pltpu.SMEM)
    scatter = dst_init is not None
    operands = [pos.reshape(t // tm, 1, tm), src] + ([dst_init] if scatter else [])
    in_specs = [pos_spec] + ([tile_spec, any_spec] if scatter else [any_spec])
    return pl.pallas_call(
        functools.partial(_row_copy_kernel, tm=tm, scatter=scatter),
        grid=(t // tm,),
        in_specs=in_specs,
        out_specs=any_spec if scatter else tile_spec,
        out_shape=jax.ShapeDtypeStruct((out_rows, width), src.dtype),
        scratch_shapes=[pltpu.SemaphoreType.DMA(())],
        input_output_aliases={2: 0} if scatter else {},
        compiler_params=_params(("arbitrary",)),
        name=name,
    )(*operands)


def _moe_sorted_kernel(ea_ref, eb_ref, nused_ref, xs_ref, nffn_ref, wga_ref, wua_ref, wda_ref,
                       wgb_ref, wub_ref, wdb_ref, o_ref, *, tm):
    i = pl.program_id(0)

    @pl.when(i < nused_ref[0])
    def _():
        xn = xs_ref[:, 0:D_MODEL]
        comb = xs_ref[:, D_MODEL:]
        ms = jnp.mean(xn * xn, axis=-1, keepdims=True)
        h = (xn * lax.rsqrt(ms + EPS) * nffn_ref[...]).astype(BF16)
        lane = lax.broadcasted_iota(jnp.int32, (tm, LANES), 1)
        out = xn
        for e_ref, wg_ref, wu_ref, wd_ref in ((ea_ref, wga_ref, wua_ref, wda_ref),
                                              (eb_ref, wgb_ref, wub_ref, wdb_ref)):
            w = jnp.sum(jnp.where(lane == e_ref[i] + RE_LANE, comb, 0.0), axis=-1, keepdims=True)
            hid = _silu(_dot(h, wg_ref[0])) * _dot(h, wu_ref[0])
            out = out + w * _dot(hid.astype(BF16), wd_ref[0])
        o_ref[...] = out

    @pl.when(i >= nused_ref[0])
    def _():
        o_ref[...] = jnp.zeros_like(o_ref)


def _moe_sorted(ea, eb, nused, xs, nffn, wg, wu, wd, *, tm):
    n_tiles = xs.shape[0] // tm

    def w_in(which):
        return pl.BlockSpec((1, D_MODEL, D_EXPERT), lambda i, ea, eb, nu: ((ea, eb)[which][i], 0, 0))

    def w_out(which):
        return pl.BlockSpec((1, D_EXPERT, D_MODEL), lambda i, ea, eb, nu: ((ea, eb)[which][i], 0, 0))

    grid_spec = pltpu.PrefetchScalarGridSpec(
        num_scalar_prefetch=3,
        grid=(n_tiles,),
        in_specs=[
            pl.BlockSpec((tm, XE_COLS), lambda i, ea, eb, nu: (i, 0)),
            pl.BlockSpec((1, D_MODEL), lambda i, ea, eb, nu: (0, 0)),
            w_in(0), w_in(0), w_out(0), w_in(1), w_in(1), w_out(1),
        ],
        out_specs=pl.BlockSpec((tm, D_MODEL), lambda i, ea, eb, nu: (i, 0)),
    )
    return pl.pallas_call(
        functools.partial(_moe_sorted_kernel, tm=tm),
        grid_spec=grid_spec,
        out_shape=jax.ShapeDtypeStruct((n_tiles * tm, D_MODEL), F32),
        compiler_params=_params(("arbitrary",)),
        name="moe",
    )(ea, eb, nused, xs, nffn, wg, wu, wd, wg, wu, wd)


_PAIR_LO = np.array([0, 0, 0, 1, 1, 2], np.int32)
_PAIR_HI = np.array([1, 2, 3, 2, 3, 3], np.int32)


def _moe(xe, route, counts, nffn, wg, wu, wd):
    t = xe.shape[0]
    tm = min(MOE_TILE, t)
    n_cls = N_EGROUPS * PAIRS_PER_GROUP
    n_tiles = t // tm + n_cls
    cnt = counts[0, :n_cls].astype(jnp.int32)
    tiles = (cnt + tm - 1) // tm
    tile_end = jnp.cumsum(tiles)
    n_used = tile_end[-1]
    cls_base = (tile_end - tiles) * tm
    tile_idx = jnp.minimum(jnp.arange(n_tiles), n_used - 1)
    tile_cls = jnp.sum((tile_end[None, :] <= tile_idx[:, None]).astype(jnp.int32), axis=1)
    grp, pair = tile_cls // PAIRS_PER_GROUP, tile_cls % PAIRS_PER_GROUP
    ea = EXP_PER_GROUP * grp + jnp.asarray(_PAIR_LO)[pair]
    eb = EXP_PER_GROUP * grp + jnp.asarray(_PAIR_HI)[pair]
    pos = cls_base[route[:, 0].astype(jnp.int32)] + route[:, 1].astype(jnp.int32)

    xs = _row_copy(pos, xe, jnp.zeros((n_tiles * tm, XE_COLS), F32), n_tiles * tm,
                   name="moe_scatter")
    ys = _moe_sorted(ea, eb, n_used.reshape(1), xs, nffn, wg, wu, wd, tm=tm)
    return _row_copy(pos, ys, None, t, name="moe_gather")


def _cols(w, off, width):
    return w[:, off:off + width]


def _pad_lanes(v, lane0, width=LANES):
    out = jnp.zeros((1, width), F32)
    return out.at[0, lane0:lane0 + v.shape[0]].set(v.astype(F32))


def _layer(x2, b, s, norm_mix, w_in, fox_f_bias, fox_q_norm, fox_k_norm, gla_w_lr, gla_b_gate,
           gla_out_norm, ssm_conv_w, ssm_conv_b, ssm_dt_bias, ssm_a_log, ssm_d, ssm_out_norm,
           w_branch, b_branch_gate, w_out, norm_ffn, w_router_grp, b_router_grp,
           w_router_exp, b_router_exp, w_exp_gate, w_exp_up, w_exp_down):
    t = b * s
    w_big = jnp.concatenate([
        _cols(w_in, _O_GATE, 3 * D_MODEL), _cols(w_in, _O_FQ, MIX_W), _cols(w_in, _O_FK, MIX_W),
        _cols(w_in, _O_FV, MIX_W), _cols(w_in, _O_GV, MIX_W), _cols(w_in, _O_GR, MIX_W),
        _cols(w_in, _O_SZ, MIX_W), _cols(w_in, _O_SX, MIX_W), _cols(w_in, _O_GQ, 256),
        _cols(w_in, _O_GK, 256), _cols(w_in, _O_SB, LANES), _cols(w_in, _O_SC, LANES),
    ], axis=1).astype(BF16)
    w_small = jnp.concatenate([
        _cols(w_in, _O_FF, FOX_HEADS), _cols(w_in, _O_GLR, GLA_RANK), _cols(w_in, _O_SDT, SSM_HEADS),
        jnp.zeros((D_MODEL, LANES - FOX_HEADS - GLA_RANK - SSM_HEADS), F32),
    ], axis=1)
    ws_hi, ws_lo = _split2(w_small)

    big, small = _inproj(x2, norm_mix.reshape(1, D_MODEL), w_big, ws_hi, ws_lo)
    big3 = big.reshape(b, s, BIG_COLS)
    small3 = small.reshape(b, s, LANES)

    qt, kf, vt = _fox_prep(big3, small3, _pad_lanes(fox_f_bias, FF_LANE),
                           jnp.tile(fox_q_norm, FOX_HEADS).reshape(1, MIX_W),
                           jnp.tile(fox_k_norm, FOX_HEADS).reshape(1, MIX_W), ts=min(FOX_TILE, s))
    logit_bound = ((FOX_HD ** 0.5) * 1.01 * jnp.max(jnp.abs(fox_q_norm))
                   * jnp.max(jnp.abs(fox_k_norm)))
    o_a = _fox_attn(qt, kf, vt, logit_bound)

    wl = jnp.zeros((LANES, GLA_HEADS * GLA_DK), F32).at[GLR_LANE:GLR_LANE + GLA_RANK].set(gla_w_lr)
    wl_hi, wl_lo = _split2(wl)
    o_b = _gla(big3, small3, wl_hi, wl_lo, gla_b_gate.reshape(1, -1),
               gla_out_norm.reshape(1, GLA_DV))

    o_c = _ssd(big3, small3, ssm_conv_w, ssm_conv_b.reshape(1, -1),
               jnp.repeat(ssm_dt_bias, SSM_HD).reshape(1, MIX_W),
               jnp.repeat(ssm_a_log, SSM_HD).reshape(1, MIX_W),
               jnp.broadcast_to(ssm_dt_bias[:, None], (SSM_HEADS, LANES)),
               jnp.broadcast_to(ssm_a_log[:, None], (SSM_HEADS, LANES)),
               jnp.repeat(ssm_d, SSM_HD).reshape(1, MIX_W),
               ssm_out_norm.reshape(1, MIX_W))

    w_r = jnp.concatenate([w_router_grp, w_router_exp,
                           jnp.zeros((D_MODEL, LANES - N_EGROUPS - N_EXPERTS), F32)], axis=1)
    wr_hi, wr_lo = _split2(w_r)
    b_r = jnp.concatenate([b_router_grp, b_router_exp,
                           jnp.zeros((LANES - N_EGROUPS - N_EXPERTS,), F32)]).reshape(1, LANES)
    nffn = norm_ffn.reshape(1, D_MODEL)
    xe, route, counts = _merge(o_a.reshape(t, MIX_W), o_b.reshape(t, MIX_W), o_c.reshape(t, MIX_W),
                               big, x2, w_branch.astype(BF16), b_branch_gate, w_out.astype(BF16),
                               nffn, wr_hi, wr_lo, b_r)

    return _moe(xe, route, counts, nffn, w_exp_gate.astype(BF16), w_exp_up.astype(BF16),
                w_exp_down.astype(BF16))


def kernel(x, norm_mix, w_in, fox_f_bias, fox_q_norm, fox_k_norm, gla_w_lr, gla_b_gate, gla_out_norm, ssm_conv_w, ssm_conv_b, ssm_dt_bias, ssm_a_log, ssm_d, ssm_out_norm, w_branch, b_branch_gate, w_out, norm_ffn, w_router_grp, b_router_grp, w_router_exp, b_router_exp, w_exp_gate, w_exp_up, w_exp_down):
    b, s, d = x.shape
    x2 = x.reshape(b * s, d)
    per_layer = (norm_mix, w_in, fox_f_bias, fox_q_norm, fox_k_norm, gla_w_lr, gla_b_gate,
                 gla_out_norm, ssm_conv_w, ssm_conv_b, ssm_dt_bias, ssm_a_log, ssm_d,
                 ssm_out_norm, w_branch, b_branch_gate, w_out, norm_ffn, w_router_grp,
                 b_router_grp, w_router_exp, b_router_exp, w_exp_gate, w_exp_up, w_exp_down)
    for l in range(norm_mix.shape[0]):
        x2 = _layer(x2, b, s, *[p[l] for p in per_layer])
    return x2.reshape(b, s, d)
```

```python
import functools

import numpy as np
import jax
import jax.numpy as jnp
from jax import lax
from jax.experimental import pallas as pl
from jax.experimental.pallas import tpu as pltpu

F32 = jnp.float32
BF16 = jnp.bfloat16

D_MODEL = 1024
MIX_W = 512
EPS = 1e-6
FOX_HEADS = 8
FOX_HD = 64
FOX_PAIRS = FOX_HEADS // 2
GLA_HEADS = 4
GLA_DK = 64
GLA_DV = 128
GLA_RANK = 16
GLA_GATE_NORM = 16.0
GLA_CHUNK = 64
SSM_HEADS = 8
SSM_HD = 64
SSM_GROUPS = 2
SSM_STATE = 64
SSM_CONV = 4
N_EGROUPS = 4
EXP_PER_GROUP = 4
N_EXPERTS = 16
PAIRS_PER_GROUP = 6
D_EXPERT = 512

LANES = 128
NEG_BIG = -1e30
LOG2E = 1.4426950408889634
VMEM_LIMIT = 56 * 1024 * 1024

GATE_OFF, FQ_OFF, FK_OFF, FV_OFF = 0, 3072, 3584, 4096
GV_OFF, GR_OFF, SZ_OFF, SX_OFF = 4608, 5120, 5632, 6144
GQ_OFF, GK_OFF, SB_OFF, SC_OFF = 6656, 6912, 7168, 7296
BIG_COLS = 7424
FF_LANE, GLR_LANE, SDT_LANE = 0, 8, 24
_O_FQ, _O_FK, _O_FV, _O_FF = 0, 512, 1024, 1536
_O_GQ, _O_GK, _O_GV, _O_GR, _O_GLR = 1544, 1800, 2056, 2568, 3080
_O_SZ, _O_SX, _O_SB, _O_SC, _O_SDT, _O_GATE = 3096, 3608, 4120, 4248, 4376, 4384
RG_LANE, RE_LANE = 0, 4
XE_COLS = D_MODEL + LANES


def _split2(x):
    hi = x.astype(BF16)
    lo = (x - hi.astype(F32)).astype(BF16)
    return hi, lo


def _split3(x):
    x1 = x.astype(BF16)
    r = x - x1.astype(F32)
    x2 = r.astype(BF16)
    x3 = (r - x2.astype(F32)).astype(BF16)
    return x1, x2, x3


def _dot(a, b):
    return jnp.dot(a, b, preferred_element_type=F32)


def _dot_nt(a, b):
    return lax.dot_general(a, b, (((1,), (1,)), ((), ())), preferred_element_type=F32)


def _dot_tn(a, b):
    return lax.dot_general(a, b, (((0,), (0,)), ((), ())), preferred_element_type=F32)


def _dot3_left(m_bf16, x_f32):
    x1, x2, x3 = _split3(x_f32)
    return _dot(m_bf16, x1) + _dot(m_bf16, x2) + _dot(m_bf16, x3)


def _dot3_right(x_f32, m_bf16):
    x1, x2, x3 = _split3(x_f32)
    return _dot(x1, m_bf16) + _dot(x2, m_bf16) + _dot(x3, m_bf16)


def _dot_f32w(x_f32, w_hi, w_lo):
    x_hi, x_lo = _split2(x_f32)
    return _dot(x_hi, w_hi) + _dot(x_lo, w_hi) + _dot(x_hi, w_lo)


def _shr(x, pow2):
    return jnp.right_shift(x, pow2.bit_length() - 1)


def _log_sigmoid(x):
    return jnp.minimum(x, 0.0) - jnp.log(1.0 + jnp.exp(-jnp.abs(x)))


def _softplus(x):
    return jnp.maximum(x, 0.0) + jnp.log(1.0 + jnp.exp(-jnp.abs(x)))


def _sigmoid(x):
    return 0.5 * jnp.tanh(0.5 * x) + 0.5


def _silu(x):
    return x * _sigmoid(x)


def _params(sem):
    return pltpu.CompilerParams(dimension_semantics=sem, vmem_limit_bytes=VMEM_LIMIT)


def _inproj_kernel(x_ref, g_ref, w_ref, wsh_ref, wsl_ref, big_ref, small_ref, *, tn):
    x = x_ref[...]
    ms = jnp.mean(x * x, axis=-1, keepdims=True)
    h = x * lax.rsqrt(ms + EPS) * g_ref[...]
    hb = h.astype(BF16)
    for c in range(BIG_COLS // tn):
        cols = slice(c * tn, (c + 1) * tn)
        big_ref[:, cols] = _dot(hb, w_ref[:, cols]).astype(BF16)
    h_lo = (h - hb.astype(F32)).astype(BF16)
    wsh = wsh_ref[...]
    small_ref[...] = _dot(hb, wsh) + _dot(h_lo, wsh) + _dot(hb, wsl_ref[...])


def _inproj(x2, gain, w_big, ws_hi, ws_lo, *, tm=512, tn=256):
    t = x2.shape[0]
    return pl.pallas_call(
        functools.partial(_inproj_kernel, tn=tn),
        grid=(t // tm,),
        in_specs=[
            pl.BlockSpec((tm, D_MODEL), lambda i: (i, 0)),
            pl.BlockSpec((1, D_MODEL), lambda i: (0, 0)),
            pl.BlockSpec((D_MODEL, BIG_COLS), lambda i: (0, 0), pipeline_mode=pl.Buffered(1)),
            pl.BlockSpec((D_MODEL, LANES), lambda i: (0, 0)),
            pl.BlockSpec((D_MODEL, LANES), lambda i: (0, 0)),
        ],
        out_specs=[
            pl.BlockSpec((tm, BIG_COLS), lambda i: (i, 0)),
            pl.BlockSpec((tm, LANES), lambda i: (i, 0)),
        ],
        out_shape=[
            jax.ShapeDtypeStruct((t, BIG_COLS), BF16),
            jax.ShapeDtypeStruct((t, LANES), F32),
        ],
        compiler_params=_params(("arbitrary",)),
        name="inproj",
    )(x2, gain, w_big, ws_hi, ws_lo)


def _fox_consts(ts):
    ltri = np.tril(np.ones((ts, ts), np.float32))
    hsum = np.kron(np.eye(FOX_HEADS, dtype=np.float32), np.ones((FOX_HD, FOX_HD), np.float32))
    sq = np.zeros((3, LANES, MIX_W), np.float32)
    sk = np.zeros((3, LANES, MIX_W), np.float32)
    oneq = np.zeros((1, MIX_W), np.float32)
    onek = np.zeros((1, MIX_W), np.float32)
    for h in range(FOX_HEADS):
        base = LANES * (h // 2) + 6 * (h % 2)
        for j in range(3):
            sq[j, FF_LANE + h, base + j] = 1.0
            sk[j, FF_LANE + h, base + 3 + j] = -1.0
            oneq[0, base + 3 + j] = 1.0
            onek[0, base + j] = 1.0
    return (jnp.asarray(ltri, BF16), jnp.asarray(hsum, BF16), jnp.asarray(sq, BF16),
            jnp.asarray(sk, BF16), jnp.asarray(oneq), jnp.asarray(onek))


def _fox_prep_kernel(fq_ref, fk_ref, fv_ref, small_ref, fbias_ref, qg_ref, kg_ref, ltri_ref,
                     hsum_ref, sq_ref, sk_ref, oneq_ref, onek_ref, qt_ref, kf_ref, vt_ref, carry_ref,
                     *, ts):
    @pl.when(pl.program_id(1) == 0)
    def _():
        carry_ref[...] = jnp.zeros_like(carry_ref)

    lane = lax.broadcasted_iota(jnp.int32, (ts, LANES), 1)
    f = small_ref[0] + fbias_ref[...]
    ls = jnp.where(lane < FOX_HEADS, _log_sigmoid(f) * LOG2E, 0.0)
    c = _dot3_left(ltri_ref[...], ls) + carry_ref[0:1, :]
    carry_ref[...] = jnp.broadcast_to(c[ts - 1:ts, :], carry_ref.shape)
    c1, c2, c3 = _split3(c)
    qaug = _dot(c1, sq_ref[0]) + _dot(c2, sq_ref[1]) + _dot(c3, sq_ref[2]) + oneq_ref[...]
    kaug = _dot(c1, sk_ref[0]) + _dot(c2, sk_ref[1]) + _dot(c3, sk_ref[2]) + onek_ref[...]

    hsum = hsum_ref[...]

    def head_norm(xb, gain):
        x = xb.astype(F32)
        s_hi, s_lo = _split2(x * x)
        ss = _dot(s_hi, hsum) + _dot(s_lo, hsum)
        return x * lax.rsqrt(ss * (1.0 / FOX_HD) + EPS) * gain

    qn = head_norm(fq_ref[0], qg_ref[...]) * (FOX_HD ** -0.5 * LOG2E)
    kn = head_norm(fk_ref[0], kg_ref[...])
    v = fv_ref[0].astype(F32)
    for p in range(FOX_PAIRS):
        src = slice(LANES * p, LANES * (p + 1))
        dst_x = slice(2 * LANES * p, 2 * LANES * p + LANES)
        dst_a = slice(2 * LANES * p + LANES, 2 * LANES * (p + 1))
        kf_ref[0, :, dst_x] = kn[:, src].astype(BF16)
        kf_ref[0, :, dst_a] = kaug[:, src].astype(BF16)
        qt_ref[0, p, 0, 0:LANES, :] = qn[:, src].T.astype(BF16)
        qt_ref[0, p, 0, LANES:2 * LANES, :] = qaug[:, src].T.astype(BF16)
        vt_ref[0, p, 0] = v[:, src].T.astype(BF16)


def _fox_prep(big3, small3, fbias, qgain, kgain, *, ts):
    b, s, _ = big3.shape
    consts = _fox_consts(ts)
    const_specs = [
        pl.BlockSpec((ts, ts), lambda bi, i: (0, 0)),
        pl.BlockSpec((MIX_W, MIX_W), lambda bi, i: (0, 0)),
        pl.BlockSpec((3, LANES, MIX_W), lambda bi, i: (0, 0, 0)),
        pl.BlockSpec((3, LANES, MIX_W), lambda bi, i: (0, 0, 0)),
        pl.BlockSpec((1, MIX_W), lambda bi, i: (0, 0)),
        pl.BlockSpec((1, MIX_W), lambda bi, i: (0, 0)),
    ]
    return pl.pallas_call(
        functools.partial(_fox_prep_kernel, ts=ts),
        grid=(b, s // ts),
        in_specs=[
            pl.BlockSpec((1, ts, MIX_W), lambda bi, i: (bi, i, FQ_OFF // MIX_W)),
            pl.BlockSpec((1, ts, MIX_W), lambda bi, i: (bi, i, FK_OFF // MIX_W)),
            pl.BlockSpec((1, ts, MIX_W), lambda bi, i: (bi, i, FV_OFF // MIX_W)),
            pl.BlockSpec((1, ts, LANES), lambda bi, i: (bi, i, 0)),
            pl.BlockSpec((1, LANES), lambda bi, i: (0, 0)),
            pl.BlockSpec((1, MIX_W), lambda bi, i: (0, 0)),
            pl.BlockSpec((1, MIX_W), lambda bi, i: (0, 0)),
        ] + const_specs,
        out_specs=[
            pl.BlockSpec((1, FOX_PAIRS, 1, 2 * LANES, ts), lambda bi, i: (bi, 0, i, 0, 0)),
            pl.BlockSpec((1, ts, 2 * MIX_W), lambda bi, i: (bi, i, 0)),
            pl.BlockSpec((1, FOX_PAIRS, 1, LANES, ts), lambda bi, i: (bi, 0, i, 0, 0)),
        ],
        out_shape=[
            jax.ShapeDtypeStruct((b, FOX_PAIRS, s // ts, 2 * LANES, ts), BF16),
            jax.ShapeDtypeStruct((b, s, 2 * MIX_W), BF16),
            jax.ShapeDtypeStruct((b, FOX_PAIRS, s // ts, LANES, ts), BF16),
        ],
        scratch_shapes=[pltpu.VMEM((8, LANES), F32)],
        compiler_params=_params(("arbitrary", "arbitrary")),
        name="fox_prep",
    )(big3, big3, big3, small3, fbias, qgain, kgain, *consts)


FOX_NOSHIFT_BOUND = 40.0
FOX_TILE = 512
FOX_UNROLL = 4


def _fox_attn_kernel(qt_ref, k_ref, vt_ref, o_ref, acc_ref, l_ref, m_ref, *, tq, online):
    i = pl.program_id(2)
    qt = qt_ref[0, 0, 0]
    qrow = lax.broadcasted_iota(jnp.int32, (2 * LANES, 1), 0)
    in_a = (qrow < FOX_HD) | ((qrow >= LANES) & (qrow < LANES + 6))
    in_b = ((qrow >= FOX_HD) & (qrow < LANES)) | ((qrow >= LANES + 6) & (qrow < LANES + 12))
    zero = jnp.zeros_like(qt)
    qt_heads = (jnp.where(in_a, qt, zero), jnp.where(in_b, qt, zero))

    acc_ref[...] = jnp.zeros_like(acc_ref)
    l_ref[...] = jnp.zeros_like(l_ref)
    if online:
        m_ref[...] = jnp.full_like(m_ref, NEG_BIG)

    def scores(j, a, diag):
        k = k_ref[0, pl.ds(pl.multiple_of(j * tq, tq), tq), :]
        st = _dot(k, qt_heads[a])
        if diag:
            krow = lax.broadcasted_iota(jnp.int32, (tq, tq), 0)
            qcol = lax.broadcasted_iota(jnp.int32, (tq, tq), 1)
            st = jnp.where(krow <= qcol, st, NEG_BIG)
        return st

    def accumulate(j, a, st):
        vt_a = vt_ref[0, 0, j, FOX_HD * a:FOX_HD * (a + 1), :]
        if online:
            m_prev = m_ref[a]
            m_new = jnp.maximum(m_prev, jnp.max(st, axis=0, keepdims=True))
            alpha = jnp.exp2(m_prev - m_new)
            m_ref[a] = m_new
            pt = jnp.exp2(st - m_new)
            l_ref[a] = alpha * l_ref[a] + jnp.sum(pt.reshape(tq // 8, 8, tq), axis=0)
            acc_ref[a] = alpha * acc_ref[a] + _dot(vt_a, pt.astype(BF16))
        else:
            pt = jnp.exp2(st)
            l_ref[a] += jnp.sum(pt.reshape(tq // 8, 8, tq), axis=0)
            acc_ref[a] += _dot(vt_a, pt.astype(BF16))

    def run(units):
        st = scores(*units[0])
        for u, unit in enumerate(units):
            st_next = scores(*units[u + 1]) if u + 1 < len(units) else None
            accumulate(unit[0], unit[1], st)
            st = st_next

    def units(first_block, n_regular, with_diag):
        blocks = [(first_block + d, False) for d in range(n_regular)]
        if with_diag:
            blocks.append((first_block + n_regular, True))
        return [(j, a, diag) for j, diag in blocks for a in range(2)]

    def body(jj, carry):
        run(units(FOX_UNROLL * jj, FOX_UNROLL, False))
        return carry

    lax.fori_loop(0, i // FOX_UNROLL, body, 0)
    for rem in range(FOX_UNROLL):
        @pl.when(i % FOX_UNROLL == rem)
        def _():
            run(units(i - rem, rem, True))

    halves = [acc_ref[a] * (1.0 / jnp.sum(l_ref[a], axis=0, keepdims=True)) for a in range(2)]
    o_ref[0] = jnp.concatenate(halves, axis=0).T.astype(BF16)


def _fox_attn(qt, kf, vt, logit_bound):
    b, _, nq, _, tq = qt.shape
    s = nq * tq

    def call(online, name):
        return pl.pallas_call(
            functools.partial(_fox_attn_kernel, tq=tq, online=online),
            grid=(b, FOX_PAIRS, nq),
            in_specs=[
                pl.BlockSpec((1, 1, 1, 2 * LANES, tq), lambda bi, p, i: (bi, p, i, 0, 0)),
                pl.BlockSpec((1, s, 2 * LANES), lambda bi, p, i: (bi, 0, p)),
                pl.BlockSpec((1, 1, nq, LANES, tq), lambda bi, p, i: (bi, p, 0, 0, 0)),
            ],
            out_specs=pl.BlockSpec((1, tq, LANES), lambda bi, p, i: (bi, i, p)),
            out_shape=jax.ShapeDtypeStruct((b, s, MIX_W), BF16),
            scratch_shapes=[
                pltpu.VMEM((2, FOX_HD, tq), F32),
                pltpu.VMEM((2, 8, tq), F32),
                pltpu.VMEM((2, 1, tq), F32),
            ],
            compiler_params=_params(("arbitrary", "arbitrary", "arbitrary")),
            name=name,
        )(qt, kf, vt)

    return lax.cond(logit_bound < FOX_NOSHIFT_BOUND,
                    lambda: call(False, "fox_attn"), lambda: call(True, "fox_attn_online"))


def _gla_consts(r):
    idx = np.arange(r)
    same = (idx[:, None] // GLA_CHUNK) == (idx[None, :] // GLA_CHUNK)
    lblk = (same & (idx[None, :] <= idx[:, None])).astype(np.float32)
    ablk = same.astype(np.float32)
    return jnp.asarray(lblk, BF16), jnp.asarray(ablk, BF16)


def _gla_kernel(q_ref, k_ref, v_ref, r_ref, small_ref, wlh_ref, wll_ref, bg_ref, lblk_ref,
                ablk_ref, gain_ref, o_ref, st_ref, oacc_ref, *, r):
    @pl.when(pl.program_id(1) == 0)
    def _():
        st_ref[...] = jnp.zeros_like(st_ref)

    kw = GLA_HEADS * GLA_DK
    gate = _dot_f32w(small_ref[0], wlh_ref[...], wll_ref[...]) + bg_ref[...]
    log_a = _log_sigmoid(gate) * (1.0 / GLA_GATE_NORM)
    a1, a2, a3 = _split3(log_a)
    lblk = lblk_ref[...]
    ablk = ablk_ref[...]
    bcum = _dot(lblk, a1) + _dot(lblk, a2) + _dot(lblk, a3)
    btot = _dot(ablk, a1) + _dot(ablk, a2) + _dot(ablk, a3)
    q = q_ref[0].astype(F32) * (GLA_DK ** -0.5)
    k = k_ref[0].astype(F32)
    q_dec = (q * jnp.exp(bcum)).astype(BF16)
    k_dec = (k * jnp.exp(-bcum)).astype(BF16)
    k_end = (k * jnp.exp(btot - bcum)).astype(BF16)
    d_tot = jnp.exp(btot)
    v = v_ref[0]

    row = lax.broadcasted_iota(jnp.int32, (r, r), 0)
    col = lax.broadcasted_iota(jnp.int32, (r, r), 1)
    keep = (_shr(row, GLA_CHUNK) == _shr(col, GLA_CHUNK)) & (col <= row)
    klane = lax.broadcasted_iota(jnp.int32, (1, kw), 1)
    qzero = jnp.zeros_like(q_dec)
    for h in range(GLA_HEADS):
        in_h = (klane >= GLA_DK * h) & (klane < GLA_DK * (h + 1))
        att = _dot_nt(jnp.where(in_h, q_dec, qzero), k_dec)
        att = jnp.where(keep, att, 0.0).astype(BF16)
        vcols = slice(GLA_DV * h, GLA_DV * (h + 1))
        oacc_ref[:, vcols] = _dot(att, v[:, vcols])

    srow = lax.broadcasted_iota(jnp.int32, (MIX_W, kw), 0)
    scol = lax.broadcasted_iota(jnp.int32, (MIX_W, kw), 1)
    same_head = _shr(srow, GLA_DV) == _shr(scol, GLA_DK)
    for c in range(r // GLA_CHUNK):
        rows = slice(GLA_CHUNK * c, GLA_CHUNK * (c + 1))
        st = st_ref[...]
        oacc_ref[rows, :] += _dot_nt(q_dec[rows], st.astype(BF16))
        kv_t = _dot_tn(v[rows], k_end[rows])
        st_ref[...] = d_tot[GLA_CHUNK * c:GLA_CHUNK * c + 1, :] * st + jnp.where(same_head, kv_t, 0.0)

    gain = gain_ref[...]
    gr = r_ref[0].astype(F32)
    for h in range(GLA_HEADS):
        vcols = slice(GLA_DV * h, GLA_DV * (h + 1))
        o = oacc_ref[:, vcols]
        ms = jnp.mean(o * o, axis=-1, keepdims=True)
        o_ref[0, :, vcols] = (o * lax.rsqrt(ms + EPS) * gain * _silu(gr[:, vcols])).astype(BF16)


def _gla(big3, small3, wl_hi, wl_lo, bgate, gain, *, r=256):
    b, s, _ = big3.shape
    r = min(r, s)
    kw = GLA_HEADS * GLA_DK
    lblk, ablk = _gla_consts(r)
    return pl.pallas_call(
        functools.partial(_gla_kernel, r=r),
        grid=(b, s // r),
        in_specs=[
            pl.BlockSpec((1, r, kw), lambda bi, i: (bi, i, GQ_OFF // kw)),
            pl.BlockSpec((1, r, kw), lambda bi, i: (bi, i, GK_OFF // kw)),
            pl.BlockSpec((1, r, MIX_W), lambda bi, i: (bi, i, GV_OFF // MIX_W)),
            pl.BlockSpec((1, r, MIX_W), lambda bi, i: (bi, i, GR_OFF // MIX_W)),
            pl.BlockSpec((1, r, LANES), lambda bi, i: (bi, i, 0)),
            pl.BlockSpec((LANES, kw), lambda bi, i: (0, 0)),
            pl.BlockSpec((LANES, kw), lambda bi, i: (0, 0)),
            pl.BlockSpec((1, kw), lambda bi, i: (0, 0)),
            pl.BlockSpec((r, r), lambda bi, i: (0, 0)),
            pl.BlockSpec((r, r), lambda bi, i: (0, 0)),
            pl.BlockSpec((1, GLA_DV), lambda bi, i: (0, 0)),
        ],
        out_specs=pl.BlockSpec((1, r, MIX_W), lambda bi, i: (bi, i, 0)),
        out_shape=jax.ShapeDtypeStruct((b, s, MIX_W), BF16),
        scratch_shapes=[pltpu.VMEM((MIX_W, kw), F32), pltpu.VMEM((r, MIX_W), F32)],
        compiler_params=_params(("arbitrary", "arbitrary")),
        name="gla",
    )(big3, big3, big3, big3, small3, wl_hi, wl_lo, bgate, lblk, ablk, gain)


SSD_HALO = 16
SSD_CONV_W = MIX_W + 2 * SSM_GROUPS * SSM_STATE


def _ssd_consts(r):
    expand = np.zeros((LANES, MIX_W), np.float32)
    for h in range(SSM_HEADS):
        expand[SDT_LANE + h, SSM_HD * h:SSM_HD * (h + 1)] = 1.0
    rep = np.zeros((LANES, MIX_W), np.float32)
    for h in range(SSM_HEADS):
        g = h // (SSM_HEADS // SSM_GROUPS)
        for n in range(SSM_STATE):
            rep[SSM_STATE * g + n, SSM_HD * h + n] = 1.0
    ltri = np.tril(np.ones((r, r), np.float32))
    return (jnp.asarray(expand, BF16), jnp.asarray(rep, BF16), jnp.asarray(ltri, BF16),
            jnp.asarray(ltri.T, BF16))


def _ssd_kernel(z_ref, x_ref, xp_ref, b_ref, bp_ref, c_ref, cp_ref, small_ref, cw_ref, cb_ref,
                dtb_ref, alog_ref, dtbc_ref, alogc_ref, dskip_ref, onorm_ref, expand_ref,
                rep_ref, ltri_ref, utri_ref, o_ref, sw_ref, ext_ref, y_ref, *, r):
    first = pl.program_id(1) == 0

    @pl.when(first)
    def _():
        sw_ref[...] = jnp.zeros_like(sw_ref)

    keep_prev = jnp.where(first, 0.0, 1.0)
    ext_ref[0:SSD_HALO, 0:MIX_W] = xp_ref[0].astype(F32) * keep_prev
    ext_ref[0:SSD_HALO, MIX_W:MIX_W + LANES] = bp_ref[0].astype(F32) * keep_prev
    ext_ref[0:SSD_HALO, MIX_W + LANES:SSD_CONV_W] = cp_ref[0].astype(F32) * keep_prev
    ext_ref[SSD_HALO:, 0:MIX_W] = x_ref[0].astype(F32)
    ext_ref[SSD_HALO:, MIX_W:MIX_W + LANES] = b_ref[0].astype(F32)
    ext_ref[SSD_HALO:, MIX_W + LANES:SSD_CONV_W] = c_ref[0].astype(F32)
    conv = cb_ref[...] + cw_ref[SSM_CONV - 1:SSM_CONV, :] * ext_ref[SSD_HALO:, :]
    for back in range(1, SSM_CONV):
        tap = SSM_CONV - 1 - back
        conv = conv + cw_ref[tap:tap + 1, :] * ext_ref[pl.ds(SSD_HALO - back, r), :]
    xbc = _silu(conv)
    xs = xbc[:, 0:MIX_W]
    bm = xbc[:, MIX_W:MIX_W + LANES].astype(BF16)
    cm = xbc[:, MIX_W + LANES:SSD_CONV_W].astype(BF16)

    sm = small_ref[0]
    dt = _dot3_right(_softplus(sm + dtb_ref[...]), expand_ref[...])
    a_neg = -jnp.exp(alog_ref[...])
    acs = _dot3_left(ltri_ref[...], dt * a_neg)
    acs_last = acs[r - 1:r, :]
    sm_t = sm.T
    dt_t = _softplus(sm_t[SDT_LANE:SDT_LANE + SSM_HEADS, :] + dtbc_ref[:, 0:1])
    acs_t = _dot3_right(dt_t * (-jnp.exp(alogc_ref[:, 0:1])), utri_ref[...])

    xdt = (xs * dt).astype(BF16)
    row = lax.broadcasted_iota(jnp.int32, (r, r), 0)
    col = lax.broadcasted_iota(jnp.int32, (r, r), 1)
    causal = col <= row
    glane = lax.broadcasted_iota(jnp.int32, (1, LANES), 1)
    first_half = glane < SSM_STATE
    czero = jnp.zeros_like(cm)
    hpg = SSM_HEADS // SSM_GROUPS
    for g in range(SSM_GROUPS):
        cg = jnp.where(first_half if g == 0 else ~first_half, cm, czero)
        cb_g = _dot_nt(cg, bm)
        for pair in range(hpg // 2):
            p = g * (hpg // 2) + pair
            xp = xdt[:, LANES * p:LANES * (p + 1)]
            xzero = jnp.zeros_like(xp)
            acc = None
            for a in range(2):
                h = 2 * p + a
                dmat = acs[:, SSM_HD * h:SSM_HD * h + 1] - acs_t[h:h + 1, :]
                sc = (cb_g * jnp.exp(jnp.where(causal, dmat, NEG_BIG))).astype(BF16)
                xh = jnp.where(first_half if a == 0 else ~first_half, xp, xzero)
                contrib = _dot(sc, xh)
                acc = contrib if acc is None else acc + contrib
            y_ref[:, LANES * p:LANES * (p + 1)] = acc

    rep = rep_ref[...]
    sw = sw_ref[...]
    cw = (_dot(cm, rep) * jnp.exp(acs)).astype(BF16)
    y = y_ref[...] + _dot(cw, sw.astype(BF16)) + dskip_ref[...] * xs
    bw = (_dot(bm, rep) * jnp.exp(acs_last - acs)).astype(BF16)
    upd = _dot_tn(bw, xdt)
    srow = lax.broadcasted_iota(jnp.int32, (MIX_W, MIX_W), 0)
    scol = lax.broadcasted_iota(jnp.int32, (MIX_W, MIX_W), 1)
    same_head = _shr(srow, SSM_STATE) == _shr(scol, SSM_HD)
    sw_ref[...] = sw * jnp.exp(acs_last) + jnp.where(same_head, upd, 0.0)

    y = y * _silu(z_ref[0].astype(F32))
    gw = MIX_W // SSM_GROUPS
    for g in range(SSM_GROUPS):
        cols = slice(gw * g, gw * (g + 1))
        yg = y[:, cols]
        ms = jnp.mean(yg * yg, axis=-1, keepdims=True)
        o_ref[0, :, cols] = (yg * lax.rsqrt(ms + EPS) * onorm_ref[:, cols]).astype(BF16)


def _ssd(big3, small3, conv_w, conv_b, dtb_w, alog_w, dtb_c, alog_c, dskip_w, onorm, *, r=256):
    b, s, _ = big3.shape
    r = min(r, s)
    consts = _ssd_consts(r)
    hb = r // SSD_HALO

    def cur(width, off):
        return pl.BlockSpec((1, r, width), lambda bi, i: (bi, i, off // width))

    def prev(width, off):
        return pl.BlockSpec((1, SSD_HALO, width),
                            lambda bi, i: (bi, jnp.maximum(i * hb - 1, 0), off // width))

    def whole(shape):
        return pl.BlockSpec(shape, lambda bi, i: (0,) * len(shape))

    return pl.pallas_call(
        functools.partial(_ssd_kernel, r=r),
        grid=(b, s // r),
        in_specs=[
            cur(MIX_W, SZ_OFF),
            cur(MIX_W, SX_OFF), prev(MIX_W, SX_OFF),
            cur(LANES, SB_OFF), prev(LANES, SB_OFF),
            cur(LANES, SC_OFF), prev(LANES, SC_OFF),
            pl.BlockSpec((1, r, LANES), lambda bi, i: (bi, i, 0)),
            whole((SSM_CONV, SSD_CONV_W)), whole((1, SSD_CONV_W)),
            whole((1, LANES)), whole((1, MIX_W)),
            whole((SSM_HEADS, LANES)), whole((SSM_HEADS, LANES)),
            whole((1, MIX_W)), whole((1, MIX_W)),
            whole((LANES, MIX_W)), whole((LANES, MIX_W)), whole((r, r)), whole((r, r)),
        ],
        out_specs=pl.BlockSpec((1, r, MIX_W), lambda bi, i: (bi, i, 0)),
        out_shape=jax.ShapeDtypeStruct((b, s, MIX_W), BF16),
        scratch_shapes=[
            pltpu.VMEM((MIX_W, MIX_W), F32),
            pltpu.VMEM((r + SSD_HALO, SSD_CONV_W), F32),
            pltpu.VMEM((r, MIX_W), F32),
        ],
        compiler_params=_params(("arbitrary", "arbitrary")),
        name="ssd",
    )(big3, big3, big3, big3, big3, big3, big3, small3, conv_w, conv_b, dtb_w, alog_w,
      dtb_c, alog_c, dskip_w, onorm, *consts)


def _merge_kernel(oa_ref, ob_ref, oc_ref, gate_ref, x_ref, wb_ref, bgate_ref, wo_ref, nffn_ref,
                  wrh_ref, wrl_ref, br_ref, lstrict_ref, xe_ref, route_ref, counts_ref, *, tm):
    @pl.when(pl.program_id(0) == 0)
    def _():
        counts_ref[...] = jnp.zeros_like(counts_ref)

    mixed = None
    for ridx, o_ref in enumerate((oa_ref, ob_ref, oc_ref)):
        cols = slice(D_MODEL * ridx, D_MODEL * (ridx + 1))
        gate = _sigmoid(gate_ref[:, cols].astype(F32) + bgate_ref[ridx:ridx + 1, :])
        term = gate * _dot(o_ref[...], wb_ref[ridx])
        mixed = term if mixed is None else mixed + term
    xn = x_ref[...] + _dot(mixed.astype(BF16), wo_ref[...])
    xe_ref[:, 0:D_MODEL] = xn
    ms = jnp.mean(xn * xn, axis=-1, keepdims=True)
    h = xn * lax.rsqrt(ms + EPS) * nffn_ref[...]

    logits = _dot_f32w(h, wrh_ref[...], wrl_ref[...]) + br_ref[...]
    lane = lax.broadcasted_iota(jnp.int32, (tm, LANES), 1)
    lane_f = lane.astype(F32)
    big_lane = float(LANES)
    is_grp = lane < RE_LANE
    gl = jnp.where(is_grp, logits, NEG_BIG)
    gmax = jnp.max(gl, axis=-1, keepdims=True)
    g_w = 1.0 / jnp.sum(jnp.exp(gl - gmax), axis=-1, keepdims=True)
    g_sel = jnp.min(jnp.where(gl == gmax, lane_f, big_lane), axis=-1, keepdims=True)
    grp_of_lane = _shr(jnp.maximum(lane - RE_LANE, 0), EXP_PER_GROUP).astype(F32)
    in_grp = (lane >= RE_LANE) & (lane < RE_LANE + N_EXPERTS) & (grp_of_lane == g_sel)
    el = jnp.where(in_grp, logits, NEG_BIG)
    e1 = jnp.max(el, axis=-1, keepdims=True)
    i1 = jnp.min(jnp.where(in_grp & (el == e1), lane_f, big_lane), axis=-1, keepdims=True)
    rest = in_grp & (lane_f != i1)
    el2 = jnp.where(rest, logits, NEG_BIG)
    e2 = jnp.max(el2, axis=-1, keepdims=True)
    i2 = jnp.min(jnp.where(rest & (el2 == e2), lane_f, big_lane), axis=-1, keepdims=True)
    ratio = jnp.exp(e2 - e1)
    w1 = g_w / (1.0 + ratio)
    w2 = w1 * ratio
    xe_ref[:, D_MODEL:] = jnp.where(lane_f == i1, w1, 0.0) + jnp.where(lane_f == i2, w2, 0.0)

    first_lane = RE_LANE + EXP_PER_GROUP * g_sel
    lo = jnp.minimum(i1, i2) - first_lane
    hi = jnp.maximum(i1, i2) - first_lane
    cls = PAIRS_PER_GROUP * g_sel + lo * (7.0 - lo) * 0.5 + (hi - lo - 1.0)
    is_cls = lane_f == cls
    onehot = jnp.where(is_cls, 1.0, 0.0).astype(BF16)
    before = _dot(lstrict_ref[...], onehot) + counts_ref[0:1, :]
    rank = jnp.sum(jnp.where(is_cls, before, 0.0), axis=-1, keepdims=True)
    rank_hi = jnp.floor(rank * (1.0 / LANES))
    rank_lo = rank - rank_hi * LANES
    cols = jnp.where(lane == 0, cls, jnp.where(lane == 1, rank_hi, jnp.where(lane == 2, rank_lo, 0.0)))
    pick = jnp.where(lax.broadcasted_iota(jnp.int32, (8, LANES), 0) == lane[0:8, :], 1.0, 0.0)
    route_ref[...] = _dot_nt(pick.astype(BF16), cols.astype(BF16))
    total = _dot(jnp.ones((8, tm), BF16), onehot)
    counts_ref[...] = counts_ref[...] + total


def _merge(oa, ob, oc, big, x2, wb, bgate, wo, nffn, wr_hi, wr_lo, br, *, tm=512):
    t = x2.shape[0]
    tm = min(tm, t)
    lstrict = jnp.asarray(np.tril(np.ones((tm, tm), np.float32), -1), BF16)

    def whole(shape):
        return pl.BlockSpec(shape, lambda i: (0,) * len(shape))

    return pl.pallas_call(
        functools.partial(_merge_kernel, tm=tm),
        grid=(t // tm,),
        in_specs=[
            pl.BlockSpec((tm, MIX_W), lambda i: (i, 0)),
            pl.BlockSpec((tm, MIX_W), lambda i: (i, 0)),
            pl.BlockSpec((tm, MIX_W), lambda i: (i, 0)),
            pl.BlockSpec((tm, 3 * D_MODEL), lambda i: (i, 0)),
            pl.BlockSpec((tm, D_MODEL), lambda i: (i, 0)),
            whole((3, MIX_W, D_MODEL)), whole((3, D_MODEL)), whole((D_MODEL, D_MODEL)),
            whole((1, D_MODEL)), whole((D_MODEL, LANES)), whole((D_MODEL, LANES)),
            whole((1, LANES)), whole((tm, tm)),
        ],
        out_specs=[
            pl.BlockSpec((tm, XE_COLS), lambda i: (i, 0)),
            pl.BlockSpec((8, tm), lambda i: (0, i)),
            pl.BlockSpec((8, LANES), lambda i: (0, 0)),
        ],
        out_shape=[
            jax.ShapeDtypeStruct((t, XE_COLS), F32),
            jax.ShapeDtypeStruct((8, t), F32),
            jax.ShapeDtypeStruct((8, LANES), F32),
        ],
        compiler_params=_params(("arbitrary",)),
        name="merge",
    )(oa, ob, oc, big, x2, wb, bgate, wo, nffn, wr_hi, wr_lo, br, lstrict)


MOE_TILE = 256
ROW_DMA_TILE = 512


def _row_copy_kernel(pos_ref, src_ref, *rest, tm, scatter):
    dst_ref, sem = rest[-2], rest[-1]

    def copy(r):
        near, far = pl.ds(r, 1), pl.ds(pos_ref[0, 0, r], 1)
        if scatter:
            return pltpu.make_async_copy(src_ref.at[near], dst_ref.at[far], sem)
        return pltpu.make_async_copy(src_ref.at[far], dst_ref.at[near], sem)

    def drain(r, carry):
        copy(r).wait()
        return carry

    for r in range(tm):
        copy(r).start(priority=r % 2)
    lax.fori_loop(0, tm, drain, 0, unroll=8)


def _row_copy(pos, src, dst_init, out_rows, *, name):
    t = pos.shape[0]
    tm = min(ROW_DMA_TILE, t)
    width = src.shape[1]
    any_spec = pl.BlockSpec(memory_space=pl.ANY)
    tile_spec = pl.BlockSpec((tm, width), lambda i: (i, 0))
    pos_spec = pl.BlockSpec((1, 1, tm), lambda i: (i, 0, 0), memory_space=pltpu.SMEM)
    scatter = dst_init is not None
    operands = [pos.reshape(t // tm, 1, tm), src] + ([dst_init] if scatter else [])
    in_specs = [pos_spec] + ([tile_spec, any_spec] if scatter else [any_spec])
    return pl.pallas_call(
        functools.partial(_row_copy_kernel, tm=tm, scatter=scatter),
        grid=(t // tm,),
        in_specs=in_specs,
        out_specs=any_spec if scatter else tile_spec,
        out_shape=jax.ShapeDtypeStruct((out_rows, width), src.dtype),
        scratch_shapes=[pltpu.SemaphoreType.DMA(())],
        input_output_aliases={2: 0} if scatter else {},
        compiler_params=_params(("arbitrary",)),
        name=name,
    )(*operands)


def _moe_sorted_kernel(ea_ref, eb_ref, nused_ref, xs_ref, nffn_ref, wga_ref, wua_ref, wda_ref,
                       wgb_ref, wub_ref, wdb_ref, o_ref, *, tm):
    i = pl.program_id(0)

    @pl.when(i >= nused_ref[0])
    def _():
        o_ref[...] = jnp.zeros_like(o_ref)


    @pl.when(i < nused_ref[0])
    def _():
        xn = xs_ref[:, 0:D_MODEL]
        comb = xs_ref[:, D_MODEL:]
        ms = jnp.mean(xn * xn, axis=-1, keepdims=True)
        h = (xn * lax.rsqrt(ms + EPS) * nffn_ref[...]).astype(BF16)
        lane = lax.broadcasted_iota(jnp.int32, (tm, LANES), 1)
        out = xn
        for e_ref, wg_ref, wu_ref, wd_ref in ((ea_ref, wga_ref, wua_ref, wda_ref),
                                              (eb_ref, wgb_ref, wub_ref, wdb_ref)):
            w = jnp.sum(jnp.where(lane == e_ref[i] + RE_LANE, comb, 0.0), axis=-1, keepdims=True)
            hid = _silu(_dot(h, wg_ref[0])) * _dot(h, wu_ref[0])
            out = out + w * _dot(hid.astype(BF16), wd_ref[0])
        o_ref[...] = out


def _moe_sorted(ea, eb, nused, xs, nffn, wg, wu, wd, *, tm):
    n_tiles = xs.shape[0] // tm

    def w_in(which):
        return pl.BlockSpec((1, D_MODEL, D_EXPERT), lambda i, ea, eb, nu: ((ea, eb)[which][i], 0, 0))

    def w_out(which):
        return pl.BlockSpec((1, D_EXPERT, D_MODEL), lambda i, ea, eb, nu: ((ea, eb)[which][i], 0, 0))

    grid_spec = pltpu.PrefetchScalarGridSpec(
        num_scalar_prefetch=3,
        grid=(n_tiles,),
        in_specs=[
            pl.BlockSpec((tm, XE_COLS), lambda i, ea, eb, nu: (jnp.minimum(i, nu[0] - 1), 0)),
            pl.BlockSpec((1, D_MODEL), lambda i, ea, eb, nu: (0, 0)),
            w_in(0), w_in(0), w_out(0), w_in(1), w_in(1), w_out(1),
        ],
        out_specs=pl.BlockSpec((tm, D_MODEL), lambda i, ea, eb, nu: (i, 0)),
    )
    return pl.pallas_call(
        functools.partial(_moe_sorted_kernel, tm=tm),
        grid_spec=grid_spec,
        out_shape=jax.ShapeDtypeStruct((n_tiles * tm, D_MODEL), F32),
        compiler_params=_params(("arbitrary",)),
        name="moe",
    )(ea, eb, nused, xs, nffn, wg, wu, wd, wg, wu, wd)


_PAIR_LO = np.array([0, 0, 0, 1, 1, 2], np.int32)
_PAIR_HI = np.array([1, 2, 3, 2, 3, 3], np.int32)


def _moe(xe, route, counts, nffn, wg, wu, wd):
    t = xe.shape[0]
    tm = min(MOE_TILE, t)
    n_cls = N_EGROUPS * PAIRS_PER_GROUP
    n_tiles = t // tm + n_cls
    cnt = counts[0, :n_cls].astype(jnp.int32)
    tiles = (cnt + tm - 1) // tm
    tile_end = jnp.cumsum(tiles)
    n_used = tile_end[-1]
    cls_base = (tile_end - tiles) * tm
    tile_idx = jnp.minimum(jnp.arange(n_tiles), n_used - 1)
    tile_cls = jnp.sum((tile_end[None, :] <= tile_idx[:, None]).astype(jnp.int32), axis=1)
    grp, pair = tile_cls // PAIRS_PER_GROUP, tile_cls % PAIRS_PER_GROUP
    ea = EXP_PER_GROUP * grp + jnp.asarray(_PAIR_LO)[pair]
    eb = EXP_PER_GROUP * grp + jnp.asarray(_PAIR_HI)[pair]
    rank = (route[1] * LANES + route[2]).astype(jnp.int32)
    pos = cls_base[route[0].astype(jnp.int32)] + rank

    xs = _row_copy(pos, xe, jnp.zeros((n_tiles * tm, XE_COLS), F32), n_tiles * tm,
                   name="moe_scatter")
    ys = _moe_sorted(ea, eb, n_used.reshape(1), xs, nffn, wg, wu, wd, tm=tm)
    return _row_copy(pos, ys, None, t, name="moe_gather")


def _cols(w, off, width):
    return w[:, off:off + width]


def _pad_lanes(v, lane0, width=LANES):
    out = jnp.zeros((1, width), F32)
    return out.at[0, lane0:lane0 + v.shape[0]].set(v.astype(F32))


def _layer(x2, b, s, norm_mix, w_in, fox_f_bias, fox_q_norm, fox_k_norm, gla_w_lr, gla_b_gate,
           gla_out_norm, ssm_conv_w, ssm_conv_b, ssm_dt_bias, ssm_a_log, ssm_d, ssm_out_norm,
           w_branch, b_branch_gate, w_out, norm_ffn, w_router_grp, b_router_grp,
           w_router_exp, b_router_exp, w_exp_gate, w_exp_up, w_exp_down):
    t = b * s
    w_big = jnp.concatenate([
        _cols(w_in, _O_GATE, 3 * D_MODEL), _cols(w_in, _O_FQ, MIX_W), _cols(w_in, _O_FK, MIX_W),
        _cols(w_in, _O_FV, MIX_W), _cols(w_in, _O_GV, MIX_W), _cols(w_in, _O_GR, MIX_W),
        _cols(w_in, _O_SZ, MIX_W), _cols(w_in, _O_SX, MIX_W), _cols(w_in, _O_GQ, 256),
        _cols(w_in, _O_GK, 256), _cols(w_in, _O_SB, LANES), _cols(w_in, _O_SC, LANES),
    ], axis=1).astype(BF16)
    w_small = jnp.concatenate([
        _cols(w_in, _O_FF, FOX_HEADS), _cols(w_in, _O_GLR, GLA_RANK), _cols(w_in, _O_SDT, SSM_HEADS),
        jnp.zeros((D_MODEL, LANES - FOX_HEADS - GLA_RANK - SSM_HEADS), F32),
    ], axis=1)
    ws_hi, ws_lo = _split2(w_small)

    big, small = _inproj(x2, norm_mix.reshape(1, D_MODEL), w_big, ws_hi, ws_lo)
    big3 = big.reshape(b, s, BIG_COLS)
    small3 = small.reshape(b, s, LANES)

    qt, kf, vt = _fox_prep(big3, small3, _pad_lanes(fox_f_bias, FF_LANE),
                           jnp.tile(fox_q_norm, FOX_HEADS).reshape(1, MIX_W),
                           jnp.tile(fox_k_norm, FOX_HEADS).reshape(1, MIX_W), ts=min(FOX_TILE, s))
    logit_bound = ((FOX_HD ** 0.5) * 1.01 * jnp.max(jnp.abs(fox_q_norm))
                   * jnp.max(jnp.abs(fox_k_norm)))
    o_a = _fox_attn(qt, kf, vt, logit_bound)

    wl = jnp.zeros((LANES, GLA_HEADS * GLA_DK), F32).at[GLR_LANE:GLR_LANE + GLA_RANK].set(gla_w_lr)
    wl_hi, wl_lo = _split2(wl)
    o_b = _gla(big3, small3, wl_hi, wl_lo, gla_b_gate.reshape(1, -1),
               gla_out_norm.reshape(1, GLA_DV))

    o_c = _ssd(big3, small3, ssm_conv_w, ssm_conv_b.reshape(1, -1),
               _pad_lanes(ssm_dt_bias, SDT_LANE),
               jnp.repeat(ssm_a_log, SSM_HD).reshape(1, MIX_W),
               jnp.broadcast_to(ssm_dt_bias[:, None], (SSM_HEADS, LANES)),
               jnp.broadcast_to(ssm_a_log[:, None], (SSM_HEADS, LANES)),
               jnp.repeat(ssm_d, SSM_HD).reshape(1, MIX_W),
               ssm_out_norm.reshape(1, MIX_W))

    w_r = jnp.concatenate([w_router_grp, w_router_exp,
                           jnp.zeros((D_MODEL, LANES - N_EGROUPS - N_EXPERTS), F32)], axis=1)
    wr_hi, wr_lo = _split2(w_r)
    b_r = jnp.concatenate([b_router_grp, b_router_exp,
                           jnp.zeros((LANES - N_EGROUPS - N_EXPERTS,), F32)]).reshape(1, LANES)
    nffn = norm_ffn.reshape(1, D_MODEL)
    xe, route, counts = _merge(o_a.reshape(t, MIX_W), o_b.reshape(t, MIX_W), o_c.reshape(t, MIX_W),
                               big, x2, w_branch.astype(BF16), b_branch_gate, w_out.astype(BF16),
                               nffn, wr_hi, wr_lo, b_r)

    return _moe(xe, route, counts, nffn, w_exp_gate.astype(BF16), w_exp_up.astype(BF16),
                w_exp_down.astype(BF16))


def kernel(x, norm_mix, w_in, fox_f_bias, fox_q_norm, fox_k_norm, gla_w_lr, gla_b_gate, gla_out_norm, ssm_conv_w, ssm_conv_b, ssm_dt_bias, ssm_a_log, ssm_d, ssm_out_norm, w_branch, b_branch_gate, w_out, norm_ffn, w_router_grp, b_router_grp, w_router_exp, b_router_exp, w_exp_gate, w_exp_up, w_exp_down):
    b, s, d = x.shape
    x2 = x.reshape(b * s, d)
    per_layer = (norm_mix, w_in, fox_f_bias, fox_q_norm, fox_k_norm, gla_w_lr, gla_b_gate,
                 gla_out_norm, ssm_conv_w, ssm_conv_b, ssm_dt_bias, ssm_a_log, ssm_d,
                 ssm_out_norm, w_branch, b_branch_gate, w_out, norm_ffn, w_router_grp,
                 b_router_grp, w_router_exp, b_router_exp, w_exp_gate, w_exp_up, w_exp_down)
    for l in range(norm_mix.shape[0]):
        x2 = _layer(x2, b, s, *[p[l] for p in per_layer])
    return x2.reshape(b, s, d)
```

```python
import functools

import numpy as np
import jax
import jax.numpy as jnp
from jax import lax
from jax.experimental import pallas as pl
from jax.experimental.pallas import tpu as pltpu

F32 = jnp.float32
BF16 = jnp.bfloat16

D_MODEL = 1024
MIX_W = 512
EPS = 1e-6
FOX_HEADS = 8
FOX_HD = 64
FOX_PAIRS = FOX_HEADS // 2
GLA_HEADS = 4
GLA_DK = 64
GLA_DV = 128
GLA_RANK = 16
GLA_GATE_NORM = 16.0
GLA_CHUNK = 64
SSM_HEADS = 8
SSM_HD = 64
SSM_GROUPS = 2
SSM_STATE = 64
SSM_CONV = 4
N_EGROUPS = 4
EXP_PER_GROUP = 4
N_EXPERTS = 16
PAIRS_PER_GROUP = 6
D_EXPERT = 512

LANES = 128
NEG_BIG = -1e30
LOG2E = 1.4426950408889634
VMEM_LIMIT = 56 * 1024 * 1024

GATE_OFF, FQ_OFF, FK_OFF, FV_OFF = 0, 3072, 3584, 4096
GV_OFF, GR_OFF, SZ_OFF, SX_OFF = 4608, 5120, 5632, 6144
GQ_OFF, GK_OFF, SB_OFF, SC_OFF = 6656, 6912, 7168, 7296
BIG_COLS = 7424
FF_LANE, GLR_LANE, SDT_LANE = 0, 8, 24
_O_FQ, _O_FK, _O_FV, _O_FF = 0, 512, 1024, 1536
_O_GQ, _O_GK, _O_GV, _O_GR, _O_GLR = 1544, 1800, 2056, 2568, 3080
_O_SZ, _O_SX, _O_SB, _O_SC, _O_SDT, _O_GATE = 3096, 3608, 4120, 4248, 4376, 4384
RG_LANE, RE_LANE = 0, 8
ROUTE_ROWS = 32
XE_COLS = D_MODEL + LANES


def _split2(x):
    hi = x.astype(BF16)
    lo = (x - hi.astype(F32)).astype(BF16)
    return hi, lo


def _split3(x):
    x1 = x.astype(BF16)
    r = x - x1.astype(F32)
    x2 = r.astype(BF16)
    x3 = (r - x2.astype(F32)).astype(BF16)
    return x1, x2, x3


def _dot(a, b):
    return jnp.dot(a, b, preferred_element_type=F32)


def _dot_nt(a, b):
    return lax.dot_general(a, b, (((1,), (1,)), ((), ())), preferred_element_type=F32)


def _dot_tn(a, b):
    return lax.dot_general(a, b, (((0,), (0,)), ((), ())), preferred_element_type=F32)


def _dot3_left(m_bf16, x_f32):
    x1, x2, x3 = _split3(x_f32)
    return _dot(m_bf16, x1) + _dot(m_bf16, x2) + _dot(m_bf16, x3)


def _dot3_right(x_f32, m_bf16):
    x1, x2, x3 = _split3(x_f32)
    return _dot(x1, m_bf16) + _dot(x2, m_bf16) + _dot(x3, m_bf16)


def _dot_f32w(x_f32, w_hi, w_lo):
    x_hi, x_lo = _split2(x_f32)
    return _dot(x_hi, w_hi) + _dot(x_lo, w_hi) + _dot(x_hi, w_lo)


def _shr(x, pow2):
    return jnp.right_shift(x, pow2.bit_length() - 1)


def _log_sigmoid(x):
    return jnp.minimum(x, 0.0) - jnp.log(1.0 + jnp.exp(-jnp.abs(x)))


def _softplus(x):
    return jnp.maximum(x, 0.0) + jnp.log(1.0 + jnp.exp(-jnp.abs(x)))


def _sigmoid(x):
    return 0.5 * jnp.tanh(0.5 * x) + 0.5


def _silu(x):
    return x * _sigmoid(x)


def _params(sem):
    return pltpu.CompilerParams(dimension_semantics=sem, vmem_limit_bytes=VMEM_LIMIT)


def _inproj_kernel(x_ref, g_ref, w_ref, wsh_ref, wsl_ref, big_ref, small_ref, *, tn):
    x = x_ref[...]
    ms = jnp.mean(x * x, axis=-1, keepdims=True)
    h = x * lax.rsqrt(ms + EPS) * g_ref[...]
    hb = h.astype(BF16)
    for c in range(BIG_COLS // tn):
        cols = slice(c * tn, (c + 1) * tn)
        big_ref[:, cols] = _dot(hb, w_ref[:, cols]).astype(BF16)
    h_lo = (h - hb.astype(F32)).astype(BF16)
    wsh = wsh_ref[...]
    small_ref[...] = _dot(hb, wsh) + _dot(h_lo, wsh) + _dot(hb, wsl_ref[...])


def _inproj(x2, gain, w_big, ws_hi, ws_lo, *, tm=512, tn=256):
    t = x2.shape[0]
    return pl.pallas_call(
        functools.partial(_inproj_kernel, tn=tn),
        grid=(t // tm,),
        in_specs=[
            pl.BlockSpec((tm, D_MODEL), lambda i: (i, 0)),
            pl.BlockSpec((1, D_MODEL), lambda i: (0, 0)),
            pl.BlockSpec((D_MODEL, BIG_COLS), lambda i: (0, 0), pipeline_mode=pl.Buffered(1)),
            pl.BlockSpec((D_MODEL, LANES), lambda i: (0, 0)),
            pl.BlockSpec((D_MODEL, LANES), lambda i: (0, 0)),
        ],
        out_specs=[
            pl.BlockSpec((tm, BIG_COLS), lambda i: (i, 0)),
            pl.BlockSpec((tm, LANES), lambda i: (i, 0)),
        ],
        out_shape=[
            jax.ShapeDtypeStruct((t, BIG_COLS), BF16),
            jax.ShapeDtypeStruct((t, LANES), F32),
        ],
        compiler_params=_params(("arbitrary",)),
        name="inproj",
    )(x2, gain, w_big, ws_hi, ws_lo)


def _fox_consts(ts):
    ltri = np.tril(np.ones((ts, ts), np.float32))
    hsum = np.kron(np.eye(FOX_HEADS, dtype=np.float32), np.ones((FOX_HD, FOX_HD), np.float32))
    sq = np.zeros((3, LANES, MIX_W), np.float32)
    sk = np.zeros((3, LANES, MIX_W), np.float32)
    oneq = np.zeros((1, MIX_W), np.float32)
    onek = np.zeros((1, MIX_W), np.float32)
    for h in range(FOX_HEADS):
        base = LANES * (h // 2) + 6 * (h % 2)
        for j in range(3):
            sq[j, FF_LANE + h, base + j] = 1.0
            sk[j, FF_LANE + h, base + 3 + j] = -1.0
            oneq[0, base + 3 + j] = 1.0
            onek[0, base + j] = 1.0
    saug = np.concatenate([sq.reshape(3 * LANES, MIX_W), sk.reshape(3 * LANES, MIX_W)], axis=1)
    return (jnp.asarray(ltri, BF16), jnp.asarray(hsum, BF16), jnp.asarray(saug, BF16),
            jnp.asarray(oneq), jnp.asarray(onek))


def _fox_prep_kernel(fq_ref, fk_ref, fv_ref, small_ref, fbias_ref, qg_ref, kg_ref, ltri_ref,
                     hsum_ref, saug_ref, oneq_ref, onek_ref, qt_ref, kf_ref, vt_ref, carry_ref,
                     *, ts):
    @pl.when(pl.program_id(1) == 0)
    def _():
        carry_ref[...] = jnp.zeros_like(carry_ref)

    lane = lax.broadcasted_iota(jnp.int32, (ts, LANES), 1)
    f = small_ref[0] + fbias_ref[...]
    ls = jnp.where(lane < FOX_HEADS, _log_sigmoid(f) * LOG2E, 0.0)
    parts = _dot(ltri_ref[...], jnp.concatenate(_split3(ls), axis=1))
    c = parts[:, 0:LANES] + parts[:, LANES:2 * LANES] + parts[:, 2 * LANES:] + carry_ref[0:1, :]
    carry_ref[...] = jnp.broadcast_to(c[ts - 1:ts, :], carry_ref.shape)
    aug = _dot(jnp.concatenate(_split3(c), axis=1), saug_ref[...])
    qaug = aug[:, 0:MIX_W] + oneq_ref[...]
    kaug = aug[:, MIX_W:] + onek_ref[...]

    hsum = hsum_ref[...]

    def head_norm(xb, gain):
        x = xb.astype(F32)
        s_hi, s_lo = _split2(x * x)
        ss = _dot(s_hi, hsum) + _dot(s_lo, hsum)
        return x * lax.rsqrt(ss * (1.0 / FOX_HD) + EPS) * gain

    qn = head_norm(fq_ref[0], qg_ref[...]) * (FOX_HD ** -0.5 * LOG2E)
    kn = head_norm(fk_ref[0], kg_ref[...])
    v = fv_ref[0].astype(F32)
    for p in range(FOX_PAIRS):
        src = slice(LANES * p, LANES * (p + 1))
        dst_x = slice(2 * LANES * p, 2 * LANES * p + LANES)
        dst_a = slice(2 * LANES * p + LANES, 2 * LANES * (p + 1))
        kf_ref[0, :, dst_x] = kn[:, src].astype(BF16)
        kf_ref[0, :, dst_a] = kaug[:, src].astype(BF16)
        qt_ref[0, p, 0, 0:LANES, :] = qn[:, src].T.astype(BF16)
        qt_ref[0, p, 0, LANES:2 * LANES, :] = qaug[:, src].T.astype(BF16)
        vt_ref[0, p, 0] = v[:, src].T.astype(BF16)


def _fox_prep(big3, small3, fbias, qgain, kgain, *, ts):
    b, s, _ = big3.shape
    consts = _fox_consts(ts)
    const_specs = [
        pl.BlockSpec((ts, ts), lambda bi, i: (0, 0)),
        pl.BlockSpec((MIX_W, MIX_W), lambda bi, i: (0, 0)),
        pl.BlockSpec((3 * LANES, 2 * MIX_W), lambda bi, i: (0, 0)),
        pl.BlockSpec((1, MIX_W), lambda bi, i: (0, 0)),
        pl.BlockSpec((1, MIX_W), lambda bi, i: (0, 0)),
    ]
    return pl.pallas_call(
        functools.partial(_fox_prep_kernel, ts=ts),
        grid=(b, s // ts),
        in_specs=[
            pl.BlockSpec((1, ts, MIX_W), lambda bi, i: (bi, i, FQ_OFF // MIX_W)),
            pl.BlockSpec((1, ts, MIX_W), lambda bi, i: (bi, i, FK_OFF // MIX_W)),
            pl.BlockSpec((1, ts, MIX_W), lambda bi, i: (bi, i, FV_OFF // MIX_W)),
            pl.BlockSpec((1, ts, LANES), lambda bi, i: (bi, i, 0)),
            pl.BlockSpec((1, LANES), lambda bi, i: (0, 0)),
            pl.BlockSpec((1, MIX_W), lambda bi, i: (0, 0)),
            pl.BlockSpec((1, MIX_W), lambda bi, i: (0, 0)),
        ] + const_specs,
        out_specs=[
            pl.BlockSpec((1, FOX_PAIRS, 1, 2 * LANES, ts), lambda bi, i: (bi, 0, i, 0, 0)),
            pl.BlockSpec((1, ts, 2 * MIX_W), lambda bi, i: (bi, i, 0)),
            pl.BlockSpec((1, FOX_PAIRS, 1, LANES, ts), lambda bi, i: (bi, 0, i, 0, 0)),
        ],
        out_shape=[
            jax.ShapeDtypeStruct((b, FOX_PAIRS, s // ts, 2 * LANES, ts), BF16),
            jax.ShapeDtypeStruct((b, s, 2 * MIX_W), BF16),
            jax.ShapeDtypeStruct((b, FOX_PAIRS, s // ts, LANES, ts), BF16),
        ],
        scratch_shapes=[pltpu.VMEM((8, LANES), F32)],
        compiler_params=_params(("arbitrary", "arbitrary")),
        name="fox_prep",
    )(big3, big3, big3, small3, fbias, qgain, kgain, *consts)


FOX_NOSHIFT_BOUND = 40.0
FOX_TILE = 512
FOX_UNROLL = 4


def _fox_attn_kernel(qt_ref, k_ref, vt_ref, o_ref, acc_ref, l_ref, m_ref, *, tq, online):
    i = pl.program_id(2)
    qt = qt_ref[0, 0, 0]
    qrow = lax.broadcasted_iota(jnp.int32, (2 * LANES, 1), 0)
    in_a = (qrow < FOX_HD) | ((qrow >= LANES) & (qrow < LANES + 6))
    in_b = ((qrow >= FOX_HD) & (qrow < LANES)) | ((qrow >= LANES + 6) & (qrow < LANES + 12))
    zero = jnp.zeros_like(qt)
    qt_heads = (jnp.where(in_a, qt, zero), jnp.where(in_b, qt, zero))

    acc_ref[...] = jnp.zeros_like(acc_ref)
    l_ref[...] = jnp.zeros_like(l_ref)
    if online:
        m_ref[...] = jnp.full_like(m_ref, NEG_BIG)

    def scores(j, a, diag):
        k = k_ref[0, pl.ds(pl.multiple_of(j * tq, tq), tq), :]
        st = _dot(k, qt_heads[a])
        if diag:
            krow = lax.broadcasted_iota(jnp.int32, (tq, tq), 0)
            qcol = lax.broadcasted_iota(jnp.int32, (tq, tq), 1)
            st = jnp.where(krow <= qcol, st, NEG_BIG)
        return st

    def accumulate(j, a, st):
        vt_a = vt_ref[0, 0, j, FOX_HD * a:FOX_HD * (a + 1), :]
        if online:
            m_prev = m_ref[a]
            m_new = jnp.maximum(m_prev, jnp.max(st, axis=0, keepdims=True))
            alpha = jnp.exp2(m_prev - m_new)
            m_ref[a] = m_new
            pt = jnp.exp2(st - m_new)
            l_ref[a] = alpha * l_ref[a] + jnp.sum(pt.reshape(tq // 8, 8, tq), axis=0)
            acc_ref[a] = alpha * acc_ref[a] + _dot(vt_a, pt.astype(BF16))
        else:
            pt = jnp.exp2(st)
            l_ref[a] += jnp.sum(pt.reshape(tq // 8, 8, tq), axis=0)
            acc_ref[a] += _dot(vt_a, pt.astype(BF16))

    def run(units):
        st = scores(*units[0])
        for u, unit in enumerate(units):
            st_next = scores(*units[u + 1]) if u + 1 < len(units) else None
            accumulate(unit[0], unit[1], st)
            st = st_next

    def units(first_block, n_regular, with_diag):
        blocks = [(first_block + d, False) for d in range(n_regular)]
        if with_diag:
            blocks.append((first_block + n_regular, True))
        return [(j, a, diag) for j, diag in blocks for a in range(2)]

    def body(jj, carry):
        run(units(FOX_UNROLL * jj, FOX_UNROLL, False))
        return carry

    lax.fori_loop(0, i // FOX_UNROLL, body, 0)
    for rem in range(FOX_UNROLL):
        @pl.when(i % FOX_UNROLL == rem)
        def _():
            run(units(i - rem, rem, True))

    halves = [acc_ref[a] * (1.0 / jnp.sum(l_ref[a], axis=0, keepdims=True)) for a in range(2)]
    o_ref[0] = jnp.concatenate(halves, axis=0).T.astype(BF16)


def _fox_attn(qt, kf, vt, logit_bound):
    b, _, nq, _, tq = qt.shape
    s = nq * tq

    def call(online, name):
        return pl.pallas_call(
            functools.partial(_fox_attn_kernel, tq=tq, online=online),
            grid=(b, FOX_PAIRS, nq),
            in_specs=[
                pl.BlockSpec((1, 1, 1, 2 * LANES, tq), lambda bi, p, i: (bi, p, i, 0, 0)),
                pl.BlockSpec((1, s, 2 * LANES), lambda bi, p, i: (bi, 0, p)),
                pl.BlockSpec((1, 1, nq, LANES, tq), lambda bi, p, i: (bi, p, 0, 0, 0)),
            ],
            out_specs=pl.BlockSpec((1, tq, LANES), lambda bi, p, i: (bi, i, p)),
            out_shape=jax.ShapeDtypeStruct((b, s, MIX_W), BF16),
            scratch_shapes=[
                pltpu.VMEM((2, FOX_HD, tq), F32),
                pltpu.VMEM((2, 8, tq), F32),
                pltpu.VMEM((2, 1, tq), F32),
            ],
            compiler_params=_params(("arbitrary", "arbitrary", "arbitrary")),
            name=name,
        )(qt, kf, vt)

    return lax.cond(logit_bound < FOX_NOSHIFT_BOUND,
                    lambda: call(False, "fox_attn"), lambda: call(True, "fox_attn_online"))


def _gla_consts(r):
    idx = np.arange(r)
    same = (idx[:, None] // GLA_CHUNK) == (idx[None, :] // GLA_CHUNK)
    lblk = (same & (idx[None, :] <= idx[:, None])).astype(np.float32)
    ablk = same.astype(np.float32)
    return jnp.asarray(lblk, BF16), jnp.asarray(ablk, BF16)


def _gla_kernel(q_ref, k_ref, v_ref, r_ref, small_ref, wlh_ref, wll_ref, bg_ref, lblk_ref,
                ablk_ref, gain_ref, o_ref, st_ref, oacc_ref, *, r):
    @pl.when(pl.program_id(1) == 0)
    def _():
        st_ref[...] = jnp.zeros_like(st_ref)

    kw = GLA_HEADS * GLA_DK
    gate = _dot_f32w(small_ref[0], wlh_ref[...], wll_ref[...]) + bg_ref[...]
    log_a = _log_sigmoid(gate) * (1.0 / GLA_GATE_NORM)
    a1, a2, a3 = _split3(log_a)
    lblk = lblk_ref[...]
    ablk = ablk_ref[...]
    bcum = _dot(lblk, a1) + _dot(lblk, a2) + _dot(lblk, a3)
    btot = _dot(ablk, a1) + _dot(ablk, a2) + _dot(ablk, a3)
    q = q_ref[0].astype(F32) * (GLA_DK ** -0.5)
    k = k_ref[0].astype(F32)
    q_dec = (q * jnp.exp(bcum)).astype(BF16)
    k_dec = (k * jnp.exp(-bcum)).astype(BF16)
    k_end = (k * jnp.exp(btot - bcum)).astype(BF16)
    d_tot = jnp.exp(btot)
    v = v_ref[0]

    row = lax.broadcasted_iota(jnp.int32, (r, r), 0)
    col = lax.broadcasted_iota(jnp.int32, (r, r), 1)
    keep = (_shr(row, GLA_CHUNK) == _shr(col, GLA_CHUNK)) & (col <= row)
    klane = lax.broadcasted_iota(jnp.int32, (1, kw), 1)
    qzero = jnp.zeros_like(q_dec)
    for h in range(GLA_HEADS):
        in_h = (klane >= GLA_DK * h) & (klane < GLA_DK * (h + 1))
        att = _dot_nt(jnp.where(in_h, q_dec, qzero), k_dec)
        att = jnp.where(keep, att, 0.0).astype(BF16)
        vcols = slice(GLA_DV * h, GLA_DV * (h + 1))
        oacc_ref[:, vcols] = _dot(att, v[:, vcols])

    srow = lax.broadcasted_iota(jnp.int32, (MIX_W, kw), 0)
    scol = lax.broadcasted_iota(jnp.int32, (MIX_W, kw), 1)
    same_head = _shr(srow, GLA_DV) == _shr(scol, GLA_DK)
    for c in range(r // GLA_CHUNK):
        rows = slice(GLA_CHUNK * c, GLA_CHUNK * (c + 1))
        st = st_ref[...]
        oacc_ref[rows, :] += _dot_nt(q_dec[rows], st.astype(BF16))
        kv_t = _dot_tn(v[rows], k_end[rows])
        st_ref[...] = d_tot[GLA_CHUNK * c:GLA_CHUNK * c + 1, :] * st + jnp.where(same_head, kv_t, 0.0)

    gain = gain_ref[...]
    gr = r_ref[0].astype(F32)
    for h in range(GLA_HEADS):
        vcols = slice(GLA_DV * h, GLA_DV * (h + 1))
        o = oacc_ref[:, vcols]
        ms = jnp.mean(o * o, axis=-1, keepdims=True)
        o_ref[0, :, vcols] = (o * lax.rsqrt(ms + EPS) * gain * _silu(gr[:, vcols])).astype(BF16)


def _gla(big3, small3, wl_hi, wl_lo, bgate, gain, *, r=256):
    b, s, _ = big3.shape
    r = min(r, s)
    kw = GLA_HEADS * GLA_DK
    lblk, ablk = _gla_consts(r)
    return pl.pallas_call(
        functools.partial(_gla_kernel, r=r),
        grid=(b, s // r),
        in_specs=[
            pl.BlockSpec((1, r, kw), lambda bi, i: (bi, i, GQ_OFF // kw)),
            pl.BlockSpec((1, r, kw), lambda bi, i: (bi, i, GK_OFF // kw)),
            pl.BlockSpec((1, r, MIX_W), lambda bi, i: (bi, i, GV_OFF // MIX_W)),
            pl.BlockSpec((1, r, MIX_W), lambda bi, i: (bi, i, GR_OFF // MIX_W)),
            pl.BlockSpec((1, r, LANES), lambda bi, i: (bi, i, 0)),
            pl.BlockSpec((LANES, kw), lambda bi, i: (0, 0)),
            pl.BlockSpec((LANES, kw), lambda bi, i: (0, 0)),
            pl.BlockSpec((1, kw), lambda bi, i: (0, 0)),
            pl.BlockSpec((r, r), lambda bi, i: (0, 0)),
            pl.BlockSpec((r, r), lambda bi, i: (0, 0)),
            pl.BlockSpec((1, GLA_DV), lambda bi, i: (0, 0)),
        ],
        out_specs=pl.BlockSpec((1, r, MIX_W), lambda bi, i: (bi, i, 0)),
        out_shape=jax.ShapeDtypeStruct((b, s, MIX_W), BF16),
        scratch_shapes=[pltpu.VMEM((MIX_W, kw), F32), pltpu.VMEM((r, MIX_W), F32)],
        compiler_params=_params(("arbitrary", "arbitrary")),
        name="gla",
    )(big3, big3, big3, big3, small3, wl_hi, wl_lo, bgate, lblk, ablk, gain)


SSD_HALO = 16
SSD_CONV_W = MIX_W + 2 * SSM_GROUPS * SSM_STATE


def _ssd_consts(r):
    expand = np.zeros((LANES, MIX_W), np.float32)
    for h in range(SSM_HEADS):
        expand[SDT_LANE + h, SSM_HD * h:SSM_HD * (h + 1)] = 1.0
    rep = np.zeros((LANES, MIX_W), np.float32)
    for h in range(SSM_HEADS):
        g = h // (SSM_HEADS // SSM_GROUPS)
        for n in range(SSM_STATE):
            rep[SSM_STATE * g + n, SSM_HD * h + n] = 1.0
    ltri = np.tril(np.ones((r, r), np.float32))
    return (jnp.asarray(expand, BF16), jnp.asarray(rep, BF16), jnp.asarray(ltri, BF16),
            jnp.asarray(ltri.T, BF16))


def _ssd_kernel(z_ref, x_ref, xp_ref, b_ref, bp_ref, c_ref, cp_ref, small_ref, cw_ref, cb_ref,
                dtb_ref, alog_ref, dtbc_ref, alogc_ref, dskip_ref, onorm_ref, expand_ref,
                rep_ref, ltri_ref, utri_ref, o_ref, sw_ref, ext_ref, y_ref, *, r):
    first = pl.program_id(1) == 0

    @pl.when(first)
    def _():
        sw_ref[...] = jnp.zeros_like(sw_ref)

    keep_prev = jnp.where(first, 0.0, 1.0)
    ext_ref[0:SSD_HALO, 0:MIX_W] = xp_ref[0].astype(F32) * keep_prev
    ext_ref[0:SSD_HALO, MIX_W:MIX_W + LANES] = bp_ref[0].astype(F32) * keep_prev
    ext_ref[0:SSD_HALO, MIX_W + LANES:SSD_CONV_W] = cp_ref[0].astype(F32) * keep_prev
    ext_ref[SSD_HALO:, 0:MIX_W] = x_ref[0].astype(F32)
    ext_ref[SSD_HALO:, MIX_W:MIX_W + LANES] = b_ref[0].astype(F32)
    ext_ref[SSD_HALO:, MIX_W + LANES:SSD_CONV_W] = c_ref[0].astype(F32)
    conv = cb_ref[...] + cw_ref[SSM_CONV - 1:SSM_CONV, :] * ext_ref[SSD_HALO:, :]
    for back in range(1, SSM_CONV):
        tap = SSM_CONV - 1 - back
        conv = conv + cw_ref[tap:tap + 1, :] * ext_ref[pl.ds(SSD_HALO - back, r), :]
    xbc = _silu(conv)
    xs = xbc[:, 0:MIX_W]
    bm = xbc[:, MIX_W:MIX_W + LANES].astype(BF16)
    cm = xbc[:, MIX_W + LANES:SSD_CONV_W].astype(BF16)

    sm = small_ref[0]
    dt = _dot3_right(_softplus(sm + dtb_ref[...]), expand_ref[...])
    a_neg = -jnp.exp(alog_ref[...])
    acs = _dot3_left(ltri_ref[...], dt * a_neg)
    acs_last = acs[r - 1:r, :]
    sm_t = sm.T
    dt_t = _softplus(sm_t[SDT_LANE:SDT_LANE + SSM_HEADS, :] + dtbc_ref[:, 0:1])
    acs_t = _dot3_right(dt_t * (-jnp.exp(alogc_ref[:, 0:1])), utri_ref[...])

    xdt = (xs * dt).astype(BF16)
    row = lax.broadcasted_iota(jnp.int32, (r, r), 0)
    col = lax.broadcasted_iota(jnp.int32, (r, r), 1)
    causal = col <= row
    glane = lax.broadcasted_iota(jnp.int32, (1, LANES), 1)
    first_half = glane < SSM_STATE
    czero = jnp.zeros_like(cm)
    hpg = SSM_HEADS // SSM_GROUPS
    for g in range(SSM_GROUPS):
        cg = jnp.where(first_half if g == 0 else ~first_half, cm, czero)
        cb_g = _dot_nt(cg, bm)
        for pair in range(hpg // 2):
            p = g * (hpg // 2) + pair
            xp = xdt[:, LANES * p:LANES * (p + 1)]
            xzero = jnp.zeros_like(xp)
            acc = None
            for a in range(2):
                h = 2 * p + a
                dmat = acs[:, SSM_HD * h:SSM_HD * h + 1] - acs_t[h:h + 1, :]
                sc = (cb_g * jnp.exp(jnp.where(causal, dmat, NEG_BIG))).astype(BF16)
                xh = jnp.where(first_half if a == 0 else ~first_half, xp, xzero)
                contrib = _dot(sc, xh)
                acc = contrib if acc is None else acc + contrib
            y_ref[:, LANES * p:LANES * (p + 1)] = acc

    rep = rep_ref[...]
    sw = sw_ref[...]
    cw = (_dot(cm, rep) * jnp.exp(acs)).astype(BF16)
    y = y_ref[...] + _dot(cw, sw.astype(BF16)) + dskip_ref[...] * xs
    bw = (_dot(bm, rep) * jnp.exp(acs_last - acs)).astype(BF16)
    upd = _dot_tn(bw, xdt)
    srow = lax.broadcasted_iota(jnp.int32, (MIX_W, MIX_W), 0)
    scol = lax.broadcasted_iota(jnp.int32, (MIX_W, MIX_W), 1)
    same_head = _shr(srow, SSM_STATE) == _shr(scol, SSM_HD)
    sw_ref[...] = sw * jnp.exp(acs_last) + jnp.where(same_head, upd, 0.0)

    y = y * _silu(z_ref[0].astype(F32))
    gw = MIX_W // SSM_GROUPS
    for g in range(SSM_GROUPS):
        cols = slice(gw * g, gw * (g + 1))
        yg = y[:, cols]
        ms = jnp.mean(yg * yg, axis=-1, keepdims=True)
        o_ref[0, :, cols] = (yg * lax.rsqrt(ms + EPS) * onorm_ref[:, cols]).astype(BF16)


def _ssd(big3, small3, conv_w, conv_b, dtb_w, alog_w, dtb_c, alog_c, dskip_w, onorm, *, r=256):
    b, s, _ = big3.shape
    r = min(r, s)
    consts = _ssd_consts(r)
    hb = r // SSD_HALO

    def cur(width, off):
        return pl.BlockSpec((1, r, width), lambda bi, i: (bi, i, off // width))

    def prev(width, off):
        return pl.BlockSpec((1, SSD_HALO, width),
                            lambda bi, i: (bi, jnp.maximum(i * hb - 1, 0), off // width))

    def whole(shape):
        return pl.BlockSpec(shape, lambda bi, i: (0,) * len(shape))

    return pl.pallas_call(
        functools.partial(_ssd_kernel, r=r),
        grid=(b, s // r),
        in_specs=[
            cur(MIX_W, SZ_OFF),
            cur(MIX_W, SX_OFF), prev(MIX_W, SX_OFF),
            cur(LANES, SB_OFF), prev(LANES, SB_OFF),
            cur(LANES, SC_OFF), prev(LANES, SC_OFF),
            pl.BlockSpec((1, r, LANES), lambda bi, i: (bi, i, 0)),
            whole((SSM_CONV, SSD_CONV_W)), whole((1, SSD_CONV_W)),
            whole((1, LANES)), whole((1, MIX_W)),
            whole((SSM_HEADS, LANES)), whole((SSM_HEADS, LANES)),
            whole((1, MIX_W)), whole((1, MIX_W)),
            whole((LANES, MIX_W)), whole((LANES, MIX_W)), whole((r, r)), whole((r, r)),
        ],
        out_specs=pl.BlockSpec((1, r, MIX_W), lambda bi, i: (bi, i, 0)),
        out_shape=jax.ShapeDtypeStruct((b, s, MIX_W), BF16),
        scratch_shapes=[
            pltpu.VMEM((MIX_W, MIX_W), F32),
            pltpu.VMEM((r + SSD_HALO, SSD_CONV_W), F32),
            pltpu.VMEM((r, MIX_W), F32),
        ],
        compiler_params=_params(("arbitrary", "arbitrary")),
        name="ssd",
    )(big3, big3, big3, big3, big3, big3, big3, small3, conv_w, conv_b, dtb_w, alog_w,
      dtb_c, alog_c, dskip_w, onorm, *consts)


def _merge_kernel(oa_ref, ob_ref, oc_ref, gate_ref, x_ref, wb_ref, bgate_ref, wo_ref, nffn_ref,
                  wrh_ref, wrl_ref, br_ref, ustrict_ref, xe_ref, route_ref, counts_ref, *, tm):
    @pl.when(pl.program_id(0) == 0)
    def _():
        counts_ref[...] = jnp.zeros_like(counts_ref)

    mixed = None
    for ridx, o_ref in enumerate((oa_ref, ob_ref, oc_ref)):
        cols = slice(D_MODEL * ridx, D_MODEL * (ridx + 1))
        gate = _sigmoid(gate_ref[:, cols].astype(F32) + bgate_ref[ridx:ridx + 1, :])
        term = gate * _dot(o_ref[...], wb_ref[ridx])
        mixed = term if mixed is None else mixed + term
    xn = x_ref[...] + _dot(mixed.astype(BF16), wo_ref[...])
    xe_ref[:, 0:D_MODEL] = xn
    ms = jnp.mean(xn * xn, axis=-1, keepdims=True)
    h = xn * lax.rsqrt(ms + EPS) * nffn_ref[...]

    lt = (_dot_f32w(h, wrh_ref[...], wrl_ref[...]) + br_ref[...]).T
    grow = lax.broadcasted_iota(jnp.int32, (8, tm), 0)
    grow_f = grow.astype(F32)
    far_row = float(LANES)
    gl = jnp.where(grow < N_EGROUPS, lt[RG_LANE:RG_LANE + 8, :], NEG_BIG)
    gmax = jnp.max(gl, axis=0, keepdims=True)
    g_w = 1.0 / jnp.sum(jnp.exp(gl - gmax), axis=0, keepdims=True)
    g_sel = jnp.min(jnp.where(gl == gmax, grow_f, far_row), axis=0, keepdims=True)
    e16 = lt[RE_LANE:RE_LANE + N_EXPERTS, :]
    erow = lax.broadcasted_iota(jnp.int32, (N_EXPERTS, tm), 0)
    erow_f = erow.astype(F32)
    in_grp = _shr(erow, EXP_PER_GROUP).astype(F32) == g_sel
    el = jnp.where(in_grp, e16, NEG_BIG)
    e1 = jnp.max(el, axis=0, keepdims=True)
    i1 = jnp.min(jnp.where(in_grp & (el == e1), erow_f, far_row), axis=0, keepdims=True)
    rest = in_grp & (erow_f != i1)
    el2 = jnp.where(rest, e16, NEG_BIG)
    e2 = jnp.max(el2, axis=0, keepdims=True)
    i2 = jnp.min(jnp.where(rest & (el2 == e2), erow_f, far_row), axis=0, keepdims=True)
    ratio = jnp.exp(e2 - e1)
    w1 = g_w / (1.0 + ratio)
    w2 = w1 * ratio
    comb_t = jnp.where(erow_f == i1, w1, 0.0) + jnp.where(erow_f == i2, w2, 0.0)
    comb_t = jnp.concatenate([jnp.zeros((RE_LANE, tm), F32), comb_t,
                              jnp.zeros((LANES - RE_LANE - N_EXPERTS, tm), F32)], axis=0)
    xe_ref[:, D_MODEL:] = comb_t.T

    first_row = EXP_PER_GROUP * g_sel
    lo = jnp.minimum(i1, i2) - first_row
    hi = jnp.maximum(i1, i2) - first_row
    cls = PAIRS_PER_GROUP * g_sel + lo * (7.0 - lo) * 0.5 + (hi - lo - 1.0)
    crow_f = lax.broadcasted_iota(jnp.int32, (ROUTE_ROWS, tm), 0).astype(F32)
    is_cls = crow_f == cls
    onehot = jnp.where(is_cls, 1.0, 0.0)
    before = _dot(onehot.astype(BF16), ustrict_ref[...]) + counts_ref[:, 0:1]
    rank = jnp.sum(jnp.where(is_cls, before, 0.0), axis=0, keepdims=True)
    counts_ref[...] = counts_ref[...] + jnp.sum(onehot, axis=1, keepdims=True)
    rank_hi = jnp.floor(rank * (1.0 / LANES))
    rank_lo = rank - rank_hi * LANES
    route_ref[...] = jnp.where(grow == 0, cls, jnp.where(grow == 1, rank_hi,
                                                         jnp.where(grow == 2, rank_lo, 0.0)))


def _merge(oa, ob, oc, big, x2, wb, bgate, wo, nffn, wr_hi, wr_lo, br, *, tm=512):
    t = x2.shape[0]
    tm = min(tm, t)
    ustrict = jnp.asarray(np.triu(np.ones((tm, tm), np.float32), 1), BF16)

    def whole(shape):
        return pl.BlockSpec(shape, lambda i: (0,) * len(shape))

    return pl.pallas_call(
        functools.partial(_merge_kernel, tm=tm),
        grid=(t // tm,),
        in_specs=[
            pl.BlockSpec((tm, MIX_W), lambda i: (i, 0)),
            pl.BlockSpec((tm, MIX_W), lambda i: (i, 0)),
            pl.BlockSpec((tm, MIX_W), lambda i: (i, 0)),
            pl.BlockSpec((tm, 3 * D_MODEL), lambda i: (i, 0)),
            pl.BlockSpec((tm, D_MODEL), lambda i: (i, 0)),
            whole((3, MIX_W, D_MODEL)), whole((3, D_MODEL)), whole((D_MODEL, D_MODEL)),
            whole((1, D_MODEL)), whole((D_MODEL, LANES)), whole((D_MODEL, LANES)),
            whole((1, LANES)), whole((tm, tm)),
        ],
        out_specs=[
            pl.BlockSpec((tm, XE_COLS), lambda i: (i, 0)),
            pl.BlockSpec((8, tm), lambda i: (0, i)),
            pl.BlockSpec((ROUTE_ROWS, LANES), lambda i: (0, 0)),
        ],
        out_shape=[
            jax.ShapeDtypeStruct((t, XE_COLS), F32),
            jax.ShapeDtypeStruct((8, t), F32),
            jax.ShapeDtypeStruct((ROUTE_ROWS, LANES), F32),
        ],
        compiler_params=_params(("arbitrary",)),
        name="merge",
    )(oa, ob, oc, big, x2, wb, bgate, wo, nffn, wr_hi, wr_lo, br, ustrict)


MOE_TILE = 256
ROW_DMA_TILE = 512


def _row_copy_kernel(pos_ref, src_ref, *rest, tm, scatter):
    dst_ref, sem = rest[-2], rest[-1]

    def copy(r):
        near, far = pl.ds(r, 1), pl.ds(pos_ref[0, 0, r], 1)
        if scatter:
            return pltpu.make_async_copy(src_ref.at[near], dst_ref.at[far], sem)
        return pltpu.make_async_copy(src_ref.at[far], dst_ref.at[near], sem)

    def drain(r, carry):
        copy(r).wait()
        return carry

    for r in range(tm):
        copy(r).start(priority=r % 2)
    lax.fori_loop(0, tm, drain, 0, unroll=8)


def _row_copy(pos, src, dst_init, out_rows, *, name):
    t = pos.shape[0]
    tm = min(ROW_DMA_TILE, t)
    width = src.shape[1]
    any_spec = pl.BlockSpec(memory_space=pl.ANY)
    tile_spec = pl.BlockSpec((tm, width), lambda i: (i, 0))
    pos_spec = pl.BlockSpec((1, 1, tm), lambda i: (i, 0, 0), memory_space=pltpu.SMEM)
    scatter = dst_init is not None
    operands = [pos.reshape(t // tm, 1, tm), src] + ([dst_init] if scatter else [])
    in_specs = [pos_spec] + ([tile_spec, any_spec] if scatter else [any_spec])
    return pl.pallas_call(
        functools.partial(_row_copy_kernel, tm=tm, scatter=scatter),
        grid=(t // tm,),
        in_specs=in_specs,
        out_specs=any_spec if scatter else tile_spec,
        out_shape=jax.ShapeDtypeStruct((out_rows, width), src.dtype),
        scratch_shapes=[pltpu.SemaphoreType.DMA(())],
        input_output_aliases={2: 0} if scatter else {},
        compiler_params=_params(("arbitrary",)),
        name=name,
    )(*operands)


def _moe_sorted_kernel(ea_ref, eb_ref, nused_ref, xs_ref, nffn_ref, wga_ref, wua_ref, wda_ref,
                       wgb_ref, wub_ref, wdb_ref, o_ref, *, tm):
    i = pl.program_id(0)

    @pl.when(i >= nused_ref[0])
    def _():
        o_ref[...] = jnp.zeros_like(o_ref)


    @pl.when(i < nused_ref[0])
    def _():
        xn = xs_ref[:, 0:D_MODEL]
        comb = xs_ref[:, D_MODEL:]
        ms = jnp.mean(xn * xn, axis=-1, keepdims=True)
        h = (xn * lax.rsqrt(ms + EPS) * nffn_ref[...]).astype(BF16)
        lane = lax.broadcasted_iota(jnp.int32, (tm, LANES), 1)
        out = xn
        for e_ref, wg_ref, wu_ref, wd_ref in ((ea_ref, wga_ref, wua_ref, wda_ref),
                                              (eb_ref, wgb_ref, wub_ref, wdb_ref)):
            w = jnp.sum(jnp.where(lane == e_ref[i] + RE_LANE, comb, 0.0), axis=-1, keepdims=True)
            hid = _silu(_dot(h, wg_ref[0])) * _dot(h, wu_ref[0])
            out = out + w * _dot(hid.astype(BF16), wd_ref[0])
        o_ref[...] = out


def _moe_sorted(ea, eb, nused, xs, nffn, wg, wu, wd, *, tm):
    n_tiles = xs.shape[0] // tm

    def w_in(which):
        return pl.BlockSpec((1, D_MODEL, D_EXPERT), lambda i, ea, eb, nu: ((ea, eb)[which][i], 0, 0))

    def w_out(which):
        return pl.BlockSpec((1, D_EXPERT, D_MODEL), lambda i, ea, eb, nu: ((ea, eb)[which][i], 0, 0))

    grid_spec = pltpu.PrefetchScalarGridSpec(
        num_scalar_prefetch=3,
        grid=(n_tiles,),
        in_specs=[
            pl.BlockSpec((tm, XE_COLS), lambda i, ea, eb, nu: (jnp.minimum(i, nu[0] - 1), 0)),
            pl.BlockSpec((1, D_MODEL), lambda i, ea, eb, nu: (0, 0)),
            w_in(0), w_in(0), w_out(0), w_in(1), w_in(1), w_out(1),
        ],
        out_specs=pl.BlockSpec((tm, D_MODEL), lambda i, ea, eb, nu: (i, 0)),
    )
    return pl.pallas_call(
        functools.partial(_moe_sorted_kernel, tm=tm),
        grid_spec=grid_spec,
        out_shape=jax.ShapeDtypeStruct((n_tiles * tm, D_MODEL), F32),
        compiler_params=_params(("arbitrary",)),
        name="moe",
    )(ea, eb, nused, xs, nffn, wg, wu, wd, wg, wu, wd)


_PAIR_LO = np.array([0, 0, 0, 1, 1, 2], np.int32)
_PAIR_HI = np.array([1, 2, 3, 2, 3, 3], np.int32)


def _moe(xe, route, counts, nffn, wg, wu, wd):
    t = xe.shape[0]
    tm = min(MOE_TILE, t)
    n_cls = N_EGROUPS * PAIRS_PER_GROUP
    n_tiles = t // tm + n_cls
    cnt = counts[:n_cls, 0].astype(jnp.int32)
    tiles = (cnt + tm - 1) // tm
    tile_end = jnp.cumsum(tiles)
    n_used = tile_end[-1]
    cls_base = (tile_end - tiles) * tm
    tile_idx = jnp.minimum(jnp.arange(n_tiles), n_used - 1)
    tile_cls = jnp.sum((tile_end[None, :] <= tile_idx[:, None]).astype(jnp.int32), axis=1)
    grp, pair = tile_cls // PAIRS_PER_GROUP, tile_cls % PAIRS_PER_GROUP
    ea = EXP_PER_GROUP * grp + jnp.asarray(_PAIR_LO)[pair]
    eb = EXP_PER_GROUP * grp + jnp.asarray(_PAIR_HI)[pair]
    rank = (route[1] * LANES + route[2]).astype(jnp.int32)
    pos = cls_base[route[0].astype(jnp.int32)] + rank

    xs = _row_copy(pos, xe, jnp.zeros((n_tiles * tm, XE_COLS), F32), n_tiles * tm,
                   name="moe_scatter")
    ys = _moe_sorted(ea, eb, n_used.reshape(1), xs, nffn, wg, wu, wd, tm=tm)
    return _row_copy(pos, ys, None, t, name="moe_gather")


def _cols(w, off, width):
    return w[:, off:off + width]


def _pad_lanes(v, lane0, width=LANES):
    out = jnp.zeros((1, width), F32)
    return out.at[0, lane0:lane0 + v.shape[0]].set(v.astype(F32))


def _layer(x2, b, s, norm_mix, w_in, fox_f_bias, fox_q_norm, fox_k_norm, gla_w_lr, gla_b_gate,
           gla_out_norm, ssm_conv_w, ssm_conv_b, ssm_dt_bias, ssm_a_log, ssm_d, ssm_out_norm,
           w_branch, b_branch_gate, w_out, norm_ffn, w_router_grp, b_router_grp,
           w_router_exp, b_router_exp, w_exp_gate, w_exp_up, w_exp_down):
    t = b * s
    w_big = jnp.concatenate([
        _cols(w_in, _O_GATE, 3 * D_MODEL), _cols(w_in, _O_FQ, MIX_W), _cols(w_in, _O_FK, MIX_W),
        _cols(w_in, _O_FV, MIX_W), _cols(w_in, _O_GV, MIX_W), _cols(w_in, _O_GR, MIX_W),
        _cols(w_in, _O_SZ, MIX_W), _cols(w_in, _O_SX, MIX_W), _cols(w_in, _O_GQ, 256),
        _cols(w_in, _O_GK, 256), _cols(w_in, _O_SB, LANES), _cols(w_in, _O_SC, LANES),
    ], axis=1).astype(BF16)
    w_small = jnp.concatenate([
        _cols(w_in, _O_FF, FOX_HEADS), _cols(w_in, _O_GLR, GLA_RANK), _cols(w_in, _O_SDT, SSM_HEADS),
        jnp.zeros((D_MODEL, LANES - FOX_HEADS - GLA_RANK - SSM_HEADS), F32),
    ], axis=1)
    ws_hi, ws_lo = _split2(w_small)

    big, small = _inproj(x2, norm_mix.reshape(1, D_MODEL), w_big, ws_hi, ws_lo)
    big3 = big.reshape(b, s, BIG_COLS)
    small3 = small.reshape(b, s, LANES)

    qt, kf, vt = _fox_prep(big3, small3, _pad_lanes(fox_f_bias, FF_LANE),
                           jnp.tile(fox_q_norm, FOX_HEADS).reshape(1, MIX_W),
                           jnp.tile(fox_k_norm, FOX_HEADS).reshape(1, MIX_W), ts=min(FOX_TILE, s))
    logit_bound = ((FOX_HD ** 0.5) * 1.01 * jnp.max(jnp.abs(fox_q_norm))
                   * jnp.max(jnp.abs(fox_k_norm)))
    o_a = _fox_attn(qt, kf, vt, logit_bound)

    wl = jnp.zeros((LANES, GLA_HEADS * GLA_DK), F32).at[GLR_LANE:GLR_LANE + GLA_RANK].set(gla_w_lr)
    wl_hi, wl_lo = _split2(wl)
    o_b = _gla(big3, small3, wl_hi, wl_lo, gla_b_gate.reshape(1, -1),
               gla_out_norm.reshape(1, GLA_DV))

    o_c = _ssd(big3, small3, ssm_conv_w, ssm_conv_b.reshape(1, -1),
               _pad_lanes(ssm_dt_bias, SDT_LANE),
               jnp.repeat(ssm_a_log, SSM_HD).reshape(1, MIX_W),
               jnp.broadcast_to(ssm_dt_bias[:, None], (SSM_HEADS, LANES)),
               jnp.broadcast_to(ssm_a_log[:, None], (SSM_HEADS, LANES)),
               jnp.repeat(ssm_d, SSM_HD).reshape(1, MIX_W),
               ssm_out_norm.reshape(1, MIX_W))

    w_r = jnp.concatenate([w_router_grp, jnp.zeros((D_MODEL, RE_LANE - N_EGROUPS), F32), w_router_exp,
                           jnp.zeros((D_MODEL, LANES - RE_LANE - N_EXPERTS), F32)], axis=1)
    wr_hi, wr_lo = _split2(w_r)
    b_r = jnp.concatenate([b_router_grp, jnp.zeros((RE_LANE - N_EGROUPS,), F32), b_router_exp,
                           jnp.zeros((LANES - RE_LANE - N_EXPERTS,), F32)]).reshape(1, LANES)
    nffn = norm_ffn.reshape(1, D_MODEL)
    xe, route, counts = _merge(o_a.reshape(t, MIX_W), o_b.reshape(t, MIX_W), o_c.reshape(t, MIX_W),
                               big, x2, w_branch.astype(BF16), b_branch_gate, w_out.astype(BF16),
                               nffn, wr_hi, wr_lo, b_r)

    return _moe(xe, route, counts, nffn, w_exp_gate.astype(BF16), w_exp_up.astype(BF16),
                w_exp_down.astype(BF16))


def kernel(x, norm_mix, w_in, fox_f_bias, fox_q_norm, fox_k_norm, gla_w_lr, gla_b_gate, gla_out_norm, ssm_conv_w, ssm_conv_b, ssm_dt_bias, ssm_a_log, ssm_d, ssm_out_norm, w_branch, b_branch_gate, w_out, norm_ffn, w_router_grp, b_router_grp, w_router_exp, b_router_exp, w_exp_gate, w_exp_up, w_exp_down):
    b, s, d = x.shape
    x2 = x.reshape(b * s, d)
    per_layer = (norm_mix, w_in, fox_f_bias, fox_q_norm, fox_k_norm, gla_w_lr, gla_b_gate,
                 gla_out_norm, ssm_conv_w, ssm_conv_b, ssm_dt_bias, ssm_a_log, ssm_d,
                 ssm_out_norm, w_branch, b_branch_gate, w_out, norm_ffn, w_router_grp,
                 b_router_grp, w_router_exp, b_router_exp, w_exp_gate, w_exp_up, w_exp_down)
    for l in range(norm_mix.shape[0]):
        x2 = _layer(x2, b, s, *[p[l] for p in per_layer])
    return x2.reshape(b, s, d)
```

```python
import functools

import numpy as np
import jax
import jax.numpy as jnp
from jax import lax
from jax.experimental import pallas as pl
from jax.experimental.pallas import tpu as pltpu

F32 = jnp.float32
BF16 = jnp.bfloat16

D_MODEL = 1024
MIX_W = 512
EPS = 1e-6
FOX_HEADS = 8
FOX_HD = 64
FOX_PAIRS = FOX_HEADS // 2
GLA_HEADS = 4
GLA_DK = 64
GLA_DV = 128
GLA_RANK = 16
GLA_GATE_NORM = 16.0
GLA_CHUNK = 64
SSM_HEADS = 8
SSM_HD = 64
SSM_GROUPS = 2
SSM_STATE = 64
SSM_CONV = 4
N_EGROUPS = 4
EXP_PER_GROUP = 4
N_EXPERTS = 16
PAIRS_PER_GROUP = 6
D_EXPERT = 512

LANES = 128
NEG_BIG = -1e30
LOG2E = 1.4426950408889634
VMEM_LIMIT = 56 * 1024 * 1024

GATE_OFF, FQ_OFF, FK_OFF, FV_OFF = 0, 3072, 3584, 4096
GV_OFF, GR_OFF, SZ_OFF, SX_OFF = 4608, 5120, 5632, 6144
GQ_OFF, GK_OFF, SB_OFF, SC_OFF = 6656, 6912, 7168, 7296
BIG_COLS = 7424
FF_LANE, GLR_LANE, SDT_LANE = 0, 8, 24
_O_FQ, _O_FK, _O_FV, _O_FF = 0, 512, 1024, 1536
_O_GQ, _O_GK, _O_GV, _O_GR, _O_GLR = 1544, 1800, 2056, 2568, 3080
_O_SZ, _O_SX, _O_SB, _O_SC, _O_SDT, _O_GATE = 3096, 3608, 4120, 4248, 4376, 4384
RG_LANE, RE_LANE = 0, 8
ROUTE_ROWS = 32
XE_COLS = D_MODEL + LANES


def _split2(x):
    hi = x.astype(BF16)
    lo = (x - hi.astype(F32)).astype(BF16)
    return hi, lo


def _split3(x):
    x1 = x.astype(BF16)
    r = x - x1.astype(F32)
    x2 = r.astype(BF16)
    x3 = (r - x2.astype(F32)).astype(BF16)
    return x1, x2, x3


def _dot(a, b):
    return jnp.dot(a, b, preferred_element_type=F32)


def _dot_nt(a, b):
    return lax.dot_general(a, b, (((1,), (1,)), ((), ())), preferred_element_type=F32)


def _dot_tn(a, b):
    return lax.dot_general(a, b, (((0,), (0,)), ((), ())), preferred_element_type=F32)


def _dot3_left(m_bf16, x_f32):
    x1, x2, x3 = _split3(x_f32)
    return _dot(m_bf16, x1) + _dot(m_bf16, x2) + _dot(m_bf16, x3)


def _dot3_right(x_f32, m_bf16):
    x1, x2, x3 = _split3(x_f32)
    return _dot(x1, m_bf16) + _dot(x2, m_bf16) + _dot(x3, m_bf16)


def _dot_f32w(x_f32, w_hi, w_lo):
    x_hi, x_lo = _split2(x_f32)
    return _dot(x_hi, w_hi) + _dot(x_lo, w_hi) + _dot(x_hi, w_lo)


def _shr(x, pow2):
    return jnp.right_shift(x, pow2.bit_length() - 1)


def _log_sigmoid(x):
    return jnp.minimum(x, 0.0) - jnp.log(1.0 + jnp.exp(-jnp.abs(x)))


def _softplus(x):
    return jnp.maximum(x, 0.0) + jnp.log(1.0 + jnp.exp(-jnp.abs(x)))


def _sigmoid(x):
    return 0.5 * jnp.tanh(0.5 * x) + 0.5


def _silu(x):
    return x * _sigmoid(x)


def _params(sem):
    return pltpu.CompilerParams(dimension_semantics=sem, vmem_limit_bytes=VMEM_LIMIT)


def _inproj_kernel(x_ref, g_ref, w_ref, wsh_ref, wsl_ref, big_ref, small_ref, *, tn):
    x = x_ref[...]
    ms = jnp.mean(x * x, axis=-1, keepdims=True)
    h = x * lax.rsqrt(ms + EPS) * g_ref[...]
    hb = h.astype(BF16)
    for c in range(BIG_COLS // tn):
        cols = slice(c * tn, (c + 1) * tn)
        big_ref[:, cols] = _dot(hb, w_ref[:, cols]).astype(BF16)
    h_lo = (h - hb.astype(F32)).astype(BF16)
    wsh = wsh_ref[...]
    small_ref[...] = _dot(hb, wsh) + _dot(h_lo, wsh) + _dot(hb, wsl_ref[...])


def _inproj(x2, gain, w_big, ws_hi, ws_lo, *, tm=512, tn=256):
    t = x2.shape[0]
    return pl.pallas_call(
        functools.partial(_inproj_kernel, tn=tn),
        grid=(t // tm,),
        in_specs=[
            pl.BlockSpec((tm, D_MODEL), lambda i: (i, 0)),
            pl.BlockSpec((1, D_MODEL), lambda i: (0, 0)),
            pl.BlockSpec((D_MODEL, BIG_COLS), lambda i: (0, 0), pipeline_mode=pl.Buffered(1)),
            pl.BlockSpec((D_MODEL, LANES), lambda i: (0, 0)),
            pl.BlockSpec((D_MODEL, LANES), lambda i: (0, 0)),
        ],
        out_specs=[
            pl.BlockSpec((tm, BIG_COLS), lambda i: (i, 0)),
            pl.BlockSpec((tm, LANES), lambda i: (i, 0)),
        ],
        out_shape=[
            jax.ShapeDtypeStruct((t, BIG_COLS), BF16),
            jax.ShapeDtypeStruct((t, LANES), F32),
        ],
        compiler_params=_params(("arbitrary",)),
        name="inproj",
    )(x2, gain, w_big, ws_hi, ws_lo)


def _fox_consts(ts):
    ltri = np.tril(np.ones((ts, ts), np.float32))
    hsum = np.kron(np.eye(FOX_HEADS, dtype=np.float32), np.ones((FOX_HD, FOX_HD), np.float32))
    sq = np.zeros((3, LANES, MIX_W), np.float32)
    sk = np.zeros((3, LANES, MIX_W), np.float32)
    oneq = np.zeros((1, MIX_W), np.float32)
    onek = np.zeros((1, MIX_W), np.float32)
    for h in range(FOX_HEADS):
        base = LANES * (h // 2) + 6 * (h % 2)
        for j in range(3):
            sq[j, FF_LANE + h, base + j] = 1.0
            sk[j, FF_LANE + h, base + 3 + j] = -1.0
            oneq[0, base + 3 + j] = 1.0
            onek[0, base + j] = 1.0
    saug = np.concatenate([sq.reshape(3 * LANES, MIX_W), sk.reshape(3 * LANES, MIX_W)], axis=1)
    return (jnp.asarray(ltri, BF16), jnp.asarray(hsum, BF16), jnp.asarray(saug, BF16),
            jnp.asarray(oneq), jnp.asarray(onek))


def _fox_prep_kernel(fq_ref, fk_ref, fv_ref, small_ref, fbias_ref, qg_ref, kg_ref, ltri_ref,
                     hsum_ref, saug_ref, oneq_ref, onek_ref, qt_ref, kf_ref, vt_ref, carry_ref,
                     *, ts):
    @pl.when(pl.program_id(1) == 0)
    def _():
        carry_ref[...] = jnp.zeros_like(carry_ref)

    lane = lax.broadcasted_iota(jnp.int32, (ts, LANES), 1)
    f = small_ref[0] + fbias_ref[...]
    ls = jnp.where(lane < FOX_HEADS, _log_sigmoid(f) * LOG2E, 0.0)
    parts = _dot(ltri_ref[...], jnp.concatenate(_split3(ls), axis=1))
    c = parts[:, 0:LANES] + parts[:, LANES:2 * LANES] + parts[:, 2 * LANES:] + carry_ref[0:1, :]
    carry_ref[...] = jnp.broadcast_to(c[ts - 1:ts, :], carry_ref.shape)
    aug = _dot(jnp.concatenate(_split3(c), axis=1), saug_ref[...])
    qaug = aug[:, 0:MIX_W] + oneq_ref[...]
    kaug = aug[:, MIX_W:] + onek_ref[...]

    hsum = hsum_ref[...]

    def head_norm(xb, gain):
        x = xb.astype(F32)
        s_hi, s_lo = _split2(x * x)
        ss = _dot(s_hi, hsum) + _dot(s_lo, hsum)
        return x * lax.rsqrt(ss * (1.0 / FOX_HD) + EPS) * gain

    qn = head_norm(fq_ref[0], qg_ref[...]) * (FOX_HD ** -0.5 * LOG2E)
    kn = head_norm(fk_ref[0], kg_ref[...])
    v = fv_ref[0].astype(F32)
    for p in range(FOX_PAIRS):
        src = slice(LANES * p, LANES * (p + 1))
        dst_x = slice(2 * LANES * p, 2 * LANES * p + LANES)
        dst_a = slice(2 * LANES * p + LANES, 2 * LANES * (p + 1))
        kf_ref[0, :, dst_x] = kn[:, src].astype(BF16)
        kf_ref[0, :, dst_a] = kaug[:, src].astype(BF16)
        qt_ref[0, p, 0, 0:LANES, :] = qn[:, src].T.astype(BF16)
        qt_ref[0, p, 0, LANES:2 * LANES, :] = qaug[:, src].T.astype(BF16)
        vt_ref[0, p, 0] = v[:, src].T.astype(BF16)


def _fox_prep(big3, small3, fbias, qgain, kgain, *, ts):
    b, s, _ = big3.shape
    consts = _fox_consts(ts)
    const_specs = [
        pl.BlockSpec((ts, ts), lambda bi, i: (0, 0)),
        pl.BlockSpec((MIX_W, MIX_W), lambda bi, i: (0, 0)),
        pl.BlockSpec((3 * LANES, 2 * MIX_W), lambda bi, i: (0, 0)),
        pl.BlockSpec((1, MIX_W), lambda bi, i: (0, 0)),
        pl.BlockSpec((1, MIX_W), lambda bi, i: (0, 0)),
    ]
    return pl.pallas_call(
        functools.partial(_fox_prep_kernel, ts=ts),
        grid=(b, s // ts),
        in_specs=[
            pl.BlockSpec((1, ts, MIX_W), lambda bi, i: (bi, i, FQ_OFF // MIX_W)),
            pl.BlockSpec((1, ts, MIX_W), lambda bi, i: (bi, i, FK_OFF // MIX_W)),
            pl.BlockSpec((1, ts, MIX_W), lambda bi, i: (bi, i, FV_OFF // MIX_W)),
            pl.BlockSpec((1, ts, LANES), lambda bi, i: (bi, i, 0)),
            pl.BlockSpec((1, LANES), lambda bi, i: (0, 0)),
            pl.BlockSpec((1, MIX_W), lambda bi, i: (0, 0)),
            pl.BlockSpec((1, MIX_W), lambda bi, i: (0, 0)),
        ] + const_specs,
        out_specs=[
            pl.BlockSpec((1, FOX_PAIRS, 1, 2 * LANES, ts), lambda bi, i: (bi, 0, i, 0, 0)),
            pl.BlockSpec((1, ts, 2 * MIX_W), lambda bi, i: (bi, i, 0)),
            pl.BlockSpec((1, FOX_PAIRS, 1, LANES, ts), lambda bi, i: (bi, 0, i, 0, 0)),
        ],
        out_shape=[
            jax.ShapeDtypeStruct((b, FOX_PAIRS, s // ts, 2 * LANES, ts), BF16),
            jax.ShapeDtypeStruct((b, s, 2 * MIX_W), BF16),
            jax.ShapeDtypeStruct((b, FOX_PAIRS, s // ts, LANES, ts), BF16),
        ],
        scratch_shapes=[pltpu.VMEM((8, LANES), F32)],
        compiler_params=_params(("arbitrary", "arbitrary")),
        name="fox_prep",
    )(big3, big3, big3, small3, fbias, qgain, kgain, *consts)


FOX_NOSHIFT_BOUND = 40.0
FOX_TILE = 512
FOX_UNROLL = 4


def _fox_attn_kernel(qt_ref, k_ref, vt_ref, o_ref, acc_ref, l_ref, m_ref, *, tq, online):
    i = pl.program_id(2)
    qt = qt_ref[0, 0, 0]
    qrow = lax.broadcasted_iota(jnp.int32, (2 * LANES, 1), 0)
    in_a = (qrow < FOX_HD) | ((qrow >= LANES) & (qrow < LANES + 6))
    in_b = ((qrow >= FOX_HD) & (qrow < LANES)) | ((qrow >= LANES + 6) & (qrow < LANES + 12))
    zero = jnp.zeros_like(qt)
    qt_heads = (jnp.where(in_a, qt, zero), jnp.where(in_b, qt, zero))

    acc_ref[...] = jnp.zeros_like(acc_ref)
    l_ref[...] = jnp.zeros_like(l_ref)
    if online:
        m_ref[...] = jnp.full_like(m_ref, NEG_BIG)

    def scores(j, a, diag):
        k = k_ref[0, pl.ds(pl.multiple_of(j * tq, tq), tq), :]
        st = _dot(k, qt_heads[a])
        if diag:
            krow = lax.broadcasted_iota(jnp.int32, (tq, tq), 0)
            qcol = lax.broadcasted_iota(jnp.int32, (tq, tq), 1)
            st = jnp.where(krow <= qcol, st, NEG_BIG)
        return st

    def accumulate(j, a, st):
        vt_a = vt_ref[0, 0, j, FOX_HD * a:FOX_HD * (a + 1), :]
        if online:
            m_prev = m_ref[a]
            m_new = jnp.maximum(m_prev, jnp.max(st, axis=0, keepdims=True))
            alpha = jnp.exp2(m_prev - m_new)
            m_ref[a] = m_new
            pt = jnp.exp2(st - m_new)
            l_ref[a] = alpha * l_ref[a] + jnp.sum(pt.reshape(tq // 8, 8, tq), axis=0)
            acc_ref[a] = alpha * acc_ref[a] + _dot(vt_a, pt.astype(BF16))
        else:
            pt = jnp.exp2(st)
            l_ref[a] += jnp.sum(pt.reshape(tq // 8, 8, tq), axis=0)
            acc_ref[a] += _dot(vt_a, pt.astype(BF16))

    def run(units):
        st = scores(*units[0])
        for u, unit in enumerate(units):
            st_next = scores(*units[u + 1]) if u + 1 < len(units) else None
            accumulate(unit[0], unit[1], st)
            st = st_next

    def units(first_block, n_regular, with_diag):
        blocks = [(first_block + d, False) for d in range(n_regular)]
        if with_diag:
            blocks.append((first_block + n_regular, True))
        return [(j, a, diag) for j, diag in blocks for a in range(2)]

    def body(jj, carry):
        run(units(FOX_UNROLL * jj, FOX_UNROLL, False))
        return carry

    lax.fori_loop(0, i // FOX_UNROLL, body, 0)
    for rem in range(FOX_UNROLL):
        @pl.when(i % FOX_UNROLL == rem)
        def _():
            run(units(i - rem, rem, True))

    halves = [acc_ref[a] * (1.0 / jnp.sum(l_ref[a], axis=0, keepdims=True)) for a in range(2)]
    o_ref[0] = jnp.concatenate(halves, axis=0).T.astype(BF16)


def _fox_attn(qt, kf, vt, logit_bound):
    b, _, nq, _, tq = qt.shape
    s = nq * tq

    def call(online, name):
        return pl.pallas_call(
            functools.partial(_fox_attn_kernel, tq=tq, online=online),
            grid=(b, FOX_PAIRS, nq),
            in_specs=[
                pl.BlockSpec((1, 1, 1, 2 * LANES, tq), lambda bi, p, i: (bi, p, i, 0, 0)),
                pl.BlockSpec((1, s, 2 * LANES), lambda bi, p, i: (bi, 0, p)),
                pl.BlockSpec((1, 1, nq, LANES, tq), lambda bi, p, i: (bi, p, 0, 0, 0)),
            ],
            out_specs=pl.BlockSpec((1, tq, LANES), lambda bi, p, i: (bi, i, p)),
            out_shape=jax.ShapeDtypeStruct((b, s, MIX_W), BF16),
            scratch_shapes=[
                pltpu.VMEM((2, FOX_HD, tq), F32),
                pltpu.VMEM((2, 8, tq), F32),
                pltpu.VMEM((2, 1, tq), F32),
            ],
            compiler_params=_params(("arbitrary", "arbitrary", "arbitrary")),
            name=name,
        )(qt, kf, vt)

    return lax.cond(logit_bound < FOX_NOSHIFT_BOUND,
                    lambda: call(False, "fox_attn"), lambda: call(True, "fox_attn_online"))


def _gla_consts(r):
    idx = np.arange(r)
    same = (idx[:, None] // GLA_CHUNK) == (idx[None, :] // GLA_CHUNK)
    lblk = (same & (idx[None, :] <= idx[:, None])).astype(np.float32)
    ablk = same.astype(np.float32)
    return jnp.asarray(lblk, BF16), jnp.asarray(ablk, BF16)


def _gla_kernel(q_ref, k_ref, v_ref, r_ref, small_ref, wlh_ref, wll_ref, bg_ref, lblk_ref,
                ablk_ref, gain_ref, o_ref, st_ref, oacc_ref, *, r):
    @pl.when(pl.program_id(1) == 0)
    def _():
        st_ref[...] = jnp.zeros_like(st_ref)

    kw = GLA_HEADS * GLA_DK
    gate = _dot_f32w(small_ref[0], wlh_ref[...], wll_ref[...]) + bg_ref[...]
    log_a = _log_sigmoid(gate) * (1.0 / GLA_GATE_NORM)
    a1, a2, a3 = _split3(log_a)
    lblk = lblk_ref[...]
    ablk = ablk_ref[...]
    bcum = _dot(lblk, a1) + _dot(lblk, a2) + _dot(lblk, a3)
    btot = _dot(ablk, a1) + _dot(ablk, a2) + _dot(ablk, a3)
    q = q_ref[0].astype(F32) * (GLA_DK ** -0.5)
    k = k_ref[0].astype(F32)
    q_dec = (q * jnp.exp(bcum)).astype(BF16)
    k_dec = (k * jnp.exp(-bcum)).astype(BF16)
    k_end_t = (k * jnp.exp(btot - bcum)).T.astype(BF16)
    d_tot_t = jnp.exp(btot).T
    v = v_ref[0]

    row = lax.broadcasted_iota(jnp.int32, (r, r), 0)
    col = lax.broadcasted_iota(jnp.int32, (r, r), 1)
    keep = (_shr(row, GLA_CHUNK) == _shr(col, GLA_CHUNK)) & (col <= row)
    klane = lax.broadcasted_iota(jnp.int32, (1, kw), 1)
    qzero = jnp.zeros_like(q_dec)
    q_heads = []
    for h in range(GLA_HEADS):
        in_h = (klane >= GLA_DK * h) & (klane < GLA_DK * (h + 1))
        q_heads.append(jnp.where(in_h, q_dec, qzero))
        att = _dot_nt(q_heads[h], k_dec)
        att = jnp.where(keep, att, 0.0).astype(BF16)
        vcols = slice(GLA_DV * h, GLA_DV * (h + 1))
        oacc_ref[:, vcols] = _dot(att, v[:, vcols])

    tlane = lax.broadcasted_iota(jnp.int32, (1, r), 1)
    kzero = jnp.zeros_like(k_end_t)
    for c in range(r // GLA_CHUNK):
        rows = slice(GLA_CHUNK * c, GLA_CHUNK * (c + 1))
        st = st_ref[...]
        st_b = st.astype(BF16)
        k_chunk = jnp.where(_shr(tlane, GLA_CHUNK) == c, k_end_t, kzero)
        updates = []
        for h in range(GLA_HEADS):
            vcols = slice(GLA_DV * h, GLA_DV * (h + 1))
            oacc_ref[rows, vcols] += _dot(q_heads[h][rows], st_b)
            updates.append(_dot(k_chunk[GLA_DK * h:GLA_DK * (h + 1), :], v[:, vcols]))
        decay = d_tot_t[:, GLA_CHUNK * c:GLA_CHUNK * c + 1]
        st_ref[...] = decay * st + jnp.concatenate(updates, axis=0)

    gain = gain_ref[...]
    gr = r_ref[0].astype(F32)
    for h in range(GLA_HEADS):
        vcols = slice(GLA_DV * h, GLA_DV * (h + 1))
        o = oacc_ref[:, vcols]
        ms = jnp.mean(o * o, axis=-1, keepdims=True)
        o_ref[0, :, vcols] = (o * lax.rsqrt(ms + EPS) * gain * _silu(gr[:, vcols])).astype(BF16)


def _gla(big3, small3, wl_hi, wl_lo, bgate, gain, *, r=256):
    b, s, _ = big3.shape
    r = min(r, s)
    kw = GLA_HEADS * GLA_DK
    lblk, ablk = _gla_consts(r)
    return pl.pallas_call(
        functools.partial(_gla_kernel, r=r),
        grid=(b, s // r),
        in_specs=[
            pl.BlockSpec((1, r, kw), lambda bi, i: (bi, i, GQ_OFF // kw)),
            pl.BlockSpec((1, r, kw), lambda bi, i: (bi, i, GK_OFF // kw)),
            pl.BlockSpec((1, r, MIX_W), lambda bi, i: (bi, i, GV_OFF // MIX_W)),
            pl.BlockSpec((1, r, MIX_W), lambda bi, i: (bi, i, GR_OFF // MIX_W)),
            pl.BlockSpec((1, r, LANES), lambda bi, i: (bi, i, 0)),
            pl.BlockSpec((LANES, kw), lambda bi, i: (0, 0)),
            pl.BlockSpec((LANES, kw), lambda bi, i: (0, 0)),
            pl.BlockSpec((1, kw), lambda bi, i: (0, 0)),
            pl.BlockSpec((r, r), lambda bi, i: (0, 0)),
            pl.BlockSpec((r, r), lambda bi, i: (0, 0)),
            pl.BlockSpec((1, GLA_DV), lambda bi, i: (0, 0)),
        ],
        out_specs=pl.BlockSpec((1, r, MIX_W), lambda bi, i: (bi, i, 0)),
        out_shape=jax.ShapeDtypeStruct((b, s, MIX_W), BF16),
        scratch_shapes=[pltpu.VMEM((kw, GLA_DV), F32), pltpu.VMEM((r, MIX_W), F32)],
        compiler_params=_params(("arbitrary", "arbitrary")),
        name="gla",
    )(big3, big3, big3, big3, small3, wl_hi, wl_lo, bgate, lblk, ablk, gain)


SSD_HALO = 16
SSD_CONV_W = MIX_W + 2 * SSM_GROUPS * SSM_STATE


def _ssd_consts(r):
    expand = np.zeros((LANES, MIX_W), np.float32)
    for h in range(SSM_HEADS):
        expand[SDT_LANE + h, SSM_HD * h:SSM_HD * (h + 1)] = 1.0
    rep = np.zeros((LANES, MIX_W), np.float32)
    for h in range(SSM_HEADS):
        g = h // (SSM_HEADS // SSM_GROUPS)
        for n in range(SSM_STATE):
            rep[SSM_STATE * g + n, SSM_HD * h + n] = 1.0
    ltri = np.tril(np.ones((r, r), np.float32))
    return (jnp.asarray(expand, BF16), jnp.asarray(rep, BF16), jnp.asarray(ltri, BF16),
            jnp.asarray(ltri.T, BF16))


def _ssd_kernel(z_ref, x_ref, xp_ref, b_ref, bp_ref, c_ref, cp_ref, small_ref, cw_ref, cb_ref,
                dtb_ref, alog_ref, dtbc_ref, alogc_ref, dskip_ref, onorm_ref, expand_ref,
                rep_ref, ltri_ref, utri_ref, o_ref, sw_ref, ext_ref, y_ref, *, r):
    first = pl.program_id(1) == 0

    @pl.when(first)
    def _():
        sw_ref[...] = jnp.zeros_like(sw_ref)

    keep_prev = jnp.where(first, 0.0, 1.0)
    ext_ref[0:SSD_HALO, 0:MIX_W] = xp_ref[0].astype(F32) * keep_prev
    ext_ref[0:SSD_HALO, MIX_W:MIX_W + LANES] = bp_ref[0].astype(F32) * keep_prev
    ext_ref[0:SSD_HALO, MIX_W + LANES:SSD_CONV_W] = cp_ref[0].astype(F32) * keep_prev
    ext_ref[SSD_HALO:, 0:MIX_W] = x_ref[0].astype(F32)
    ext_ref[SSD_HALO:, MIX_W:MIX_W + LANES] = b_ref[0].astype(F32)
    ext_ref[SSD_HALO:, MIX_W + LANES:SSD_CONV_W] = c_ref[0].astype(F32)
    conv = cb_ref[...] + cw_ref[SSM_CONV - 1:SSM_CONV, :] * ext_ref[SSD_HALO:, :]
    for back in range(1, SSM_CONV):
        tap = SSM_CONV - 1 - back
        conv = conv + cw_ref[tap:tap + 1, :] * ext_ref[pl.ds(SSD_HALO - back, r), :]
    xbc = _silu(conv)
    xs = xbc[:, 0:MIX_W]
    bm = xbc[:, MIX_W:MIX_W + LANES].astype(BF16)
    cm = xbc[:, MIX_W + LANES:SSD_CONV_W].astype(BF16)

    sm = small_ref[0]
    dt = _dot3_right(_softplus(sm + dtb_ref[...]), expand_ref[...])
    a_neg = -jnp.exp(alog_ref[...])
    acs = _dot3_left(ltri_ref[...], dt * a_neg)
    acs_last = acs[r - 1:r, :]
    sm_t = sm.T
    dt_t = _softplus(sm_t[SDT_LANE:SDT_LANE + SSM_HEADS, :] + dtbc_ref[:, 0:1])
    acs_t = _dot3_right(dt_t * (-jnp.exp(alogc_ref[:, 0:1])), utri_ref[...])

    xdt = (xs * dt).astype(BF16)
    row = lax.broadcasted_iota(jnp.int32, (r, r), 0)
    col = lax.broadcasted_iota(jnp.int32, (r, r), 1)
    causal = col <= row
    glane = lax.broadcasted_iota(jnp.int32, (1, LANES), 1)
    first_half = glane < SSM_STATE
    czero = jnp.zeros_like(cm)
    hpg = SSM_HEADS // SSM_GROUPS
    for g in range(SSM_GROUPS):
        cg = jnp.where(first_half if g == 0 else ~first_half, cm, czero)
        cb_g = _dot_nt(cg, bm)
        for pair in range(hpg // 2):
            p = g * (hpg // 2) + pair
            xp = xdt[:, LANES * p:LANES * (p + 1)]
            xzero = jnp.zeros_like(xp)
            acc = None
            for a in range(2):
                h = 2 * p + a
                dmat = acs[:, SSM_HD * h:SSM_HD * h + 1] - acs_t[h:h + 1, :]
                sc = (cb_g * jnp.exp(jnp.where(causal, dmat, NEG_BIG))).astype(BF16)
                xh = jnp.where(first_half if a == 0 else ~first_half, xp, xzero)
                contrib = _dot(sc, xh)
                acc = contrib if acc is None else acc + contrib
            y_ref[:, LANES * p:LANES * (p + 1)] = acc

    rep = rep_ref[...]
    sw = sw_ref[...]
    cw = (_dot(cm, rep) * jnp.exp(acs)).astype(BF16)
    y = y_ref[...] + _dot(cw, sw.astype(BF16)) + dskip_ref[...] * xs
    bw = (_dot(bm, rep) * jnp.exp(acs_last - acs)).astype(BF16)
    upd = _dot_tn(bw, xdt)
    srow = lax.broadcasted_iota(jnp.int32, (MIX_W, MIX_W), 0)
    scol = lax.broadcasted_iota(jnp.int32, (MIX_W, MIX_W), 1)
    same_head = _shr(srow, SSM_STATE) == _shr(scol, SSM_HD)
    sw_ref[...] = sw * jnp.exp(acs_last) + jnp.where(same_head, upd, 0.0)

    y = y * _silu(z_ref[0].astype(F32))
    gw = MIX_W // SSM_GROUPS
    for g in range(SSM_GROUPS):
        cols = slice(gw * g, gw * (g + 1))
        yg = y[:, cols]
        ms = jnp.mean(yg * yg, axis=-1, keepdims=True)
        o_ref[0, :, cols] = (yg * lax.rsqrt(ms + EPS) * onorm_ref[:, cols]).astype(BF16)


def _ssd(big3, small3, conv_w, conv_b, dtb_w, alog_w, dtb_c, alog_c, dskip_w, onorm, *, r=256):
    b, s, _ = big3.shape
    r = min(r, s)
    consts = _ssd_consts(r)
    hb = r // SSD_HALO

    def cur(width, off):
        return pl.BlockSpec((1, r, width), lambda bi, i: (bi, i, off // width))

    def prev(width, off):
        return pl.BlockSpec((1, SSD_HALO, width),
                            lambda bi, i: (bi, jnp.maximum(i * hb - 1, 0), off // width))

    def whole(shape):
        return pl.BlockSpec(shape, lambda bi, i: (0,) * len(shape))

    return pl.pallas_call(
        functools.partial(_ssd_kernel, r=r),
        grid=(b, s // r),
        in_specs=[
            cur(MIX_W, SZ_OFF),
            cur(MIX_W, SX_OFF), prev(MIX_W, SX_OFF),
            cur(LANES, SB_OFF), prev(LANES, SB_OFF),
            cur(LANES, SC_OFF), prev(LANES, SC_OFF),
            pl.BlockSpec((1, r, LANES), lambda bi, i: (bi, i, 0)),
            whole((SSM_CONV, SSD_CONV_W)), whole((1, SSD_CONV_W)),
            whole((1, LANES)), whole((1, MIX_W)),
            whole((SSM_HEADS, LANES)), whole((SSM_HEADS, LANES)),
            whole((1, MIX_W)), whole((1, MIX_W)),
            whole((LANES, MIX_W)), whole((LANES, MIX_W)), whole((r, r)), whole((r, r)),
        ],
        out_specs=pl.BlockSpec((1, r, MIX_W), lambda bi, i: (bi, i, 0)),
        out_shape=jax.ShapeDtypeStruct((b, s, MIX_W), BF16),
        scratch_shapes=[
            pltpu.VMEM((MIX_W, MIX_W), F32),
            pltpu.VMEM((r + SSD_HALO, SSD_CONV_W), F32),
            pltpu.VMEM((r, MIX_W), F32),
        ],
        compiler_params=_params(("arbitrary", "arbitrary")),
        name="ssd",
    )(big3, big3, big3, big3, big3, big3, big3, small3, conv_w, conv_b, dtb_w, alog_w,
      dtb_c, alog_c, dskip_w, onorm, *consts)


def _merge_kernel(oa_ref, ob_ref, oc_ref, gate_ref, x_ref, wb_ref, bgate_ref, wo_ref, nffn_ref,
                  wrh_ref, wrl_ref, br_ref, ustrict_ref, xe_ref, route_ref, counts_ref, *, tm):
    @pl.when(pl.program_id(0) == 0)
    def _():
        counts_ref[...] = jnp.zeros_like(counts_ref)

    mixed = None
    for ridx, o_ref in enumerate((oa_ref, ob_ref, oc_ref)):
        cols = slice(D_MODEL * ridx, D_MODEL * (ridx + 1))
        gate = _sigmoid(gate_ref[:, cols].astype(F32) + bgate_ref[ridx:ridx + 1, :])
        term = gate * _dot(o_ref[...], wb_ref[ridx])
        mixed = term if mixed is None else mixed + term
    xn = x_ref[...] + _dot(mixed.astype(BF16), wo_ref[...])
    xe_ref[:, 0:D_MODEL] = xn
    ms = jnp.mean(xn * xn, axis=-1, keepdims=True)
    h = xn * lax.rsqrt(ms + EPS) * nffn_ref[...]

    lt = (_dot_f32w(h, wrh_ref[...], wrl_ref[...]) + br_ref[...]).T
    grow = lax.broadcasted_iota(jnp.int32, (8, tm), 0)
    grow_f = grow.astype(F32)
    far_row = float(LANES)
    gl = jnp.where(grow < N_EGROUPS, lt[RG_LANE:RG_LANE + 8, :], NEG_BIG)
    gmax = jnp.max(gl, axis=0, keepdims=True)
    g_w = 1.0 / jnp.sum(jnp.exp(gl - gmax), axis=0, keepdims=True)
    g_sel = jnp.min(jnp.where(gl == gmax, grow_f, far_row), axis=0, keepdims=True)
    e16 = lt[RE_LANE:RE_LANE + N_EXPERTS, :]
    erow = lax.broadcasted_iota(jnp.int32, (N_EXPERTS, tm), 0)
    erow_f = erow.astype(F32)
    in_grp = _shr(erow, EXP_PER_GROUP).astype(F32) == g_sel
    el = jnp.where(in_grp, e16, NEG_BIG)
    e1 = jnp.max(el, axis=0, keepdims=True)
    i1 = jnp.min(jnp.where(in_grp & (el == e1), erow_f, far_row), axis=0, keepdims=True)
    rest = in_grp & (erow_f != i1)
    el2 = jnp.where(rest, e16, NEG_BIG)
    e2 = jnp.max(el2, axis=0, keepdims=True)
    i2 = jnp.min(jnp.where(rest & (el2 == e2), erow_f, far_row), axis=0, keepdims=True)
    ratio = jnp.exp(e2 - e1)
    w1 = g_w / (1.0 + ratio)
    w2 = w1 * ratio
    comb_t = jnp.where(erow_f == i1, w1, 0.0) + jnp.where(erow_f == i2, w2, 0.0)
    comb_t = jnp.concatenate([jnp.zeros((RE_LANE, tm), F32), comb_t,
                              jnp.zeros((LANES - RE_LANE - N_EXPERTS, tm), F32)], axis=0)
    xe_ref[:, D_MODEL:] = comb_t.T

    first_row = EXP_PER_GROUP * g_sel
    lo = jnp.minimum(i1, i2) - first_row
    hi = jnp.maximum(i1, i2) - first_row
    cls = PAIRS_PER_GROUP * g_sel + lo * (7.0 - lo) * 0.5 + (hi - lo - 1.0)
    crow_f = lax.broadcasted_iota(jnp.int32, (ROUTE_ROWS, tm), 0).astype(F32)
    is_cls = crow_f == cls
    onehot = jnp.where(is_cls, 1.0, 0.0)
    before = _dot(onehot.astype(BF16), ustrict_ref[...]) + counts_ref[:, 0:1]
    rank = jnp.sum(jnp.where(is_cls, before, 0.0), axis=0, keepdims=True)
    counts_ref[...] = counts_ref[...] + jnp.sum(onehot, axis=1, keepdims=True)
    rank_hi = jnp.floor(rank * (1.0 / LANES))
    rank_lo = rank - rank_hi * LANES
    route_ref[...] = jnp.where(grow == 0, cls, jnp.where(grow == 1, rank_hi,
                                                         jnp.where(grow == 2, rank_lo, 0.0)))


def _merge(oa, ob, oc, big, x2, wb, bgate, wo, nffn, wr_hi, wr_lo, br, *, tm=512):
    t = x2.shape[0]
    tm = min(tm, t)
    ustrict = jnp.asarray(np.triu(np.ones((tm, tm), np.float32), 1), BF16)

    def whole(shape):
        return pl.BlockSpec(shape, lambda i: (0,) * len(shape))

    return pl.pallas_call(
        functools.partial(_merge_kernel, tm=tm),
        grid=(t // tm,),
        in_specs=[
            pl.BlockSpec((tm, MIX_W), lambda i: (i, 0)),
            pl.BlockSpec((tm, MIX_W), lambda i: (i, 0)),
            pl.BlockSpec((tm, MIX_W), lambda i: (i, 0)),
            pl.BlockSpec((tm, 3 * D_MODEL), lambda i: (i, 0)),
            pl.BlockSpec((tm, D_MODEL), lambda i: (i, 0)),
            whole((3, MIX_W, D_MODEL)), whole((3, D_MODEL)), whole((D_MODEL, D_MODEL)),
            whole((1, D_MODEL)), whole((D_MODEL, LANES)), whole((D_MODEL, LANES)),
            whole((1, LANES)), whole((tm, tm)),
        ],
        out_specs=[
            pl.BlockSpec((tm, XE_COLS), lambda i: (i, 0)),
            pl.BlockSpec((8, tm), lambda i: (0, i)),
            pl.BlockSpec((ROUTE_ROWS, LANES), lambda i: (0, 0)),
        ],
        out_shape=[
            jax.ShapeDtypeStruct((t, XE_COLS), F32),
            jax.ShapeDtypeStruct((8, t), F32),
            jax.ShapeDtypeStruct((ROUTE_ROWS, LANES), F32),
        ],
        compiler_params=_params(("arbitrary",)),
        name="merge",
    )(oa, ob, oc, big, x2, wb, bgate, wo, nffn, wr_hi, wr_lo, br, ustrict)


MOE_TILE = 256
ROW_DMA_TILE = 512


def _row_copy_kernel(pos_ref, src_ref, *rest, tm, scatter):
    dst_ref, sem = rest[-2], rest[-1]

    def copy(r):
        near, far = pl.ds(r, 1), pl.ds(pos_ref[0, 0, r], 1)
        if scatter:
            return pltpu.make_async_copy(src_ref.at[near], dst_ref.at[far], sem)
        return pltpu.make_async_copy(src_ref.at[far], dst_ref.at[near], sem)

    def drain(r, carry):
        copy(r).wait()
        return carry

    for r in range(tm):
        copy(r).start(priority=r % 2)
    lax.fori_loop(0, tm, drain, 0, unroll=8)


def _row_copy(pos, src, dst_init, out_rows, *, name):
    t = pos.shape[0]
    tm = min(ROW_DMA_TILE, t)
    width = src.shape[1]
    any_spec = pl.BlockSpec(memory_space=pl.ANY)
    tile_spec = pl.BlockSpec((tm, width), lambda i: (i, 0))
    pos_spec = pl.BlockSpec((1, 1, tm), lambda i: (i, 0, 0), memory_space=pltpu.SMEM)
    scatter = dst_init is not None
    operands = [pos.reshape(t // tm, 1, tm), src] + ([dst_init] if scatter else [])
    in_specs = [pos_spec] + ([tile_spec, any_spec] if scatter else [any_spec])
    return pl.pallas_call(
        functools.partial(_row_copy_kernel, tm=tm, scatter=scatter),
        grid=(t // tm,),
        in_specs=in_specs,
        out_specs=any_spec if scatter else tile_spec,
        out_shape=jax.ShapeDtypeStruct((out_rows, width), src.dtype),
        scratch_shapes=[pltpu.SemaphoreType.DMA(())],
        input_output_aliases={2: 0} if scatter else {},
        compiler_params=_params(("arbitrary",)),
        name=name,
    )(*operands)


def _moe_sorted_kernel(ea_ref, eb_ref, nused_ref, xs_ref, nffn_ref, wga_ref, wua_ref, wda_ref,
                       wgb_ref, wub_ref, wdb_ref, o_ref, *, tm):
    i = pl.program_id(0)

    @pl.when(i >= nused_ref[0])
    def _():
        o_ref[...] = jnp.zeros_like(o_ref)


    @pl.when(i < nused_ref[0])
    def _():
        xn = xs_ref[:, 0:D_MODEL]
        comb = xs_ref[:, D_MODEL:]
        ms = jnp.mean(xn * xn, axis=-1, keepdims=True)
        h = (xn * lax.rsqrt(ms + EPS) * nffn_ref[...]).astype(BF16)
        lane = lax.broadcasted_iota(jnp.int32, (tm, LANES), 1)
        out = xn
        for e_ref, wg_ref, wu_ref, wd_ref in ((ea_ref, wga_ref, wua_ref, wda_ref),
                                              (eb_ref, wgb_ref, wub_ref, wdb_ref)):
            w = jnp.sum(jnp.where(lane == e_ref[i] + RE_LANE, comb, 0.0), axis=-1, keepdims=True)
            hid = _silu(_dot(h, wg_ref[0])) * _dot(h, wu_ref[0])
            out = out + w * _dot(hid.astype(BF16), wd_ref[0])
        o_ref[...] = out


def _moe_sorted(ea, eb, nused, xs, nffn, wg, wu, wd, *, tm):
    n_tiles = xs.shape[0] // tm

    def w_in(which):
        return pl.BlockSpec((1, D_MODEL, D_EXPERT), lambda i, ea, eb, nu: ((ea, eb)[which][i], 0, 0))

    def w_out(which):
        return pl.BlockSpec((1, D_EXPERT, D_MODEL), lambda i, ea, eb, nu: ((ea, eb)[which][i], 0, 0))

    grid_spec = pltpu.PrefetchScalarGridSpec(
        num_scalar_prefetch=3,
        grid=(n_tiles,),
        in_specs=[
            pl.BlockSpec((tm, XE_COLS), lambda i, ea, eb, nu: (jnp.minimum(i, nu[0] - 1), 0)),
            pl.BlockSpec((1, D_MODEL), lambda i, ea, eb, nu: (0, 0)),
            w_in(0), w_in(0), w_out(0), w_in(1), w_in(1), w_out(1),
        ],
        out_specs=pl.BlockSpec((tm, D_MODEL), lambda i, ea, eb, nu: (i, 0)),
    )
    return pl.pallas_call(
        functools.partial(_moe_sorted_kernel, tm=tm),
        grid_spec=grid_spec,
        out_shape=jax.ShapeDtypeStruct((n_tiles * tm, D_MODEL), F32),
        compiler_params=_params(("arbitrary",)),
        name="moe",
    )(ea, eb, nused, xs, nffn, wg, wu, wd, wg, wu, wd)


_PAIR_LO = np.array([0, 0, 0, 1, 1, 2], np.int32)
_PAIR_HI = np.array([1, 2, 3, 2, 3, 3], np.int32)


def _moe(xe, route, counts, nffn, wg, wu, wd):
    t = xe.shape[0]
    tm = min(MOE_TILE, t)
    n_cls = N_EGROUPS * PAIRS_PER_GROUP
    n_tiles = t // tm + n_cls
    cnt = counts[:n_cls, 0].astype(jnp.int32)
    tiles = (cnt + tm - 1) // tm
    tile_end = jnp.cumsum(tiles)
    n_used = tile_end[-1]
    cls_base = (tile_end - tiles) * tm
    tile_idx = jnp.minimum(jnp.arange(n_tiles), n_used - 1)
    tile_cls = jnp.sum((tile_end[None, :] <= tile_idx[:, None]).astype(jnp.int32), axis=1)
    grp, pair = tile_cls // PAIRS_PER_GROUP, tile_cls % PAIRS_PER_GROUP
    ea = EXP_PER_GROUP * grp + jnp.asarray(_PAIR_LO)[pair]
    eb = EXP_PER_GROUP * grp + jnp.asarray(_PAIR_HI)[pair]
    rank = (route[1] * LANES + route[2]).astype(jnp.int32)
    pos = cls_base[route[0].astype(jnp.int32)] + rank

    xs = _row_copy(pos, xe, jnp.zeros((n_tiles * tm, XE_COLS), F32), n_tiles * tm,
                   name="moe_scatter")
    ys = _moe_sorted(ea, eb, n_used.reshape(1), xs, nffn, wg, wu, wd, tm=tm)
    return _row_copy(pos, ys, None, t, name="moe_gather")


def _cols(w, off, width):
    return w[:, off:off + width]


def _pad_lanes(v, lane0, width=LANES):
    out = jnp.zeros((1, width), F32)
    return out.at[0, lane0:lane0 + v.shape[0]].set(v.astype(F32))


def _layer(x2, b, s, norm_mix, w_in, fox_f_bias, fox_q_norm, fox_k_norm, gla_w_lr, gla_b_gate,
           gla_out_norm, ssm_conv_w, ssm_conv_b, ssm_dt_bias, ssm_a_log, ssm_d, ssm_out_norm,
           w_branch, b_branch_gate, w_out, norm_ffn, w_router_grp, b_router_grp,
           w_router_exp, b_router_exp, w_exp_gate, w_exp_up, w_exp_down):
    t = b * s
    w_big = jnp.concatenate([
        _cols(w_in, _O_GATE, 3 * D_MODEL), _cols(w_in, _O_FQ, MIX_W), _cols(w_in, _O_FK, MIX_W),
        _cols(w_in, _O_FV, MIX_W), _cols(w_in, _O_GV, MIX_W), _cols(w_in, _O_GR, MIX_W),
        _cols(w_in, _O_SZ, MIX_W), _cols(w_in, _O_SX, MIX_W), _cols(w_in, _O_GQ, 256),
        _cols(w_in, _O_GK, 256), _cols(w_in, _O_SB, LANES), _cols(w_in, _O_SC, LANES),
    ], axis=1).astype(BF16)
    w_small = jnp.concatenate([
        _cols(w_in, _O_FF, FOX_HEADS), _cols(w_in, _O_GLR, GLA_RANK), _cols(w_in, _O_SDT, SSM_HEADS),
        jnp.zeros((D_MODEL, LANES - FOX_HEADS - GLA_RANK - SSM_HEADS), F32),
    ], axis=1)
    ws_hi, ws_lo = _split2(w_small)

    big, small = _inproj(x2, norm_mix.reshape(1, D_MODEL), w_big, ws_hi, ws_lo)
    big3 = big.reshape(b, s, BIG_COLS)
    small3 = small.reshape(b, s, LANES)

    qt, kf, vt = _fox_prep(big3, small3, _pad_lanes(fox_f_bias, FF_LANE),
                           jnp.tile(fox_q_norm, FOX_HEADS).reshape(1, MIX_W),
                           jnp.tile(fox_k_norm, FOX_HEADS).reshape(1, MIX_W), ts=min(FOX_TILE, s))
    logit_bound = ((FOX_HD ** 0.5) * 1.01 * jnp.max(jnp.abs(fox_q_norm))
                   * jnp.max(jnp.abs(fox_k_norm)))
    o_a = _fox_attn(qt, kf, vt, logit_bound)

    wl = jnp.zeros((LANES, GLA_HEADS * GLA_DK), F32).at[GLR_LANE:GLR_LANE + GLA_RANK].set(gla_w_lr)
    wl_hi, wl_lo = _split2(wl)
    o_b = _gla(big3, small3, wl_hi, wl_lo, gla_b_gate.reshape(1, -1),
               gla_out_norm.reshape(1, GLA_DV))

    o_c = _ssd(big3, small3, ssm_conv_w, ssm_conv_b.reshape(1, -1),
               _pad_lanes(ssm_dt_bias, SDT_LANE),
               jnp.repeat(ssm_a_log, SSM_HD).reshape(1, MIX_W),
               jnp.broadcast_to(ssm_dt_bias[:, None], (SSM_HEADS, LANES)),
               jnp.broadcast_to(ssm_a_log[:, None], (SSM_HEADS, LANES)),
               jnp.repeat(ssm_d, SSM_HD).reshape(1, MIX_W),
               ssm_out_norm.reshape(1, MIX_W))

    w_r = jnp.concatenate([w_router_grp, jnp.zeros((D_MODEL, RE_LANE - N_EGROUPS), F32), w_router_exp,
                           jnp.zeros((D_MODEL, LANES - RE_LANE - N_EXPERTS), F32)], axis=1)
    wr_hi, wr_lo = _split2(w_r)
    b_r = jnp.concatenate([b_router_grp, jnp.zeros((RE_LANE - N_EGROUPS,), F32), b_router_exp,
                           jnp.zeros((LANES - RE_LANE - N_EXPERTS,), F32)]).reshape(1, LANES)
    nffn = norm_ffn.reshape(1, D_MODEL)
    xe, route, counts = _merge(o_a.reshape(t, MIX_W), o_b.reshape(t, MIX_W), o_c.reshape(t, MIX_W),
                               big, x2, w_branch.astype(BF16), b_branch_gate, w_out.astype(BF16),
                               nffn, wr_hi, wr_lo, b_r)

    return _moe(xe, route, counts, nffn, w_exp_gate.astype(BF16), w_exp_up.astype(BF16),
                w_exp_down.astype(BF16))


def kernel(x, norm_mix, w_in, fox_f_bias, fox_q_norm, fox_k_norm, gla_w_lr, gla_b_gate, gla_out_norm, ssm_conv_w, ssm_conv_b, ssm_dt_bias, ssm_a_log, ssm_d, ssm_out_norm, w_branch, b_branch_gate, w_out, norm_ffn, w_router_grp, b_router_grp, w_router_exp, b_router_exp, w_exp_gate, w_exp_up, w_exp_down):
    b, s, d = x.shape
    x2 = x.reshape(b * s, d)
    per_layer = (norm_mix, w_in, fox_f_bias, fox_q_norm, fox_k_norm, gla_w_lr, gla_b_gate,
                 gla_out_norm, ssm_conv_w, ssm_conv_b, ssm_dt_bias, ssm_a_log, ssm_d,
                 ssm_out_norm, w_branch, b_branch_gate, w_out, norm_ffn, w_router_grp,
                 b_router_grp, w_router_exp, b_router_exp, w_exp_gate, w_exp_up, w_exp_down)
    for l in range(norm_mix.shape[0]):
        x2 = _layer(x2, b, s, *[p[l] for p in per_layer])
    return x2.reshape(b, s, d)
```

```python
import functools

import numpy as np
import jax
import jax.numpy as jnp
from jax import lax
from jax.experimental import pallas as pl
from jax.experimental.pallas import tpu as pltpu

F32 = jnp.float32
BF16 = jnp.bfloat16

D_MODEL = 1024
MIX_W = 512
EPS = 1e-6
FOX_HEADS = 8
FOX_HD = 64
FOX_PAIRS = FOX_HEADS // 2
GLA_HEADS = 4
GLA_DK = 64
GLA_DV = 128
GLA_RANK = 16
GLA_GATE_NORM = 16.0
GLA_CHUNK = 64
SSM_HEADS = 8
SSM_HD = 64
SSM_GROUPS = 2
SSM_STATE = 64
SSM_CONV = 4
N_EGROUPS = 4
EXP_PER_GROUP = 4
N_EXPERTS = 16
PAIRS_PER_GROUP = 6
D_EXPERT = 512

LANES = 128
NEG_BIG = -1e30
LOG2E = 1.4426950408889634
VMEM_LIMIT = 56 * 1024 * 1024

GATE_OFF, FQ_OFF, FK_OFF, FV_OFF = 0, 3072, 3584, 4096
GV_OFF, GR_OFF, SZ_OFF, SX_OFF = 4608, 5120, 5632, 6144
GQ_OFF, GK_OFF, SB_OFF, SC_OFF = 6656, 6912, 7168, 7296
BIG_COLS = 7424
FF_LANE, GLR_LANE, SDT_LANE = 0, 8, 24
_O_FQ, _O_FK, _O_FV, _O_FF = 0, 512, 1024, 1536
_O_GQ, _O_GK, _O_GV, _O_GR, _O_GLR = 1544, 1800, 2056, 2568, 3080
_O_SZ, _O_SX, _O_SB, _O_SC, _O_SDT, _O_GATE = 3096, 3608, 4120, 4248, 4376, 4384
RG_LANE, RE_LANE = 0, 8
ROUTE_ROWS = 32
XE_COLS = D_MODEL + LANES


def _split2(x):
    hi = x.astype(BF16)
    lo = (x - hi.astype(F32)).astype(BF16)
    return hi, lo


def _split3(x):
    x1 = x.astype(BF16)
    r = x - x1.astype(F32)
    x2 = r.astype(BF16)
    x3 = (r - x2.astype(F32)).astype(BF16)
    return x1, x2, x3


def _dot(a, b):
    return jnp.dot(a, b, preferred_element_type=F32)


def _dot_nt(a, b):
    return lax.dot_general(a, b, (((1,), (1,)), ((), ())), preferred_element_type=F32)


def _dot_tn(a, b):
    return lax.dot_general(a, b, (((0,), (0,)), ((), ())), preferred_element_type=F32)


def _dot3_left(m_bf16, x_f32):
    x1, x2, x3 = _split3(x_f32)
    return _dot(m_bf16, x1) + _dot(m_bf16, x2) + _dot(m_bf16, x3)


def _dot3_right(x_f32, m_bf16):
    x1, x2, x3 = _split3(x_f32)
    return _dot(x1, m_bf16) + _dot(x2, m_bf16) + _dot(x3, m_bf16)


def _dot_f32w(x_f32, w_hi, w_lo):
    x_hi, x_lo = _split2(x_f32)
    return _dot(x_hi, w_hi) + _dot(x_lo, w_hi) + _dot(x_hi, w_lo)


def _shr(x, pow2):
    return jnp.right_shift(x, pow2.bit_length() - 1)


def _log_sigmoid(x):
    return jnp.minimum(x, 0.0) - jnp.log(1.0 + jnp.exp(-jnp.abs(x)))


def _softplus(x):
    return jnp.maximum(x, 0.0) + jnp.log(1.0 + jnp.exp(-jnp.abs(x)))


def _sigmoid(x):
    return 0.5 * jnp.tanh(0.5 * x) + 0.5


def _silu(x):
    return x * _sigmoid(x)


def _params(sem):
    return pltpu.CompilerParams(dimension_semantics=sem, vmem_limit_bytes=VMEM_LIMIT)


def _inproj_kernel(x_ref, g_ref, w_ref, wsh_ref, wsl_ref, big_ref, small_ref, *, tn):
    x = x_ref[...]
    ms = jnp.mean(x * x, axis=-1, keepdims=True)
    h = x * lax.rsqrt(ms + EPS) * g_ref[...]
    hb = h.astype(BF16)
    for c in range(BIG_COLS // tn):
        cols = slice(c * tn, (c + 1) * tn)
        big_ref[:, cols] = _dot(hb, w_ref[:, cols]).astype(BF16)
    h_lo = (h - hb.astype(F32)).astype(BF16)
    wsh = wsh_ref[...]
    small_ref[...] = _dot(hb, wsh) + _dot(h_lo, wsh) + _dot(hb, wsl_ref[...])


def _inproj(x2, gain, w_big, ws_hi, ws_lo, *, tm=512, tn=256):
    t = x2.shape[0]
    return pl.pallas_call(
        functools.partial(_inproj_kernel, tn=tn),
        grid=(t // tm,),
        in_specs=[
            pl.BlockSpec((tm, D_MODEL), lambda i: (i, 0)),
            pl.BlockSpec((1, D_MODEL), lambda i: (0, 0)),
            pl.BlockSpec((D_MODEL, BIG_COLS), lambda i: (0, 0), pipeline_mode=pl.Buffered(1)),
            pl.BlockSpec((D_MODEL, LANES), lambda i: (0, 0)),
            pl.BlockSpec((D_MODEL, LANES), lambda i: (0, 0)),
        ],
        out_specs=[
            pl.BlockSpec((tm, BIG_COLS), lambda i: (i, 0)),
            pl.BlockSpec((tm, LANES), lambda i: (i, 0)),
        ],
        out_shape=[
            jax.ShapeDtypeStruct((t, BIG_COLS), BF16),
            jax.ShapeDtypeStruct((t, LANES), F32),
        ],
        compiler_params=_params(("arbitrary",)),
        name="inproj",
    )(x2, gain, w_big, ws_hi, ws_lo)


def _fox_consts(ts):
    ltri = np.tril(np.ones((ts, ts), np.float32))
    hsum = np.kron(np.eye(FOX_HEADS, dtype=np.float32), np.ones((FOX_HD, FOX_HD), np.float32))
    sq = np.zeros((3, LANES, MIX_W), np.float32)
    sk = np.zeros((3, LANES, MIX_W), np.float32)
    oneq = np.zeros((1, MIX_W), np.float32)
    onek = np.zeros((1, MIX_W), np.float32)
    for h in range(FOX_HEADS):
        base = LANES * (h // 2) + 6 * (h % 2)
        for j in range(3):
            sq[j, FF_LANE + h, base + j] = 1.0
            sk[j, FF_LANE + h, base + 3 + j] = -1.0
            oneq[0, base + 3 + j] = 1.0
            onek[0, base + j] = 1.0
    saug = np.concatenate([sq.reshape(3 * LANES, MIX_W), sk.reshape(3 * LANES, MIX_W)], axis=1)
    return (jnp.asarray(ltri, BF16), jnp.asarray(hsum, BF16), jnp.asarray(saug, BF16),
            jnp.asarray(oneq), jnp.asarray(onek))


def _fox_prep_kernel(fq_ref, fk_ref, fv_ref, small_ref, fbias_ref, qg_ref, kg_ref, ltri_ref,
                     hsum_ref, saug_ref, oneq_ref, onek_ref, qt_ref, kf_ref, vt_ref, carry_ref,
                     *, ts):
    @pl.when(pl.program_id(1) == 0)
    def _():
        carry_ref[...] = jnp.zeros_like(carry_ref)

    lane = lax.broadcasted_iota(jnp.int32, (ts, LANES), 1)
    f = small_ref[0] + fbias_ref[...]
    ls = jnp.where(lane < FOX_HEADS, _log_sigmoid(f) * LOG2E, 0.0)
    parts = _dot(ltri_ref[...], jnp.concatenate(_split3(ls), axis=1))
    c = parts[:, 0:LANES] + parts[:, LANES:2 * LANES] + parts[:, 2 * LANES:] + carry_ref[0:1, :]
    carry_ref[...] = jnp.broadcast_to(c[ts - 1:ts, :], carry_ref.shape)
    aug = _dot(jnp.concatenate(_split3(c), axis=1), saug_ref[...])
    qaug = aug[:, 0:MIX_W] + oneq_ref[...]
    kaug = aug[:, MIX_W:] + onek_ref[...]

    hsum = hsum_ref[...]

    def head_norm(xb, gain):
        x = xb.astype(F32)
        s_hi, s_lo = _split2(x * x)
        ss = _dot(s_hi, hsum) + _dot(s_lo, hsum)
        return x * lax.rsqrt(ss * (1.0 / FOX_HD) + EPS) * gain

    qn = head_norm(fq_ref[0], qg_ref[...]) * (FOX_HD ** -0.5 * LOG2E)
    kn = head_norm(fk_ref[0], kg_ref[...])
    v = fv_ref[0].astype(F32)
    for p in range(FOX_PAIRS):
        src = slice(LANES * p, LANES * (p + 1))
        dst_x = slice(2 * LANES * p, 2 * LANES * p + LANES)
        dst_a = slice(2 * LANES * p + LANES, 2 * LANES * (p + 1))
        kf_ref[0, :, dst_x] = kn[:, src].astype(BF16)
        kf_ref[0, :, dst_a] = kaug[:, src].astype(BF16)
        qt_ref[0, p, 0, 0:LANES, :] = qn[:, src].T.astype(BF16)
        qt_ref[0, p, 0, LANES:2 * LANES, :] = qaug[:, src].T.astype(BF16)
        vt_ref[0, p, 0] = v[:, src].T.astype(BF16)


def _fox_prep(big3, small3, fbias, qgain, kgain, *, ts):
    b, s, _ = big3.shape
    consts = _fox_consts(ts)
    const_specs = [
        pl.BlockSpec((ts, ts), lambda bi, i: (0, 0)),
        pl.BlockSpec((MIX_W, MIX_W), lambda bi, i: (0, 0)),
        pl.BlockSpec((3 * LANES, 2 * MIX_W), lambda bi, i: (0, 0)),
        pl.BlockSpec((1, MIX_W), lambda bi, i: (0, 0)),
        pl.BlockSpec((1, MIX_W), lambda bi, i: (0, 0)),
    ]
    return pl.pallas_call(
        functools.partial(_fox_prep_kernel, ts=ts),
        grid=(b, s // ts),
        in_specs=[
            pl.BlockSpec((1, ts, MIX_W), lambda bi, i: (bi, i, FQ_OFF // MIX_W)),
            pl.BlockSpec((1, ts, MIX_W), lambda bi, i: (bi, i, FK_OFF // MIX_W)),
            pl.BlockSpec((1, ts, MIX_W), lambda bi, i: (bi, i, FV_OFF // MIX_W)),
            pl.BlockSpec((1, ts, LANES), lambda bi, i: (bi, i, 0)),
            pl.BlockSpec((1, LANES), lambda bi, i: (0, 0)),
            pl.BlockSpec((1, MIX_W), lambda bi, i: (0, 0)),
            pl.BlockSpec((1, MIX_W), lambda bi, i: (0, 0)),
        ] + const_specs,
        out_specs=[
            pl.BlockSpec((1, FOX_PAIRS, 1, 2 * LANES, ts), lambda bi, i: (bi, 0, i, 0, 0)),
            pl.BlockSpec((1, ts, 2 * MIX_W), lambda bi, i: (bi, i, 0)),
            pl.BlockSpec((1, FOX_PAIRS, 1, LANES, ts), lambda bi, i: (bi, 0, i, 0, 0)),
        ],
        out_shape=[
            jax.ShapeDtypeStruct((b, FOX_PAIRS, s // ts, 2 * LANES, ts), BF16),
            jax.ShapeDtypeStruct((b, s, 2 * MIX_W), BF16),
            jax.ShapeDtypeStruct((b, FOX_PAIRS, s // ts, LANES, ts), BF16),
        ],
        scratch_shapes=[pltpu.VMEM((8, LANES), F32)],
        compiler_params=_params(("arbitrary", "arbitrary")),
        name="fox_prep",
    )(big3, big3, big3, small3, fbias, qgain, kgain, *consts)


FOX_NOSHIFT_BOUND = 40.0
FOX_TILE = 512
FOX_UNROLL = 4


def _fox_attn_kernel(qt_ref, k_ref, vt_ref, o_ref, acc_ref, l_ref, m_ref, *, tq, online):
    i = pl.program_id(2)
    qt = qt_ref[0, 0, 0]
    qrow = lax.broadcasted_iota(jnp.int32, (2 * LANES, 1), 0)
    in_a = (qrow < FOX_HD) | ((qrow >= LANES) & (qrow < LANES + 6))
    in_b = ((qrow >= FOX_HD) & (qrow < LANES)) | ((qrow >= LANES + 6) & (qrow < LANES + 12))
    zero = jnp.zeros_like(qt)
    qt_heads = (jnp.where(in_a, qt, zero), jnp.where(in_b, qt, zero))

    acc_ref[...] = jnp.zeros_like(acc_ref)
    l_ref[...] = jnp.zeros_like(l_ref)
    if online:
        m_ref[...] = jnp.full_like(m_ref, NEG_BIG)

    def scores(j, a, diag):
        k = k_ref[0, pl.ds(pl.multiple_of(j * tq, tq), tq), :]
        st = _dot(k, qt_heads[a])
        if diag:
            krow = lax.broadcasted_iota(jnp.int32, (tq, tq), 0)
            qcol = lax.broadcasted_iota(jnp.int32, (tq, tq), 1)
            st = jnp.where(krow <= qcol, st, NEG_BIG)
        return st

    def accumulate(j, a, st):
        vt_a = vt_ref[0, 0, j, FOX_HD * a:FOX_HD * (a + 1), :]
        if online:
            m_prev = m_ref[a]
            m_new = jnp.maximum(m_prev, jnp.max(st, axis=0, keepdims=True))
            alpha = jnp.exp2(m_prev - m_new)
            m_ref[a] = m_new
            pt = jnp.exp2(st - m_new)
            l_ref[a] = alpha * l_ref[a] + jnp.sum(pt.reshape(tq // 8, 8, tq), axis=0)
            acc_ref[a] = alpha * acc_ref[a] + _dot(vt_a, pt.astype(BF16))
        else:
            pt = jnp.exp2(st)
            l_ref[a] += jnp.sum(pt.reshape(tq // 8, 8, tq), axis=0)
            acc_ref[a] += _dot(vt_a, pt.astype(BF16))

    def run(units):
        st = scores(*units[0])
        for u, unit in enumerate(units):
            st_next = scores(*units[u + 1]) if u + 1 < len(units) else None
            accumulate(unit[0], unit[1], st)
            st = st_next

    def units(first_block, n_regular, with_diag):
        blocks = [(first_block + d, False) for d in range(n_regular)]
        if with_diag:
            blocks.append((first_block + n_regular, True))
        return [(j, a, diag) for j, diag in blocks for a in range(2)]

    def body(jj, carry):
        run(units(FOX_UNROLL * jj, FOX_UNROLL, False))
        return carry

    lax.fori_loop(0, i // FOX_UNROLL, body, 0)
    for rem in range(FOX_UNROLL):
        @pl.when(i % FOX_UNROLL == rem)
        def _():
            run(units(i - rem, rem, True))

    halves = [acc_ref[a] * (1.0 / jnp.sum(l_ref[a], axis=0, keepdims=True)) for a in range(2)]
    o_ref[0] = jnp.concatenate(halves, axis=0).T.astype(BF16)


def _fox_attn(qt, kf, vt, logit_bound):
    b, _, nq, _, tq = qt.shape
    s = nq * tq

    def call(online, name):
        return pl.pallas_call(
            functools.partial(_fox_attn_kernel, tq=tq, online=online),
            grid=(b, FOX_PAIRS, nq),
            in_specs=[
                pl.BlockSpec((1, 1, 1, 2 * LANES, tq), lambda bi, p, i: (bi, p, i, 0, 0)),
                pl.BlockSpec((1, s, 2 * LANES), lambda bi, p, i: (bi, 0, p)),
                pl.BlockSpec((1, 1, nq, LANES, tq), lambda bi, p, i: (bi, p, 0, 0, 0)),
            ],
            out_specs=pl.BlockSpec((1, tq, LANES), lambda bi, p, i: (bi, i, p)),
            out_shape=jax.ShapeDtypeStruct((b, s, MIX_W), BF16),
            scratch_shapes=[
                pltpu.VMEM((2, FOX_HD, tq), F32),
                pltpu.VMEM((2, 8, tq), F32),
                pltpu.VMEM((2, 1, tq), F32),
            ],
            compiler_params=_params(("arbitrary", "arbitrary", "arbitrary")),
            name=name,
        )(qt, kf, vt)

    return lax.cond(logit_bound < FOX_NOSHIFT_BOUND,
                    lambda: call(False, "fox_attn"), lambda: call(True, "fox_attn_online"))


def _gla_consts(r):
    idx = np.arange(r)
    same = (idx[:, None] // GLA_CHUNK) == (idx[None, :] // GLA_CHUNK)
    lblk = (same & (idx[None, :] <= idx[:, None])).astype(np.float32)
    ablk = same.astype(np.float32)
    return jnp.asarray(lblk, BF16), jnp.asarray(ablk, BF16)


def _gla_kernel(q_ref, k_ref, v_ref, r_ref, small_ref, wlh_ref, wll_ref, bg_ref, lblk_ref,
                ablk_ref, gain_ref, o_ref, st_ref, oacc_ref, *, r):
    @pl.when(pl.program_id(1) == 0)
    def _():
        st_ref[...] = jnp.zeros_like(st_ref)

    kw = GLA_HEADS * GLA_DK
    gate = _dot_f32w(small_ref[0], wlh_ref[...], wll_ref[...]) + bg_ref[...]
    log_a = _log_sigmoid(gate) * (1.0 / GLA_GATE_NORM)
    a1, a2, a3 = _split3(log_a)
    lblk = lblk_ref[...]
    ablk = ablk_ref[...]
    bcum = _dot(lblk, a1) + _dot(lblk, a2) + _dot(lblk, a3)
    btot = _dot(ablk, a1) + _dot(ablk, a2) + _dot(ablk, a3)
    q = q_ref[0].astype(F32) * (GLA_DK ** -0.5)
    k = k_ref[0].astype(F32)
    q_dec = (q * jnp.exp(bcum)).astype(BF16)
    k_dec = (k * jnp.exp(-bcum)).astype(BF16)
    k_end_t = (k * jnp.exp(btot - bcum)).T.astype(BF16)
    d_tot_t = jnp.exp(btot).T
    v = v_ref[0]

    row = lax.broadcasted_iota(jnp.int32, (r, r), 0)
    col = lax.broadcasted_iota(jnp.int32, (r, r), 1)
    keep = (_shr(row, GLA_CHUNK) == _shr(col, GLA_CHUNK)) & (col <= row)
    klane = lax.broadcasted_iota(jnp.int32, (1, kw), 1)
    qzero = jnp.zeros_like(q_dec)
    q_heads = []
    for h in range(GLA_HEADS):
        in_h = (klane >= GLA_DK * h) & (klane < GLA_DK * (h + 1))
        q_heads.append(jnp.where(in_h, q_dec, qzero))
        att = _dot_nt(q_heads[h], k_dec)
        att = jnp.where(keep, att, 0.0).astype(BF16)
        vcols = slice(GLA_DV * h, GLA_DV * (h + 1))
        oacc_ref[:, vcols] = _dot(att, v[:, vcols])

    tlane = lax.broadcasted_iota(jnp.int32, (1, r), 1)
    kzero = jnp.zeros_like(k_end_t)
    for c in range(r // GLA_CHUNK):
        rows = slice(GLA_CHUNK * c, GLA_CHUNK * (c + 1))
        st = st_ref[...]
        st_b = st.astype(BF16)
        k_chunk = jnp.where(_shr(tlane, GLA_CHUNK) == c, k_end_t, kzero)
        updates = []
        for h in range(GLA_HEADS):
            vcols = slice(GLA_DV * h, GLA_DV * (h + 1))
            oacc_ref[rows, vcols] += _dot(q_heads[h][rows], st_b)
            updates.append(_dot(k_chunk[GLA_DK * h:GLA_DK * (h + 1), :], v[:, vcols]))
        decay = d_tot_t[:, GLA_CHUNK * c:GLA_CHUNK * c + 1]
        st_ref[...] = decay * st + jnp.concatenate(updates, axis=0)

    gain = gain_ref[...]
    gr = r_ref[0].astype(F32)
    for h in range(GLA_HEADS):
        vcols = slice(GLA_DV * h, GLA_DV * (h + 1))
        o = oacc_ref[:, vcols]
        ms = jnp.mean(o * o, axis=-1, keepdims=True)
        o_ref[0, :, vcols] = (o * lax.rsqrt(ms + EPS) * gain * _silu(gr[:, vcols])).astype(BF16)


def _gla(big3, small3, wl_hi, wl_lo, bgate, gain, *, r=256):
    b, s, _ = big3.shape
    r = min(r, s)
    kw = GLA_HEADS * GLA_DK
    lblk, ablk = _gla_consts(r)
    return pl.pallas_call(
        functools.partial(_gla_kernel, r=r),
        grid=(b, s // r),
        in_specs=[
            pl.BlockSpec((1, r, kw), lambda bi, i: (bi, i, GQ_OFF // kw)),
            pl.BlockSpec((1, r, kw), lambda bi, i: (bi, i, GK_OFF // kw)),
            pl.BlockSpec((1, r, MIX_W), lambda bi, i: (bi, i, GV_OFF // MIX_W)),
            pl.BlockSpec((1, r, MIX_W), lambda bi, i: (bi, i, GR_OFF // MIX_W)),
            pl.BlockSpec((1, r, LANES), lambda bi, i: (bi, i, 0)),
            pl.BlockSpec((LANES, kw), lambda bi, i: (0, 0)),
            pl.BlockSpec((LANES, kw), lambda bi, i: (0, 0)),
            pl.BlockSpec((1, kw), lambda bi, i: (0, 0)),
            pl.BlockSpec((r, r), lambda bi, i: (0, 0)),
            pl.BlockSpec((r, r), lambda bi, i: (0, 0)),
            pl.BlockSpec((1, GLA_DV), lambda bi, i: (0, 0)),
        ],
        out_specs=pl.BlockSpec((1, r, MIX_W), lambda bi, i: (bi, i, 0)),
        out_shape=jax.ShapeDtypeStruct((b, s, MIX_W), BF16),
        scratch_shapes=[pltpu.VMEM((kw, GLA_DV), F32), pltpu.VMEM((r, MIX_W), F32)],
        compiler_params=_params(("arbitrary", "arbitrary")),
        name="gla",
    )(big3, big3, big3, big3, small3, wl_hi, wl_lo, bgate, lblk, ablk, gain)


SSD_HALO = 16
SSD_CONV_W = MIX_W + 2 * SSM_GROUPS * SSM_STATE


def _ssd_consts(r):
    expand = np.zeros((LANES, MIX_W), np.float32)
    for h in range(SSM_HEADS):
        expand[SDT_LANE + h, SSM_HD * h:SSM_HD * (h + 1)] = 1.0
    rep = np.zeros((LANES, MIX_W), np.float32)
    for h in range(SSM_HEADS):
        g = h // (SSM_HEADS // SSM_GROUPS)
        for n in range(SSM_STATE):
            rep[SSM_STATE * g + n, SSM_HD * h + n] = 1.0
    ltri = np.tril(np.ones((r, r), np.float32))
    return (jnp.asarray(expand, BF16), jnp.asarray(rep, BF16), jnp.asarray(ltri, BF16),
            jnp.asarray(ltri.T, BF16))


def _ssd_kernel(z_ref, x_ref, xp_ref, b_ref, bp_ref, c_ref, cp_ref, small_ref, cw_ref, cb_ref,
                dtb_ref, alog_ref, dtbc_ref, alogc_ref, dskip_ref, onorm_ref, expand_ref,
                rep_ref, ltri_ref, utri_ref, o_ref, sw_ref, ext_ref, y_ref, *, r):
    first = pl.program_id(1) == 0

    @pl.when(first)
    def _():
        sw_ref[...] = jnp.zeros_like(sw_ref)

    keep_prev = jnp.where(first, 0.0, 1.0)
    ext_ref[0:SSD_HALO, 0:MIX_W] = xp_ref[0].astype(F32) * keep_prev
    ext_ref[0:SSD_HALO, MIX_W:MIX_W + LANES] = bp_ref[0].astype(F32) * keep_prev
    ext_ref[0:SSD_HALO, MIX_W + LANES:SSD_CONV_W] = cp_ref[0].astype(F32) * keep_prev
    ext_ref[SSD_HALO:, 0:MIX_W] = x_ref[0].astype(F32)
    ext_ref[SSD_HALO:, MIX_W:MIX_W + LANES] = b_ref[0].astype(F32)
    ext_ref[SSD_HALO:, MIX_W + LANES:SSD_CONV_W] = c_ref[0].astype(F32)
    conv = cb_ref[...] + cw_ref[SSM_CONV - 1:SSM_CONV, :] * ext_ref[SSD_HALO:, :]
    for back in range(1, SSM_CONV):
        tap = SSM_CONV - 1 - back
        conv = conv + cw_ref[tap:tap + 1, :] * ext_ref[pl.ds(SSD_HALO - back, r), :]
    xbc = _silu(conv)
    xs = xbc[:, 0:MIX_W]
    bm = xbc[:, MIX_W:MIX_W + LANES].astype(BF16)
    cm = xbc[:, MIX_W + LANES:SSD_CONV_W].astype(BF16)

    sm = small_ref[0]
    dt = _dot3_right(_softplus(sm + dtb_ref[...]), expand_ref[...])
    a_neg = -jnp.exp(alog_ref[...])
    acs = _dot3_left(ltri_ref[...], dt * a_neg)
    acs_last = acs[r - 1:r, :]
    sm_t = sm.T
    dt_t = _softplus(sm_t[SDT_LANE:SDT_LANE + SSM_HEADS, :] + dtbc_ref[:, 0:1])
    acs_t = _dot3_right(dt_t * (-jnp.exp(alogc_ref[:, 0:1])), utri_ref[...])

    xdt = (xs * dt).astype(BF16)
    row = lax.broadcasted_iota(jnp.int32, (r, r), 0)
    col = lax.broadcasted_iota(jnp.int32, (r, r), 1)
    causal = col <= row
    glane = lax.broadcasted_iota(jnp.int32, (1, LANES), 1)
    first_half = glane < SSM_STATE
    czero = jnp.zeros_like(cm)
    hpg = SSM_HEADS // SSM_GROUPS
    for g in range(SSM_GROUPS):
        cg = jnp.where(first_half if g == 0 else ~first_half, cm, czero)
        cb_g = _dot_nt(cg, bm)
        for pair in range(hpg // 2):
            p = g * (hpg // 2) + pair
            xp = xdt[:, LANES * p:LANES * (p + 1)]
            xzero = jnp.zeros_like(xp)
            acc = None
            for a in range(2):
                h = 2 * p + a
                dmat = acs[:, SSM_HD * h:SSM_HD * h + 1] - acs_t[h:h + 1, :]
                sc = (cb_g * jnp.exp(jnp.where(causal, dmat, NEG_BIG))).astype(BF16)
                xh = jnp.where(first_half if a == 0 else ~first_half, xp, xzero)
                contrib = _dot(sc, xh)
                acc = contrib if acc is None else acc + contrib
            y_ref[:, LANES * p:LANES * (p + 1)] = acc

    rep = rep_ref[...]
    sw = sw_ref[...]
    cw = (_dot(cm, rep) * jnp.exp(acs)).astype(BF16)
    y = y_ref[...] + _dot(cw, sw.astype(BF16)) + dskip_ref[...] * xs
    bw = (_dot(bm, rep) * jnp.exp(acs_last - acs)).astype(BF16)
    upd = _dot_tn(bw, xdt)
    srow = lax.broadcasted_iota(jnp.int32, (MIX_W, MIX_W), 0)
    scol = lax.broadcasted_iota(jnp.int32, (MIX_W, MIX_W), 1)
    same_head = _shr(srow, SSM_STATE) == _shr(scol, SSM_HD)
    sw_ref[...] = sw * jnp.exp(acs_last) + jnp.where(same_head, upd, 0.0)

    y = y * _silu(z_ref[0].astype(F32))
    gw = MIX_W // SSM_GROUPS
    for g in range(SSM_GROUPS):
        cols = slice(gw * g, gw * (g + 1))
        yg = y[:, cols]
        ms = jnp.mean(yg * yg, axis=-1, keepdims=True)
        o_ref[0, :, cols] = (yg * lax.rsqrt(ms + EPS) * onorm_ref[:, cols]).astype(BF16)


def _ssd(big3, small3, conv_w, conv_b, dtb_w, alog_w, dtb_c, alog_c, dskip_w, onorm, *, r=256):
    b, s, _ = big3.shape
    r = min(r, s)
    consts = _ssd_consts(r)
    hb = r // SSD_HALO

    def cur(width, off):
        return pl.BlockSpec((1, r, width), lambda bi, i: (bi, i, off // width))

    def prev(width, off):
        return pl.BlockSpec((1, SSD_HALO, width),
                            lambda bi, i: (bi, jnp.maximum(i * hb - 1, 0), off // width))

    def whole(shape):
        return pl.BlockSpec(shape, lambda bi, i: (0,) * len(shape))

    return pl.pallas_call(
        functools.partial(_ssd_kernel, r=r),
        grid=(b, s // r),
        in_specs=[
            cur(MIX_W, SZ_OFF),
            cur(MIX_W, SX_OFF), prev(MIX_W, SX_OFF),
            cur(LANES, SB_OFF), prev(LANES, SB_OFF),
            cur(LANES, SC_OFF), prev(LANES, SC_OFF),
            pl.BlockSpec((1, r, LANES), lambda bi, i: (bi, i, 0)),
            whole((SSM_CONV, SSD_CONV_W)), whole((1, SSD_CONV_W)),
            whole((1, LANES)), whole((1, MIX_W)),
            whole((SSM_HEADS, LANES)), whole((SSM_HEADS, LANES)),
            whole((1, MIX_W)), whole((1, MIX_W)),
            whole((LANES, MIX_W)), whole((LANES, MIX_W)), whole((r, r)), whole((r, r)),
        ],
        out_specs=pl.BlockSpec((1, r, MIX_W), lambda bi, i: (bi, i, 0)),
        out_shape=jax.ShapeDtypeStruct((b, s, MIX_W), BF16),
        scratch_shapes=[
            pltpu.VMEM((MIX_W, MIX_W), F32),
            pltpu.VMEM((r + SSD_HALO, SSD_CONV_W), F32),
            pltpu.VMEM((r, MIX_W), F32),
        ],
        compiler_params=_params(("arbitrary", "arbitrary")),
        name="ssd",
    )(big3, big3, big3, big3, big3, big3, big3, small3, conv_w, conv_b, dtb_w, alog_w,
      dtb_c, alog_c, dskip_w, onorm, *consts)


def _merge_kernel(oa_ref, ob_ref, oc_ref, gate_ref, x_ref, wb_ref, bgate_ref, wo_ref, nffn_ref,
                  wrh_ref, wrl_ref, br_ref, ustrict_ref, xe_ref, route_ref, counts_ref, *, tm):
    @pl.when(pl.program_id(0) == 0)
    def _():
        counts_ref[...] = jnp.zeros_like(counts_ref)

    mixed = None
    for ridx, o_ref in enumerate((oa_ref, ob_ref, oc_ref)):
        cols = slice(D_MODEL * ridx, D_MODEL * (ridx + 1))
        gate = _sigmoid(gate_ref[:, cols].astype(F32) + bgate_ref[ridx:ridx + 1, :])
        term = gate * _dot(o_ref[...], wb_ref[ridx])
        mixed = term if mixed is None else mixed + term
    xn = x_ref[...] + _dot(mixed.astype(BF16), wo_ref[...])
    xe_ref[:, 0:D_MODEL] = xn
    ms = jnp.mean(xn * xn, axis=-1, keepdims=True)
    h = xn * lax.rsqrt(ms + EPS) * nffn_ref[...]

    lt = (_dot_f32w(h, wrh_ref[...], wrl_ref[...]) + br_ref[...]).T
    grow = lax.broadcasted_iota(jnp.int32, (8, tm), 0)
    grow_f = grow.astype(F32)
    far_row = float(LANES)
    gl = jnp.where(grow < N_EGROUPS, lt[RG_LANE:RG_LANE + 8, :], NEG_BIG)
    gmax = jnp.max(gl, axis=0, keepdims=True)
    g_w = 1.0 / jnp.sum(jnp.exp(gl - gmax), axis=0, keepdims=True)
    g_sel = jnp.min(jnp.where(gl == gmax, grow_f, far_row), axis=0, keepdims=True)
    e16 = lt[RE_LANE:RE_LANE + N_EXPERTS, :]
    erow = lax.broadcasted_iota(jnp.int32, (N_EXPERTS, tm), 0)
    erow_f = erow.astype(F32)
    in_grp = _shr(erow, EXP_PER_GROUP).astype(F32) == g_sel
    el = jnp.where(in_grp, e16, NEG_BIG)
    e1 = jnp.max(el, axis=0, keepdims=True)
    i1 = jnp.min(jnp.where(in_grp & (el == e1), erow_f, far_row), axis=0, keepdims=True)
    rest = in_grp & (erow_f != i1)
    el2 = jnp.where(rest, e16, NEG_BIG)
    e2 = jnp.max(el2, axis=0, keepdims=True)
    i2 = jnp.min(jnp.where(rest & (el2 == e2), erow_f, far_row), axis=0, keepdims=True)
    ratio = jnp.exp(e2 - e1)
    w1 = g_w / (1.0 + ratio)
    w2 = w1 * ratio
    comb_t = jnp.where(erow_f == i1, w1, 0.0) + jnp.where(erow_f == i2, w2, 0.0)
    comb_t = jnp.concatenate([jnp.zeros((RE_LANE, tm), F32), comb_t,
                              jnp.zeros((LANES - RE_LANE - N_EXPERTS, tm), F32)], axis=0)
    xe_ref[:, D_MODEL:] = comb_t.T

    first_row = EXP_PER_GROUP * g_sel
    lo = jnp.minimum(i1, i2) - first_row
    hi = jnp.maximum(i1, i2) - first_row
    cls = PAIRS_PER_GROUP * g_sel + lo * (7.0 - lo) * 0.5 + (hi - lo - 1.0)
    crow_f = lax.broadcasted_iota(jnp.int32, (ROUTE_ROWS, tm), 0).astype(F32)
    is_cls = crow_f == cls
    onehot = jnp.where(is_cls, 1.0, 0.0)
    before = _dot(onehot.astype(BF16), ustrict_ref[...]) + counts_ref[:, 0:1]
    rank = jnp.sum(jnp.where(is_cls, before, 0.0), axis=0, keepdims=True)
    counts_ref[...] = counts_ref[...] + jnp.sum(onehot, axis=1, keepdims=True)
    rank_hi = jnp.floor(rank * (1.0 / LANES))
    rank_lo = rank - rank_hi * LANES
    route_ref[...] = jnp.where(grow == 0, cls, jnp.where(grow == 1, rank_hi,
                                                         jnp.where(grow == 2, rank_lo, 0.0)))


def _merge(oa, ob, oc, big, x2, wb, bgate, wo, nffn, wr_hi, wr_lo, br, *, tm=512):
    t = x2.shape[0]
    tm = min(tm, t)
    ustrict = jnp.asarray(np.triu(np.ones((tm, tm), np.float32), 1), BF16)

    def whole(shape):
        return pl.BlockSpec(shape, lambda i: (0,) * len(shape))

    return pl.pallas_call(
        functools.partial(_merge_kernel, tm=tm),
        grid=(t // tm,),
        in_specs=[
            pl.BlockSpec((tm, MIX_W), lambda i: (i, 0)),
            pl.BlockSpec((tm, MIX_W), lambda i: (i, 0)),
            pl.BlockSpec((tm, MIX_W), lambda i: (i, 0)),
            pl.BlockSpec((tm, 3 * D_MODEL), lambda i: (i, 0)),
            pl.BlockSpec((tm, D_MODEL), lambda i: (i, 0)),
            whole((3, MIX_W, D_MODEL)), whole((3, D_MODEL)), whole((D_MODEL, D_MODEL)),
            whole((1, D_MODEL)), whole((D_MODEL, LANES)), whole((D_MODEL, LANES)),
            whole((1, LANES)), whole((tm, tm)),
        ],
        out_specs=[
            pl.BlockSpec((tm, XE_COLS), lambda i: (i, 0)),
            pl.BlockSpec((8, tm), lambda i: (0, i)),
            pl.BlockSpec((ROUTE_ROWS, LANES), lambda i: (0, 0)),
        ],
        out_shape=[
            jax.ShapeDtypeStruct((t, XE_COLS), F32),
            jax.ShapeDtypeStruct((8, t), F32),
            jax.ShapeDtypeStruct((ROUTE_ROWS, LANES), F32),
        ],
        compiler_params=_params(("arbitrary",)),
        name="merge",
    )(oa, ob, oc, big, x2, wb, bgate, wo, nffn, wr_hi, wr_lo, br, ustrict)


MOE_TILE = 256
ROW_DMA_TILE = 512


def _row_dma(pos_ref, near_ref, far_ref, sem, r, to_far):
    near, far = near_ref.at[pl.ds(r, 1)], far_ref.at[pl.ds(pos_ref[0, 0, r], 1)]
    return pltpu.make_async_copy(near, far, sem) if to_far else pltpu.make_async_copy(far, near, sem)


def _row_burst(pos_ref, near_ref, far_ref, sem, tm, to_far):
    for r in range(tm):
        _row_dma(pos_ref, near_ref, far_ref, sem, r, to_far).start(priority=r % 2)

    def drain(r, carry):
        _row_dma(pos_ref, near_ref, far_ref, sem, r, to_far).wait()
        return carry

    lax.fori_loop(0, tm, drain, 0, unroll=8)


def _row_gather_kernel(pos_ref, src_ref, o_ref, sem, *, tm):
    _row_burst(pos_ref, o_ref, src_ref, sem, tm, to_far=False)


def _row_scatter_kernel(pad_ref, pos_ref, src_ref, dst_ref, zeros_ref, sem, *, tm, pad_rows):
    @pl.when(pl.program_id(0) == 0)
    def _():
        zeros_ref[...] = jnp.zeros_like(zeros_ref)
        n_cls = pad_ref.shape[0] - 1

        def blank(c):
            start = pl.multiple_of(pad_ref[c], 8)
            return pltpu.make_async_copy(zeros_ref, dst_ref.at[pl.ds(start, pad_rows)], sem)

        for c in range(n_cls):
            blank(c).start()
        for c in range(n_cls):
            blank(c).wait()

        tile = pad_rows - 8

        def blank_tile(j, carry):
            start = pl.multiple_of(j * tile, tile)
            copy = pltpu.make_async_copy(zeros_ref.at[pl.ds(0, tile)], dst_ref.at[pl.ds(start, tile)], sem)
            copy.start()
            copy.wait()
            return carry

        lax.fori_loop(pad_ref[n_cls] // tile, dst_ref.shape[0] // tile, blank_tile, 0)

    _row_burst(pos_ref, src_ref, dst_ref, sem, tm, to_far=True)


def _row_gather(pos, src, out_rows):
    tm = min(ROW_DMA_TILE, out_rows)
    width = src.shape[1]
    return pl.pallas_call(
        functools.partial(_row_gather_kernel, tm=tm),
        grid=(out_rows // tm,),
        in_specs=[pl.BlockSpec((1, 1, tm), lambda i: (i, 0, 0), memory_space=pltpu.SMEM),
                  pl.BlockSpec(memory_space=pl.ANY)],
        out_specs=pl.BlockSpec((tm, width), lambda i: (i, 0)),
        out_shape=jax.ShapeDtypeStruct((out_rows, width), src.dtype),
        scratch_shapes=[pltpu.SemaphoreType.DMA(())],
        compiler_params=_params(("arbitrary",)),
        name="moe_gather",
    )(pos.reshape(out_rows // tm, 1, tm), src)


def _row_scatter(pos, pad_start, src, out_rows, pad_rows):
    t, width = src.shape
    tm = min(ROW_DMA_TILE, t)
    grid_spec = pltpu.PrefetchScalarGridSpec(
        num_scalar_prefetch=1,
        grid=(t // tm,),
        in_specs=[pl.BlockSpec((1, 1, tm), lambda i, pad: (i, 0, 0), memory_space=pltpu.SMEM),
                  pl.BlockSpec((tm, width), lambda i, pad: (i, 0))],
        out_specs=pl.BlockSpec(memory_space=pl.ANY),
        scratch_shapes=[pltpu.VMEM((pad_rows, width), src.dtype), pltpu.SemaphoreType.DMA(())],
    )
    return pl.pallas_call(
        functools.partial(_row_scatter_kernel, tm=tm, pad_rows=pad_rows),
        grid_spec=grid_spec,
        out_shape=jax.ShapeDtypeStruct((out_rows, width), src.dtype),
        compiler_params=_params(("arbitrary",)),
        name="moe_scatter",
    )(pad_start, pos.reshape(t // tm, 1, tm), src)


def _moe_sorted_kernel(ea_ref, eb_ref, nused_ref, xs_ref, nffn_ref, wga_ref, wua_ref, wda_ref,
                       wgb_ref, wub_ref, wdb_ref, o_ref, *, tm):
    i = pl.program_id(0)

    @pl.when(i >= nused_ref[0])
    def _():
        o_ref[...] = jnp.zeros_like(o_ref)


    @pl.when(i < nused_ref[0])
    def _():
        xn = xs_ref[:, 0:D_MODEL]
        comb = xs_ref[:, D_MODEL:]
        ms = jnp.mean(xn * xn, axis=-1, keepdims=True)
        h = (xn * lax.rsqrt(ms + EPS) * nffn_ref[...]).astype(BF16)
        lane = lax.broadcasted_iota(jnp.int32, (tm, LANES), 1)
        out = xn
        for e_ref, wg_ref, wu_ref, wd_ref in ((ea_ref, wga_ref, wua_ref, wda_ref),
                                              (eb_ref, wgb_ref, wub_ref, wdb_ref)):
            w = jnp.sum(jnp.where(lane == e_ref[i] + RE_LANE, comb, 0.0), axis=-1, keepdims=True)
            hid = _silu(_dot(h, wg_ref[0])) * _dot(h, wu_ref[0])
            out = out + w * _dot(hid.astype(BF16), wd_ref[0])
        o_ref[...] = out


def _moe_sorted(ea, eb, nused, xs, nffn, wg, wu, wd, *, tm):
    n_tiles = xs.shape[0] // tm

    def w_in(which):
        return pl.BlockSpec((1, D_MODEL, D_EXPERT), lambda i, ea, eb, nu: ((ea, eb)[which][i], 0, 0))

    def w_out(which):
        return pl.BlockSpec((1, D_EXPERT, D_MODEL), lambda i, ea, eb, nu: ((ea, eb)[which][i], 0, 0))

    grid_spec = pltpu.PrefetchScalarGridSpec(
        num_scalar_prefetch=3,
        grid=(n_tiles,),
        in_specs=[
            pl.BlockSpec((tm, XE_COLS), lambda i, ea, eb, nu: (jnp.minimum(i, nu[0] - 1), 0)),
            pl.BlockSpec((1, D_MODEL), lambda i, ea, eb, nu: (0, 0)),
            w_in(0), w_in(0), w_out(0), w_in(1), w_in(1), w_out(1),
        ],
        out_specs=pl.BlockSpec((tm, D_MODEL), lambda i, ea, eb, nu: (i, 0)),
    )
    return pl.pallas_call(
        functools.partial(_moe_sorted_kernel, tm=tm),
        grid_spec=grid_spec,
        out_shape=jax.ShapeDtypeStruct((n_tiles * tm, D_MODEL), F32),
        compiler_params=_params(("arbitrary",)),
        name="moe",
    )(ea, eb, nused, xs, nffn, wg, wu, wd, wg, wu, wd)


_PAIR_LO = np.array([0, 0, 0, 1, 1, 2], np.int32)
_PAIR_HI = np.array([1, 2, 3, 2, 3, 3], np.int32)


def _moe(xe, route, counts, nffn, wg, wu, wd):
    t = xe.shape[0]
    tm = min(MOE_TILE, t)
    n_cls = N_EGROUPS * PAIRS_PER_GROUP
    n_tiles = t // tm + n_cls
    cnt = counts[:n_cls, 0].astype(jnp.int32)
    tiles = (cnt + tm - 1) // tm
    tile_end = jnp.cumsum(tiles)
    n_used = tile_end[-1]
    cls_base = (tile_end - tiles) * tm
    tile_idx = jnp.minimum(jnp.arange(n_tiles), n_used - 1)
    tile_cls = jnp.sum((tile_end[None, :] <= tile_idx[:, None]).astype(jnp.int32), axis=1)
    grp, pair = tile_cls // PAIRS_PER_GROUP, tile_cls % PAIRS_PER_GROUP
    ea = EXP_PER_GROUP * grp + jnp.asarray(_PAIR_LO)[pair]
    eb = EXP_PER_GROUP * grp + jnp.asarray(_PAIR_HI)[pair]
    rank = (route[1] * LANES + route[2]).astype(jnp.int32)
    pos = cls_base[route[0].astype(jnp.int32)] + rank

    pad_start = jnp.minimum((cls_base + cnt) // 8 * 8, n_tiles * tm - (tm + 8))
    pad_start = jnp.concatenate([pad_start, (n_used * tm).reshape(1)])
    xs = _row_scatter(pos, pad_start, xe, n_tiles * tm, tm + 8)
    ys = _moe_sorted(ea, eb, n_used.reshape(1), xs, nffn, wg, wu, wd, tm=tm)
    return _row_gather(pos, ys, t)


def _cols(w, off, width):
    return w[:, off:off + width]


def _pad_lanes(v, lane0, width=LANES):
    out = jnp.zeros((1, width), F32)
    return out.at[0, lane0:lane0 + v.shape[0]].set(v.astype(F32))


def _layer(x2, b, s, norm_mix, w_in, fox_f_bias, fox_q_norm, fox_k_norm, gla_w_lr, gla_b_gate,
           gla_out_norm, ssm_conv_w, ssm_conv_b, ssm_dt_bias, ssm_a_log, ssm_d, ssm_out_norm,
           w_branch, b_branch_gate, w_out, norm_ffn, w_router_grp, b_router_grp,
           w_router_exp, b_router_exp, w_exp_gate, w_exp_up, w_exp_down):
    t = b * s
    w_big = jnp.concatenate([
        _cols(w_in, _O_GATE, 3 * D_MODEL), _cols(w_in, _O_FQ, MIX_W), _cols(w_in, _O_FK, MIX_W),
        _cols(w_in, _O_FV, MIX_W), _cols(w_in, _O_GV, MIX_W), _cols(w_in, _O_GR, MIX_W),
        _cols(w_in, _O_SZ, MIX_W), _cols(w_in, _O_SX, MIX_W), _cols(w_in, _O_GQ, 256),
        _cols(w_in, _O_GK, 256), _cols(w_in, _O_SB, LANES), _cols(w_in, _O_SC, LANES),
    ], axis=1).astype(BF16)
    w_small = jnp.concatenate([
        _cols(w_in, _O_FF, FOX_HEADS), _cols(w_in, _O_GLR, GLA_RANK), _cols(w_in, _O_SDT, SSM_HEADS),
        jnp.zeros((D_MODEL, LANES - FOX_HEADS - GLA_RANK - SSM_HEADS), F32),
    ], axis=1)
    ws_hi, ws_lo = _split2(w_small)

    big, small = _inproj(x2, norm_mix.reshape(1, D_MODEL), w_big, ws_hi, ws_lo)
    big3 = big.reshape(b, s, BIG_COLS)
    small3 = small.reshape(b, s, LANES)

    qt, kf, vt = _fox_prep(big3, small3, _pad_lanes(fox_f_bias, FF_LANE),
                           jnp.tile(fox_q_norm, FOX_HEADS).reshape(1, MIX_W),
                           jnp.tile(fox_k_norm, FOX_HEADS).reshape(1, MIX_W), ts=min(FOX_TILE, s))
    logit_bound = ((FOX_HD ** 0.5) * 1.01 * jnp.max(jnp.abs(fox_q_norm))
                   * jnp.max(jnp.abs(fox_k_norm)))
    o_a = _fox_attn(qt, kf, vt, logit_bound)

    wl = jnp.zeros((LANES, GLA_HEADS * GLA_DK), F32).at[GLR_LANE:GLR_LANE + GLA_RANK].set(gla_w_lr)
    wl_hi, wl_lo = _split2(wl)
    o_b = _gla(big3, small3, wl_hi, wl_lo, gla_b_gate.reshape(1, -1),
               gla_out_norm.reshape(1, GLA_DV))

    o_c = _ssd(big3, small3, ssm_conv_w, ssm_conv_b.reshape(1, -1),
               _pad_lanes(ssm_dt_bias, SDT_LANE),
               jnp.repeat(ssm_a_log, SSM_HD).reshape(1, MIX_W),
               jnp.broadcast_to(ssm_dt_bias[:, None], (SSM_HEADS, LANES)),
               jnp.broadcast_to(ssm_a_log[:, None], (SSM_HEADS, LANES)),
               jnp.repeat(ssm_d, SSM_HD).reshape(1, MIX_W),
               ssm_out_norm.reshape(1, MIX_W))

    w_r = jnp.concatenate([w_router_grp, jnp.zeros((D_MODEL, RE_LANE - N_EGROUPS), F32), w_router_exp,
                           jnp.zeros((D_MODEL, LANES - RE_LANE - N_EXPERTS), F32)], axis=1)
    wr_hi, wr_lo = _split2(w_r)
    b_r = jnp.concatenate([b_router_grp, jnp.zeros((RE_LANE - N_EGROUPS,), F32), b_router_exp,
                           jnp.zeros((LANES - RE_LANE - N_EXPERTS,), F32)]).reshape(1, LANES)
    nffn = norm_ffn.reshape(1, D_MODEL)
    xe, route, counts = _merge(o_a.reshape(t, MIX_W), o_b.reshape(t, MIX_W), o_c.reshape(t, MIX_W),
                               big, x2, w_branch.astype(BF16), b_branch_gate, w_out.astype(BF16),
                               nffn, wr_hi, wr_lo, b_r)

    return _moe(xe, route, counts, nffn, w_exp_gate.astype(BF16), w_exp_up.astype(BF16),
                w_exp_down.astype(BF16))


def kernel(x, norm_mix, w_in, fox_f_bias, fox_q_norm, fox_k_norm, gla_w_lr, gla_b_gate, gla_out_norm, ssm_conv_w, ssm_conv_b, ssm_dt_bias, ssm_a_log, ssm_d, ssm_out_norm, w_branch, b_branch_gate, w_out, norm_ffn, w_router_grp, b_router_grp, w_router_exp, b_router_exp, w_exp_gate, w_exp_up, w_exp_down):
    b, s, d = x.shape
    x2 = x.reshape(b * s, d)
    per_layer = (norm_mix, w_in, fox_f_bias, fox_q_norm, fox_k_norm, gla_w_lr, gla_b_gate,
                 gla_out_norm, ssm_conv_w, ssm_conv_b, ssm_dt_bias, ssm_a_log, ssm_d,
                 ssm_out_norm, w_branch, b_branch_gate, w_out, norm_ffn, w_router_grp,
                 b_router_grp, w_router_exp, b_router_exp, w_exp_gate, w_exp_up, w_exp_down)
    for l in range(norm_mix.shape[0]):
        x2 = _layer(x2, b, s, *[p[l] for p in per_layer])
    return x2.reshape(b, s, d)
```

```python
import functools

import numpy as np
import jax
import jax.numpy as jnp
from jax import lax
from jax.experimental import pallas as pl
from jax.experimental.pallas import tpu as pltpu

F32 = jnp.float32
BF16 = jnp.bfloat16

D_MODEL = 1024
MIX_W = 512
EPS = 1e-6
FOX_HEADS = 8
FOX_HD = 64
FOX_PAIRS = FOX_HEADS // 2
GLA_HEADS = 4
GLA_DK = 64
GLA_DV = 128
GLA_RANK = 16
GLA_GATE_NORM = 16.0
GLA_CHUNK = 64
SSM_HEADS = 8
SSM_HD = 64
SSM_GROUPS = 2
SSM_STATE = 64
SSM_CONV = 4
N_EGROUPS = 4
EXP_PER_GROUP = 4
N_EXPERTS = 16
PAIRS_PER_GROUP = 6
D_EXPERT = 512

LANES = 128
NEG_BIG = -1e30
LOG2E = 1.4426950408889634
VMEM_LIMIT = 56 * 1024 * 1024

GATE_OFF, FQ_OFF, FK_OFF, FV_OFF = 0, 3072, 3584, 4096
GV_OFF, GR_OFF, SZ_OFF, SX_OFF = 4608, 5120, 5632, 6144
GQ_OFF, GK_OFF, SB_OFF, SC_OFF = 6656, 6912, 7168, 7296
BIG_COLS = 7424
FF_LANE, GLR_LANE, SDT_LANE = 0, 8, 24
_O_FQ, _O_FK, _O_FV, _O_FF = 0, 512, 1024, 1536
_O_GQ, _O_GK, _O_GV, _O_GR, _O_GLR = 1544, 1800, 2056, 2568, 3080
_O_SZ, _O_SX, _O_SB, _O_SC, _O_SDT, _O_GATE = 3096, 3608, 4120, 4248, 4376, 4384
RG_LANE, RE_LANE = 0, 8
ROUTE_ROWS = 32
XE_COLS = D_MODEL + LANES


def _split2(x):
    hi = x.astype(BF16)
    lo = (x - hi.astype(F32)).astype(BF16)
    return hi, lo


def _split3(x):
    x1 = x.astype(BF16)
    r = x - x1.astype(F32)
    x2 = r.astype(BF16)
    x3 = (r - x2.astype(F32)).astype(BF16)
    return x1, x2, x3


def _dot(a, b):
    return jnp.dot(a, b, preferred_element_type=F32)


def _dot_nt(a, b):
    return lax.dot_general(a, b, (((1,), (1,)), ((), ())), preferred_element_type=F32)


def _dot_tn(a, b):
    return lax.dot_general(a, b, (((0,), (0,)), ((), ())), preferred_element_type=F32)


def _dot3_left(m_bf16, x_f32):
    x1, x2, x3 = _split3(x_f32)
    return _dot(m_bf16, x1) + _dot(m_bf16, x2) + _dot(m_bf16, x3)


def _dot3_right(x_f32, m_bf16):
    x1, x2, x3 = _split3(x_f32)
    return _dot(x1, m_bf16) + _dot(x2, m_bf16) + _dot(x3, m_bf16)


def _dot_f32w(x_f32, w_hi, w_lo):
    x_hi, x_lo = _split2(x_f32)
    return _dot(x_hi, w_hi) + _dot(x_lo, w_hi) + _dot(x_hi, w_lo)


def _shr(x, pow2):
    return jnp.right_shift(x, pow2.bit_length() - 1)


def _log_sigmoid(x):
    return jnp.minimum(x, 0.0) - jnp.log(1.0 + jnp.exp(-jnp.abs(x)))


def _softplus(x):
    return jnp.maximum(x, 0.0) + jnp.log(1.0 + jnp.exp(-jnp.abs(x)))


def _sigmoid(x):
    return 0.5 * jnp.tanh(0.5 * x) + 0.5


def _silu(x):
    return x * _sigmoid(x)


def _params(sem):
    return pltpu.CompilerParams(dimension_semantics=sem, vmem_limit_bytes=VMEM_LIMIT)


def _inproj_kernel(x_ref, g_ref, w_ref, wsh_ref, wsl_ref, big_ref, small_ref, *, tn):
    x = x_ref[...]
    ms = jnp.mean(x * x, axis=-1, keepdims=True)
    h = x * lax.rsqrt(ms + EPS) * g_ref[...]
    hb = h.astype(BF16)
    for c in range(BIG_COLS // tn):
        cols = slice(c * tn, (c + 1) * tn)
        big_ref[:, cols] = _dot(hb, w_ref[:, cols]).astype(BF16)
    h_lo = (h - hb.astype(F32)).astype(BF16)
    wsh = wsh_ref[...]
    small_ref[...] = _dot(hb, wsh) + _dot(h_lo, wsh) + _dot(hb, wsl_ref[...])


def _inproj(x2, gain, w_big, ws_hi, ws_lo, *, tm=512, tn=256):
    t = x2.shape[0]
    return pl.pallas_call(
        functools.partial(_inproj_kernel, tn=tn),
        grid=(t // tm,),
        in_specs=[
            pl.BlockSpec((tm, D_MODEL), lambda i: (i, 0)),
            pl.BlockSpec((1, D_MODEL), lambda i: (0, 0)),
            pl.BlockSpec((D_MODEL, BIG_COLS), lambda i: (0, 0), pipeline_mode=pl.Buffered(1)),
            pl.BlockSpec((D_MODEL, LANES), lambda i: (0, 0)),
            pl.BlockSpec((D_MODEL, LANES), lambda i: (0, 0)),
        ],
        out_specs=[
            pl.BlockSpec((tm, BIG_COLS), lambda i: (i, 0)),
            pl.BlockSpec((tm, LANES), lambda i: (i, 0)),
        ],
        out_shape=[
            jax.ShapeDtypeStruct((t, BIG_COLS), BF16),
            jax.ShapeDtypeStruct((t, LANES), F32),
        ],
        compiler_params=_params(("arbitrary",)),
        name="inproj",
    )(x2, gain, w_big, ws_hi, ws_lo)


def _fox_consts(ts):
    ltri = np.tril(np.ones((ts, ts), np.float32))
    hsum = np.kron(np.eye(HSUM_W // FOX_HD, dtype=np.float32), np.ones((FOX_HD, FOX_HD), np.float32))
    sq = np.zeros((3, LANES, MIX_W), np.float32)
    sk = np.zeros((3, LANES, MIX_W), np.float32)
    oneq = np.zeros((1, MIX_W), np.float32)
    onek = np.zeros((1, MIX_W), np.float32)
    for h in range(FOX_HEADS):
        base = LANES * (h // 2) + 6 * (h % 2)
        for j in range(3):
            sq[j, FF_LANE + h, base + j] = 1.0
            sk[j, FF_LANE + h, base + 3 + j] = -1.0
            oneq[0, base + 3 + j] = 1.0
            onek[0, base + j] = 1.0
    saug = np.concatenate([sq.reshape(3 * LANES, MIX_W), sk.reshape(3 * LANES, MIX_W)], axis=1)
    return (jnp.asarray(ltri, BF16), jnp.asarray(hsum, BF16), jnp.asarray(saug, BF16),
            jnp.asarray(oneq), jnp.asarray(onek))


def _fox_prep_kernel(fq_ref, fk_ref, fv_ref, small_ref, fbias_ref, qg_ref, kg_ref, ltri_ref,
                     hsum_ref, saug_ref, oneq_ref, onek_ref, qt_ref, kf_ref, vt_ref, carry_ref,
                     *, ts):
    @pl.when(pl.program_id(1) == 0)
    def _():
        carry_ref[...] = jnp.zeros_like(carry_ref)

    lane = lax.broadcasted_iota(jnp.int32, (ts, LANES), 1)
    f = small_ref[0] + fbias_ref[...]
    ls = jnp.where(lane < FOX_HEADS, _log_sigmoid(f) * LOG2E, 0.0)
    parts = _dot(ltri_ref[...], jnp.concatenate(_split3(ls), axis=1))
    c = parts[:, 0:LANES] + parts[:, LANES:2 * LANES] + parts[:, 2 * LANES:] + carry_ref[0:1, :]
    carry_ref[...] = jnp.broadcast_to(c[ts - 1:ts, :], carry_ref.shape)
    aug = _dot(jnp.concatenate(_split3(c), axis=1), saug_ref[...])
    qaug = aug[:, 0:MIX_W] + oneq_ref[...]
    kaug = aug[:, MIX_W:] + onek_ref[...]

    hsum = hsum_ref[...]

    def head_norm(xb, gain):
        x = xb.astype(F32)
        s_hi, s_lo = _split2(x * x)
        halves = []
        for half in range(MIX_W // HSUM_W):
            cols = slice(HSUM_W * half, HSUM_W * (half + 1))
            halves.append(_dot(s_hi[:, cols], hsum) + _dot(s_lo[:, cols], hsum))
        ss = jnp.concatenate(halves, axis=1)
        return x * lax.rsqrt(ss * (1.0 / FOX_HD) + EPS) * gain

    qn = head_norm(fq_ref[0], qg_ref[...]) * (FOX_HD ** -0.5 * LOG2E)
    kn = head_norm(fk_ref[0], kg_ref[...])
    v = fv_ref[0].astype(F32)
    for p in range(FOX_PAIRS):
        src = slice(LANES * p, LANES * (p + 1))
        dst_x = slice(2 * LANES * p, 2 * LANES * p + LANES)
        dst_a = slice(2 * LANES * p + LANES, 2 * LANES * (p + 1))
        kf_ref[0, :, dst_x] = kn[:, src].astype(BF16)
        kf_ref[0, :, dst_a] = kaug[:, src].astype(BF16)
        qt_ref[0, p, 0, 0:LANES, :] = qn[:, src].T.astype(BF16)
        qt_ref[0, p, 0, LANES:2 * LANES, :] = qaug[:, src].T.astype(BF16)
        vt_ref[0, p, 0] = v[:, src].T.astype(BF16)


def _fox_prep(big3, small3, fbias, qgain, kgain, *, ts):
    b, s, _ = big3.shape
    consts = _fox_consts(ts)
    const_specs = [
        pl.BlockSpec((ts, ts), lambda bi, i: (0, 0)),
        pl.BlockSpec((HSUM_W, HSUM_W), lambda bi, i: (0, 0)),
        pl.BlockSpec((3 * LANES, 2 * MIX_W), lambda bi, i: (0, 0)),
        pl.BlockSpec((1, MIX_W), lambda bi, i: (0, 0)),
        pl.BlockSpec((1, MIX_W), lambda bi, i: (0, 0)),
    ]
    return pl.pallas_call(
        functools.partial(_fox_prep_kernel, ts=ts),
        grid=(b, s // ts),
        in_specs=[
            pl.BlockSpec((1, ts, MIX_W), lambda bi, i: (bi, i, FQ_OFF // MIX_W)),
            pl.BlockSpec((1, ts, MIX_W), lambda bi, i: (bi, i, FK_OFF // MIX_W)),
            pl.BlockSpec((1, ts, MIX_W), lambda bi, i: (bi, i, FV_OFF // MIX_W)),
            pl.BlockSpec((1, ts, LANES), lambda bi, i: (bi, i, 0)),
            pl.BlockSpec((1, LANES), lambda bi, i: (0, 0)),
            pl.BlockSpec((1, MIX_W), lambda bi, i: (0, 0)),
            pl.BlockSpec((1, MIX_W), lambda bi, i: (0, 0)),
        ] + const_specs,
        out_specs=[
            pl.BlockSpec((1, FOX_PAIRS, 1, 2 * LANES, ts), lambda bi, i: (bi, 0, i, 0, 0)),
            pl.BlockSpec((1, ts, 2 * MIX_W), lambda bi, i: (bi, i, 0)),
            pl.BlockSpec((1, FOX_PAIRS, 1, LANES, ts), lambda bi, i: (bi, 0, i, 0, 0)),
        ],
        out_shape=[
            jax.ShapeDtypeStruct((b, FOX_PAIRS, s // ts, 2 * LANES, ts), BF16),
            jax.ShapeDtypeStruct((b, s, 2 * MIX_W), BF16),
            jax.ShapeDtypeStruct((b, FOX_PAIRS, s // ts, LANES, ts), BF16),
        ],
        scratch_shapes=[pltpu.VMEM((8, LANES), F32)],
        compiler_params=_params(("arbitrary", "arbitrary")),
        name="fox_prep",
    )(big3, big3, big3, small3, fbias, qgain, kgain, *consts)


FOX_NOSHIFT_BOUND = 40.0
FOX_TILE = 512
HSUM_W = 256
FOX_UNROLL = 4


def _fox_attn_kernel(qt_ref, k_ref, vt_ref, o_ref, acc_ref, l_ref, m_ref, *, tq, online):
    i = pl.program_id(2)
    qt = qt_ref[0, 0, 0]
    qrow = lax.broadcasted_iota(jnp.int32, (2 * LANES, 1), 0)
    in_a = (qrow < FOX_HD) | ((qrow >= LANES) & (qrow < LANES + 6))
    in_b = ((qrow >= FOX_HD) & (qrow < LANES)) | ((qrow >= LANES + 6) & (qrow < LANES + 12))
    zero = jnp.zeros_like(qt)
    qt_heads = (jnp.where(in_a, qt, zero), jnp.where(in_b, qt, zero))

    acc_ref[...] = jnp.zeros_like(acc_ref)
    l_ref[...] = jnp.zeros_like(l_ref)
    if online:
        m_ref[...] = jnp.full_like(m_ref, NEG_BIG)

    def scores(j, a, diag):
        k = k_ref[0, pl.ds(pl.multiple_of(j * tq, tq), tq), :]
        st = _dot(k, qt_heads[a])
        if diag:
            krow = lax.broadcasted_iota(jnp.int32, (tq, tq), 0)
            qcol = lax.broadcasted_iota(jnp.int32, (tq, tq), 1)
            st = jnp.where(krow <= qcol, st, NEG_BIG)
        return st

    def accumulate(j, a, st):
        vt_a = vt_ref[0, 0, j, FOX_HD * a:FOX_HD * (a + 1), :]
        if online:
            m_prev = m_ref[a]
            m_new = jnp.maximum(m_prev, jnp.max(st, axis=0, keepdims=True))
            alpha = jnp.exp2(m_prev - m_new)
            m_ref[a] = m_new
            pt = jnp.exp2(st - m_new)
            l_ref[a] = alpha * l_ref[a] + jnp.sum(pt.reshape(tq // 8, 8, tq), axis=0)
            acc_ref[a] = alpha * acc_ref[a] + _dot(vt_a, pt.astype(BF16))
        else:
            pt = jnp.exp2(st)
            l_ref[a] += jnp.sum(pt.reshape(tq // 8, 8, tq), axis=0)
            acc_ref[a] += _dot(vt_a, pt.astype(BF16))

    def run(units):
        st = scores(*units[0])
        for u, unit in enumerate(units):
            st_next = scores(*units[u + 1]) if u + 1 < len(units) else None
            accumulate(unit[0], unit[1], st)
            st = st_next

    def units(first_block, n_regular, with_diag):
        blocks = [(first_block + d, False) for d in range(n_regular)]
        if with_diag:
            blocks.append((first_block + n_regular, True))
        return [(j, a, diag) for j, diag in blocks for a in range(2)]

    def body(jj, carry):
        run(units(FOX_UNROLL * jj, FOX_UNROLL, False))
        return carry

    lax.fori_loop(0, i // FOX_UNROLL, body, 0)
    for rem in range(FOX_UNROLL):
        @pl.when(i % FOX_UNROLL == rem)
        def _():
            run(units(i - rem, rem, True))

    halves = [acc_ref[a] * (1.0 / jnp.sum(l_ref[a], axis=0, keepdims=True)) for a in range(2)]
    o_ref[0] = jnp.concatenate(halves, axis=0).T.astype(BF16)


def _fox_attn(qt, kf, vt, logit_bound):
    b, _, nq, _, tq = qt.shape
    s = nq * tq

    def call(online, name):
        return pl.pallas_call(
            functools.partial(_fox_attn_kernel, tq=tq, online=online),
            grid=(b, FOX_PAIRS, nq),
            in_specs=[
                pl.BlockSpec((1, 1, 1, 2 * LANES, tq), lambda bi, p, i: (bi, p, i, 0, 0)),
                pl.BlockSpec((1, s, 2 * LANES), lambda bi, p, i: (bi, 0, p)),
                pl.BlockSpec((1, 1, nq, LANES, tq), lambda bi, p, i: (bi, p, 0, 0, 0)),
            ],
            out_specs=pl.BlockSpec((1, tq, LANES), lambda bi, p, i: (bi, i, p)),
            out_shape=jax.ShapeDtypeStruct((b, s, MIX_W), BF16),
            scratch_shapes=[
                pltpu.VMEM((2, FOX_HD, tq), F32),
                pltpu.VMEM((2, 8, tq), F32),
                pltpu.VMEM((2, 1, tq), F32),
            ],
            compiler_params=_params(("arbitrary", "arbitrary", "arbitrary")),
            name=name,
        )(qt, kf, vt)

    return lax.cond(logit_bound < FOX_NOSHIFT_BOUND,
                    lambda: call(False, "fox_attn"), lambda: call(True, "fox_attn_online"))


def _gla_consts(r):
    idx = np.arange(r)
    same = (idx[:, None] // GLA_CHUNK) == (idx[None, :] // GLA_CHUNK)
    lblk = (same & (idx[None, :] <= idx[:, None])).astype(np.float32)
    ablk = same.astype(np.float32)
    return jnp.asarray(lblk, BF16), jnp.asarray(ablk, BF16)


def _gla_kernel(q_ref, k_ref, v_ref, r_ref, small_ref, wlh_ref, wll_ref, bg_ref, lblk_ref,
                ablk_ref, gain_ref, o_ref, st_ref, oacc_ref, *, r):
    @pl.when(pl.program_id(1) == 0)
    def _():
        st_ref[...] = jnp.zeros_like(st_ref)

    kw = GLA_HEADS * GLA_DK
    gate = _dot_f32w(small_ref[0], wlh_ref[...], wll_ref[...]) + bg_ref[...]
    log_a = _log_sigmoid(gate) * (1.0 / GLA_GATE_NORM)
    a1, a2, a3 = _split3(log_a)
    lblk = lblk_ref[...]
    ablk = ablk_ref[...]
    bcum = _dot(lblk, a1) + _dot(lblk, a2) + _dot(lblk, a3)
    btot = _dot(ablk, a1) + _dot(ablk, a2) + _dot(ablk, a3)
    q = q_ref[0].astype(F32) * (GLA_DK ** -0.5)
    k = k_ref[0].astype(F32)
    q_dec = (q * jnp.exp(bcum)).astype(BF16)
    k_dec = (k * jnp.exp(-bcum)).astype(BF16)
    k_end_t = (k * jnp.exp(btot - bcum)).T.astype(BF16)
    d_tot_t = jnp.exp(btot).T
    v = v_ref[0]

    row = lax.broadcasted_iota(jnp.int32, (r, r), 0)
    col = lax.broadcasted_iota(jnp.int32, (r, r), 1)
    keep = (_shr(row, GLA_CHUNK) == _shr(col, GLA_CHUNK)) & (col <= row)
    klane = lax.broadcasted_iota(jnp.int32, (1, kw), 1)
    qzero = jnp.zeros_like(q_dec)
    q_heads = []
    for h in range(GLA_HEADS):
        in_h = (klane >= GLA_DK * h) & (klane < GLA_DK * (h + 1))
        q_heads.append(jnp.where(in_h, q_dec, qzero))
        att = _dot_nt(q_heads[h], k_dec)
        att = jnp.where(keep, att, 0.0).astype(BF16)
        vcols = slice(GLA_DV * h, GLA_DV * (h + 1))
        oacc_ref[:, vcols] = _dot(att, v[:, vcols])

    tlane = lax.broadcasted_iota(jnp.int32, (1, r), 1)
    kzero = jnp.zeros_like(k_end_t)
    for c in range(r // GLA_CHUNK):
        rows = slice(GLA_CHUNK * c, GLA_CHUNK * (c + 1))
        st = st_ref[...]
        st_b = st.astype(BF16)
        k_chunk = jnp.where(_shr(tlane, GLA_CHUNK) == c, k_end_t, kzero)
        updates = []
        for h in range(GLA_HEADS):
            vcols = slice(GLA_DV * h, GLA_DV * (h + 1))
            oacc_ref[rows, vcols] += _dot(q_heads[h][rows], st_b)
            updates.append(_dot(k_chunk[GLA_DK * h:GLA_DK * (h + 1), :], v[:, vcols]))
        decay = d_tot_t[:, GLA_CHUNK * c:GLA_CHUNK * c + 1]
        st_ref[...] = decay * st + jnp.concatenate(updates, axis=0)

    gain = gain_ref[...]
    gr = r_ref[0].astype(F32)
    for h in range(GLA_HEADS):
        vcols = slice(GLA_DV * h, GLA_DV * (h + 1))
        o = oacc_ref[:, vcols]
        ms = jnp.mean(o * o, axis=-1, keepdims=True)
        o_ref[0, :, vcols] = (o * lax.rsqrt(ms + EPS) * gain * _silu(gr[:, vcols])).astype(BF16)


def _gla(big3, small3, wl_hi, wl_lo, bgate, gain, *, r=256):
    b, s, _ = big3.shape
    r = min(r, s)
    kw = GLA_HEADS * GLA_DK
    lblk, ablk = _gla_consts(r)
    return pl.pallas_call(
        functools.partial(_gla_kernel, r=r),
        grid=(b, s // r),
        in_specs=[
            pl.BlockSpec((1, r, kw), lambda bi, i: (bi, i, GQ_OFF // kw)),
            pl.BlockSpec((1, r, kw), lambda bi, i: (bi, i, GK_OFF // kw)),
            pl.BlockSpec((1, r, MIX_W), lambda bi, i: (bi, i, GV_OFF // MIX_W)),
            pl.BlockSpec((1, r, MIX_W), lambda bi, i: (bi, i, GR_OFF // MIX_W)),
            pl.BlockSpec((1, r, LANES), lambda bi, i: (bi, i, 0)),
            pl.BlockSpec((LANES, kw), lambda bi, i: (0, 0)),
            pl.BlockSpec((LANES, kw), lambda bi, i: (0, 0)),
            pl.BlockSpec((1, kw), lambda bi, i: (0, 0)),
            pl.BlockSpec((r, r), lambda bi, i: (0, 0)),
            pl.BlockSpec((r, r), lambda bi, i: (0, 0)),
            pl.BlockSpec((1, GLA_DV), lambda bi, i: (0, 0)),
        ],
        out_specs=pl.BlockSpec((1, r, MIX_W), lambda bi, i: (bi, i, 0)),
        out_shape=jax.ShapeDtypeStruct((b, s, MIX_W), BF16),
        scratch_shapes=[pltpu.VMEM((kw, GLA_DV), F32), pltpu.VMEM((r, MIX_W), F32)],
        compiler_params=_params(("arbitrary", "arbitrary")),
        name="gla",
    )(big3, big3, big3, big3, small3, wl_hi, wl_lo, bgate, lblk, ablk, gain)


SSD_HALO = 16
SSD_CONV_W = MIX_W + 2 * SSM_GROUPS * SSM_STATE
SSD_STATE_W = 256


def _ssd_consts(r):
    expand = np.zeros((LANES, MIX_W), np.float32)
    for h in range(SSM_HEADS):
        expand[SDT_LANE + h, SSM_HD * h:SSM_HD * (h + 1)] = 1.0
    rep = np.zeros((LANES, MIX_W), np.float32)
    for h in range(SSM_HEADS):
        g = h // (SSM_HEADS // SSM_GROUPS)
        for n in range(SSM_STATE):
            rep[SSM_STATE * g + n, SSM_HD * h + n] = 1.0
    ltri = np.tril(np.ones((r, r), np.float32))
    return (jnp.asarray(expand, BF16), jnp.asarray(rep, BF16), jnp.asarray(ltri, BF16),
            jnp.asarray(ltri.T, BF16))


def _ssd_kernel(z_ref, x_ref, xp_ref, b_ref, bp_ref, c_ref, cp_ref, small_ref, cw_ref, cb_ref,
                dtb_ref, alog_ref, dtbc_ref, alogc_ref, dskip_ref, onorm_ref, expand_ref,
                rep_ref, ltri_ref, utri_ref, o_ref, sw_ref, ext_ref, y_ref, *, r):
    first = pl.program_id(1) == 0

    @pl.when(first)
    def _():
        sw_ref[...] = jnp.zeros_like(sw_ref)

    keep_prev = jnp.where(first, 0.0, 1.0)
    ext_ref[0:SSD_HALO, 0:MIX_W] = xp_ref[0].astype(F32) * keep_prev
    ext_ref[0:SSD_HALO, MIX_W:MIX_W + LANES] = bp_ref[0].astype(F32) * keep_prev
    ext_ref[0:SSD_HALO, MIX_W + LANES:SSD_CONV_W] = cp_ref[0].astype(F32) * keep_prev
    ext_ref[SSD_HALO:, 0:MIX_W] = x_ref[0].astype(F32)
    ext_ref[SSD_HALO:, MIX_W:MIX_W + LANES] = b_ref[0].astype(F32)
    ext_ref[SSD_HALO:, MIX_W + LANES:SSD_CONV_W] = c_ref[0].astype(F32)
    conv = cb_ref[...] + cw_ref[SSM_CONV - 1:SSM_CONV, :] * ext_ref[SSD_HALO:, :]
    for back in range(1, SSM_CONV):
        tap = SSM_CONV - 1 - back
        conv = conv + cw_ref[tap:tap + 1, :] * ext_ref[pl.ds(SSD_HALO - back, r), :]
    xbc = _silu(conv)
    xs = xbc[:, 0:MIX_W]
    bm = xbc[:, MIX_W:MIX_W + LANES].astype(BF16)
    cm = xbc[:, MIX_W + LANES:SSD_CONV_W].astype(BF16)

    sm = small_ref[0]
    dt = _dot3_right(_softplus(sm + dtb_ref[...]), expand_ref[...])
    a_neg = -jnp.exp(alog_ref[...])
    acs = _dot3_left(ltri_ref[...], dt * a_neg)
    acs_last = acs[r - 1:r, :]
    sm_t = sm.T
    dt_t = _softplus(sm_t[SDT_LANE:SDT_LANE + SSM_HEADS, :] + dtbc_ref[:, 0:1])
    acs_t = _dot3_right(dt_t * (-jnp.exp(alogc_ref[:, 0:1])), utri_ref[...])

    xdt = (xs * dt).astype(BF16)
    row = lax.broadcasted_iota(jnp.int32, (r, r), 0)
    col = lax.broadcasted_iota(jnp.int32, (r, r), 1)
    causal = col <= row
    glane = lax.broadcasted_iota(jnp.int32, (1, LANES), 1)
    first_half = glane < SSM_STATE
    czero = jnp.zeros_like(cm)
    hpg = SSM_HEADS // SSM_GROUPS
    for g in range(SSM_GROUPS):
        cg = jnp.where(first_half if g == 0 else ~first_half, cm, czero)
        cb_g = _dot_nt(cg, bm)
        for pair in range(hpg // 2):
            p = g * (hpg // 2) + pair
            xp = xdt[:, LANES * p:LANES * (p + 1)]
            xzero = jnp.zeros_like(xp)
            acc = None
            for a in range(2):
                h = 2 * p + a
                dmat = acs[:, SSM_HD * h:SSM_HD * h + 1] - acs_t[h:h + 1, :]
                sc = (cb_g * jnp.exp(jnp.where(causal, dmat, NEG_BIG))).astype(BF16)
                xh = jnp.where(first_half if a == 0 else ~first_half, xp, xzero)
                contrib = _dot(sc, xh)
                acc = contrib if acc is None else acc + contrib
            y_ref[:, LANES * p:LANES * (p + 1)] = acc

    rep = rep_ref[...]
    cw = (_dot(cm, rep) * jnp.exp(acs)).astype(BF16)
    bw = (_dot(bm, rep) * jnp.exp(acs_last - acs)).astype(BF16)
    srow = lax.broadcasted_iota(jnp.int32, (SSD_STATE_W, SSD_STATE_W), 0)
    scol = lax.broadcasted_iota(jnp.int32, (SSD_STATE_W, SSD_STATE_W), 1)
    same_head = _shr(srow, SSM_STATE) == _shr(scol, SSM_HD)
    y_inter = []
    for half in range(MIX_W // SSD_STATE_W):
        cols = slice(SSD_STATE_W * half, SSD_STATE_W * (half + 1))
        sw = sw_ref[half]
        y_inter.append(_dot(cw[:, cols], sw.astype(BF16)))
        upd = _dot_tn(bw[:, cols], xdt[:, cols])
        sw_ref[half] = sw * jnp.exp(acs_last[:, cols]) + jnp.where(same_head, upd, 0.0)
    y = y_ref[...] + jnp.concatenate(y_inter, axis=1) + dskip_ref[...] * xs

    y = y * _silu(z_ref[0].astype(F32))
    gw = MIX_W // SSM_GROUPS
    for g in range(SSM_GROUPS):
        cols = slice(gw * g, gw * (g + 1))
        yg = y[:, cols]
        ms = jnp.mean(yg * yg, axis=-1, keepdims=True)
        o_ref[0, :, cols] = (yg * lax.rsqrt(ms + EPS) * onorm_ref[:, cols]).astype(BF16)


def _ssd(big3, small3, conv_w, conv_b, dtb_w, alog_w, dtb_c, alog_c, dskip_w, onorm, *, r=256):
    b, s, _ = big3.shape
    r = min(r, s)
    consts = _ssd_consts(r)
    hb = r // SSD_HALO

    def cur(width, off):
        return pl.BlockSpec((1, r, width), lambda bi, i: (bi, i, off // width))

    def prev(width, off):
        return pl.BlockSpec((1, SSD_HALO, width),
                            lambda bi, i: (bi, jnp.maximum(i * hb - 1, 0), off // width))

    def whole(shape):
        return pl.BlockSpec(shape, lambda bi, i: (0,) * len(shape))

    return pl.pallas_call(
        functools.partial(_ssd_kernel, r=r),
        grid=(b, s // r),
        in_specs=[
            cur(MIX_W, SZ_OFF),
            cur(MIX_W, SX_OFF), prev(MIX_W, SX_OFF),
            cur(LANES, SB_OFF), prev(LANES, SB_OFF),
            cur(LANES, SC_OFF), prev(LANES, SC_OFF),
            pl.BlockSpec((1, r, LANES), lambda bi, i: (bi, i, 0)),
            whole((SSM_CONV, SSD_CONV_W)), whole((1, SSD_CONV_W)),
            whole((1, LANES)), whole((1, MIX_W)),
            whole((SSM_HEADS, LANES)), whole((SSM_HEADS, LANES)),
            whole((1, MIX_W)), whole((1, MIX_W)),
            whole((LANES, MIX_W)), whole((LANES, MIX_W)), whole((r, r)), whole((r, r)),
        ],
        out_specs=pl.BlockSpec((1, r, MIX_W), lambda bi, i: (bi, i, 0)),
        out_shape=jax.ShapeDtypeStruct((b, s, MIX_W), BF16),
        scratch_shapes=[
            pltpu.VMEM((MIX_W // SSD_STATE_W, SSD_STATE_W, SSD_STATE_W), F32),
            pltpu.VMEM((r + SSD_HALO, SSD_CONV_W), F32),
            pltpu.VMEM((r, MIX_W), F32),
        ],
        compiler_params=_params(("arbitrary", "arbitrary")),
        name="ssd",
    )(big3, big3, big3, big3, big3, big3, big3, small3, conv_w, conv_b, dtb_w, alog_w,
      dtb_c, alog_c, dskip_w, onorm, *consts)


def _merge_kernel(oa_ref, ob_ref, oc_ref, gate_ref, x_ref, wb_ref, bgate_ref, wo_ref, nffn_ref,
                  wrh_ref, wrl_ref, br_ref, ustrict_ref, xe_ref, route_ref, counts_ref, *, tm):
    @pl.when(pl.program_id(0) == 0)
    def _():
        counts_ref[...] = jnp.zeros_like(counts_ref)

    mixed = None
    for ridx, o_ref in enumerate((oa_ref, ob_ref, oc_ref)):
        cols = slice(D_MODEL * ridx, D_MODEL * (ridx + 1))
        gate = _sigmoid(gate_ref[:, cols].astype(F32) + bgate_ref[ridx:ridx + 1, :])
        term = gate * _dot(o_ref[...], wb_ref[ridx])
        mixed = term if mixed is None else mixed + term
    xn = x_ref[...] + _dot(mixed.astype(BF16), wo_ref[...])
    xe_ref[:, 0:D_MODEL] = xn
    ms = jnp.mean(xn * xn, axis=-1, keepdims=True)
    h = xn * lax.rsqrt(ms + EPS) * nffn_ref[...]

    lt = (_dot_f32w(h, wrh_ref[...], wrl_ref[...]) + br_ref[...]).T
    grow = lax.broadcasted_iota(jnp.int32, (8, tm), 0)
    grow_f = grow.astype(F32)
    far_row = float(LANES)
    gl = jnp.where(grow < N_EGROUPS, lt[RG_LANE:RG_LANE + 8, :], NEG_BIG)
    gmax = jnp.max(gl, axis=0, keepdims=True)
    g_w = 1.0 / jnp.sum(jnp.exp(gl - gmax), axis=0, keepdims=True)
    g_sel = jnp.min(jnp.where(gl == gmax, grow_f, far_row), axis=0, keepdims=True)
    e16 = lt[RE_LANE:RE_LANE + N_EXPERTS, :]
    erow = lax.broadcasted_iota(jnp.int32, (N_EXPERTS, tm), 0)
    erow_f = erow.astype(F32)
    in_grp = _shr(erow, EXP_PER_GROUP).astype(F32) == g_sel
    el = jnp.where(in_grp, e16, NEG_BIG)
    e1 = jnp.max(el, axis=0, keepdims=True)
    i1 = jnp.min(jnp.where(in_grp & (el == e1), erow_f, far_row), axis=0, keepdims=True)
    rest = in_grp & (erow_f != i1)
    el2 = jnp.where(rest, e16, NEG_BIG)
    e2 = jnp.max(el2, axis=0, keepdims=True)
    i2 = jnp.min(jnp.where(rest & (el2 == e2), erow_f, far_row), axis=0, keepdims=True)
    ratio = jnp.exp(e2 - e1)
    w1 = g_w / (1.0 + ratio)
    w2 = w1 * ratio
    comb_t = jnp.where(erow_f == i1, w1, 0.0) + jnp.where(erow_f == i2, w2, 0.0)
    comb_t = jnp.concatenate([jnp.zeros((RE_LANE, tm), F32), comb_t,
                              jnp.zeros((LANES - RE_LANE - N_EXPERTS, tm), F32)], axis=0)
    xe_ref[:, D_MODEL:] = comb_t.T

    first_row = EXP_PER_GROUP * g_sel
    lo = jnp.minimum(i1, i2) - first_row
    hi = jnp.maximum(i1, i2) - first_row
    cls = PAIRS_PER_GROUP * g_sel + lo * (7.0 - lo) * 0.5 + (hi - lo - 1.0)
    crow_f = lax.broadcasted_iota(jnp.int32, (ROUTE_ROWS, tm), 0).astype(F32)
    is_cls = crow_f == cls
    onehot = jnp.where(is_cls, 1.0, 0.0)
    before = _dot(onehot.astype(BF16), ustrict_ref[...]) + counts_ref[:, 0:1]
    rank = jnp.sum(jnp.where(is_cls, before, 0.0), axis=0, keepdims=True)
    counts_ref[...] = counts_ref[...] + jnp.sum(onehot, axis=1, keepdims=True)
    rank_hi = jnp.floor(rank * (1.0 / LANES))
    rank_lo = rank - rank_hi * LANES
    route_ref[...] = jnp.where(grow == 0, cls, jnp.where(grow == 1, rank_hi,
                                                         jnp.where(grow == 2, rank_lo, 0.0)))


def _merge(oa, ob, oc, big, x2, wb, bgate, wo, nffn, wr_hi, wr_lo, br, *, tm=512):
    t = x2.shape[0]
    tm = min(tm, t)
    ustrict = jnp.asarray(np.triu(np.ones((tm, tm), np.float32), 1), BF16)

    def whole(shape):
        return pl.BlockSpec(shape, lambda i: (0,) * len(shape))

    return pl.pallas_call(
        functools.partial(_merge_kernel, tm=tm),
        grid=(t // tm,),
        in_specs=[
            pl.BlockSpec((tm, MIX_W), lambda i: (i, 0)),
            pl.BlockSpec((tm, MIX_W), lambda i: (i, 0)),
            pl.BlockSpec((tm, MIX_W), lambda i: (i, 0)),
            pl.BlockSpec((tm, 3 * D_MODEL), lambda i: (i, 0)),
            pl.BlockSpec((tm, D_MODEL), lambda i: (i, 0)),
            whole((3, MIX_W, D_MODEL)), whole((3, D_MODEL)), whole((D_MODEL, D_MODEL)),
            whole((1, D_MODEL)), whole((D_MODEL, LANES)), whole((D_MODEL, LANES)),
            whole((1, LANES)), whole((tm, tm)),
        ],
        out_specs=[
            pl.BlockSpec((tm, XE_COLS), lambda i: (i, 0)),
            pl.BlockSpec((8, tm), lambda i: (0, i)),
            pl.BlockSpec((ROUTE_ROWS, LANES), lambda i: (0, 0)),
        ],
        out_shape=[
            jax.ShapeDtypeStruct((t, XE_COLS), F32),
            jax.ShapeDtypeStruct((8, t), F32),
            jax.ShapeDtypeStruct((ROUTE_ROWS, LANES), F32),
        ],
        compiler_params=_params(("arbitrary",)),
        name="merge",
    )(oa, ob, oc, big, x2, wb, bgate, wo, nffn, wr_hi, wr_lo, br, ustrict)


MOE_TILE = 256
ROW_DMA_TILE = 512


def _row_dma(pos_ref, near_ref, far_ref, sem, r, to_far):
    near, far = near_ref.at[pl.ds(r, 1)], far_ref.at[pl.ds(pos_ref[0, 0, r], 1)]
    return pltpu.make_async_copy(near, far, sem) if to_far else pltpu.make_async_copy(far, near, sem)


def _row_burst(pos_ref, near_ref, far_ref, sem, tm, to_far):
    for r in range(tm):
        _row_dma(pos_ref, near_ref, far_ref, sem, r, to_far).start(priority=r % 2)

    def drain(r, carry):
        _row_dma(pos_ref, near_ref, far_ref, sem, r, to_far).wait()
        return carry

    lax.fori_loop(0, tm, drain, 0, unroll=8)


def _row_gather_kernel(pos_ref, src_ref, o_ref, sem, *, tm):
    _row_burst(pos_ref, o_ref, src_ref, sem, tm, to_far=False)


def _row_scatter_kernel(pad_ref, pos_ref, src_ref, dst_ref, zeros_ref, sem, *, tm, pad_rows):
    @pl.when(pl.program_id(0) == 0)
    def _():
        zeros_ref[...] = jnp.zeros_like(zeros_ref)
        n_cls = pad_ref.shape[0] - 1

        def blank(c):
            start = pl.multiple_of(pad_ref[c], 8)
            return pltpu.make_async_copy(zeros_ref, dst_ref.at[pl.ds(start, pad_rows)], sem)

        for c in range(n_cls):
            blank(c).start()
        for c in range(n_cls):
            blank(c).wait()

        tile = pad_rows - 8

        def blank_tile(j, carry):
            start = pl.multiple_of(j * tile, tile)
            copy = pltpu.make_async_copy(zeros_ref.at[pl.ds(0, tile)], dst_ref.at[pl.ds(start, tile)], sem)
            copy.start()
            copy.wait()
            return carry

        lax.fori_loop(pad_ref[n_cls] // tile, dst_ref.shape[0] // tile, blank_tile, 0)

    _row_burst(pos_ref, src_ref, dst_ref, sem, tm, to_far=True)


def _row_gather(pos, src, out_rows):
    tm = min(ROW_DMA_TILE, out_rows)
    width = src.shape[1]
    return pl.pallas_call(
        functools.partial(_row_gather_kernel, tm=tm),
        grid=(out_rows // tm,),
        in_specs=[pl.BlockSpec((1, 1, tm), lambda i: (i, 0, 0), memory_space=pltpu.SMEM),
                  pl.BlockSpec(memory_space=pl.ANY)],
        out_specs=pl.BlockSpec((tm, width), lambda i: (i, 0)),
        out_shape=jax.ShapeDtypeStruct((out_rows, width), src.dtype),
        scratch_shapes=[pltpu.SemaphoreType.DMA(())],
        compiler_params=_params(("arbitrary",)),
        name="moe_gather",
    )(pos.reshape(out_rows // tm, 1, tm), src)


def _row_scatter(pos, pad_start, src, out_rows, pad_rows):
    t, width = src.shape
    tm = min(ROW_DMA_TILE, t)
    grid_spec = pltpu.PrefetchScalarGridSpec(
        num_scalar_prefetch=1,
        grid=(t // tm,),
        in_specs=[pl.BlockSpec((1, 1, tm), lambda i, pad: (i, 0, 0), memory_space=pltpu.SMEM),
                  pl.BlockSpec((tm, width), lambda i, pad: (i, 0))],
        out_specs=pl.BlockSpec(memory_space=pl.ANY),
        scratch_shapes=[pltpu.VMEM((pad_rows, width), src.dtype), pltpu.SemaphoreType.DMA(())],
    )
    return pl.pallas_call(
        functools.partial(_row_scatter_kernel, tm=tm, pad_rows=pad_rows),
        grid_spec=grid_spec,
        out_shape=jax.ShapeDtypeStruct((out_rows, width), src.dtype),
        compiler_params=_params(("arbitrary",)),
        name="moe_scatter",
    )(pad_start, pos.reshape(t // tm, 1, tm), src)


def _moe_sorted_kernel(ea_ref, eb_ref, nused_ref, xs_ref, nffn_ref, wga_ref, wua_ref, wda_ref,
                       wgb_ref, wub_ref, wdb_ref, o_ref, *, tm):
    i = pl.program_id(0)

    @pl.when(i >= nused_ref[0])
    def _():
        o_ref[...] = jnp.zeros_like(o_ref)


    @pl.when(i < nused_ref[0])
    def _():
        xn = xs_ref[:, 0:D_MODEL]
        comb = xs_ref[:, D_MODEL:]
        ms = jnp.mean(xn * xn, axis=-1, keepdims=True)
        h = (xn * lax.rsqrt(ms + EPS) * nffn_ref[...]).astype(BF16)
        lane = lax.broadcasted_iota(jnp.int32, (tm, LANES), 1)
        out = xn
        for e_ref, wg_ref, wu_ref, wd_ref in ((ea_ref, wga_ref, wua_ref, wda_ref),
                                              (eb_ref, wgb_ref, wub_ref, wdb_ref)):
            w = jnp.sum(jnp.where(lane == e_ref[i] + RE_LANE, comb, 0.0), axis=-1, keepdims=True)
            hid = _silu(_dot(h, wg_ref[0])) * _dot(h, wu_ref[0])
            out = out + w * _dot(hid.astype(BF16), wd_ref[0])
        o_ref[...] = out


def _moe_sorted(ea, eb, nused, xs, nffn, wg, wu, wd, *, tm):
    n_tiles = xs.shape[0] // tm

    def w_in(which):
        return pl.BlockSpec((1, D_MODEL, D_EXPERT), lambda i, ea, eb, nu: ((ea, eb)[which][i], 0, 0))

    def w_out(which):
        return pl.BlockSpec((1, D_EXPERT, D_MODEL), lambda i, ea, eb, nu: ((ea, eb)[which][i], 0, 0))

    grid_spec = pltpu.PrefetchScalarGridSpec(
        num_scalar_prefetch=3,
        grid=(n_tiles,),
        in_specs=[
            pl.BlockSpec((tm, XE_COLS), lambda i, ea, eb, nu: (jnp.minimum(i, nu[0] - 1), 0)),
            pl.BlockSpec((1, D_MODEL), lambda i, ea, eb, nu: (0, 0)),
            w_in(0), w_in(0), w_out(0), w_in(1), w_in(1), w_out(1),
        ],
        out_specs=pl.BlockSpec((tm, D_MODEL), lambda i, ea, eb, nu: (i, 0)),
    )
    return pl.pallas_call(
        functools.partial(_moe_sorted_kernel, tm=tm),
        grid_spec=grid_spec,
        out_shape=jax.ShapeDtypeStruct((n_tiles * tm, D_MODEL), F32),
        compiler_params=_params(("arbitrary",)),
        name="moe",
    )(ea, eb, nused, xs, nffn, wg, wu, wd, wg, wu, wd)


_PAIR_LO = np.array([0, 0, 0, 1, 1, 2], np.int32)
_PAIR_HI = np.array([1, 2, 3, 2, 3, 3], np.int32)


def _moe(xe, route, counts, nffn, wg, wu, wd):
    t = xe.shape[0]
    tm = min(MOE_TILE, t)
    n_cls = N_EGROUPS * PAIRS_PER_GROUP
    n_tiles = t // tm + n_cls
    cnt = counts[:n_cls, 0].astype(jnp.int32)
    tiles = (cnt + tm - 1) // tm
    tile_end = jnp.cumsum(tiles)
    n_used = tile_end[-1]
    cls_base = (tile_end - tiles) * tm
    tile_idx = jnp.minimum(jnp.arange(n_tiles), n_used - 1)
    tile_cls = jnp.sum((tile_end[None, :] <= tile_idx[:, None]).astype(jnp.int32), axis=1)
    grp, pair = tile_cls // PAIRS_PER_GROUP, tile_cls % PAIRS_PER_GROUP
    ea = EXP_PER_GROUP * grp + jnp.asarray(_PAIR_LO)[pair]
    eb = EXP_PER_GROUP * grp + jnp.asarray(_PAIR_HI)[pair]
    rank = (route[1] * LANES + route[2]).astype(jnp.int32)
    pos = cls_base[route[0].astype(jnp.int32)] + rank

    pad_start = jnp.minimum((cls_base + cnt) // 8 * 8, n_tiles * tm - (tm + 8))
    pad_start = jnp.concatenate([pad_start, (n_used * tm).reshape(1)])
    xs = _row_scatter(pos, pad_start, xe, n_tiles * tm, tm + 8)
    ys = _moe_sorted(ea, eb, n_used.reshape(1), xs, nffn, wg, wu, wd, tm=tm)
    return _row_gather(pos, ys, t)


def _cols(w, off, width):
    return w[:, off:off + width]


def _pad_lanes(v, lane0, width=LANES):
    out = jnp.zeros((1, width), F32)
    return out.at[0, lane0:lane0 + v.shape[0]].set(v.astype(F32))


def _layer(x2, b, s, norm_mix, w_in, fox_f_bias, fox_q_norm, fox_k_norm, gla_w_lr, gla_b_gate,
           gla_out_norm, ssm_conv_w, ssm_conv_b, ssm_dt_bias, ssm_a_log, ssm_d, ssm_out_norm,
           w_branch, b_branch_gate, w_out, norm_ffn, w_router_grp, b_router_grp,
           w_router_exp, b_router_exp, w_exp_gate, w_exp_up, w_exp_down):
    t = b * s
    w_big = jnp.concatenate([
        _cols(w_in, _O_GATE, 3 * D_MODEL), _cols(w_in, _O_FQ, MIX_W), _cols(w_in, _O_FK, MIX_W),
        _cols(w_in, _O_FV, MIX_W), _cols(w_in, _O_GV, MIX_W), _cols(w_in, _O_GR, MIX_W),
        _cols(w_in, _O_SZ, MIX_W), _cols(w_in, _O_SX, MIX_W), _cols(w_in, _O_GQ, 256),
        _cols(w_in, _O_GK, 256), _cols(w_in, _O_SB, LANES), _cols(w_in, _O_SC, LANES),
    ], axis=1).astype(BF16)
    w_small = jnp.concatenate([
        _cols(w_in, _O_FF, FOX_HEADS), _cols(w_in, _O_GLR, GLA_RANK), _cols(w_in, _O_SDT, SSM_HEADS),
        jnp.zeros((D_MODEL, LANES - FOX_HEADS - GLA_RANK - SSM_HEADS), F32),
    ], axis=1)
    ws_hi, ws_lo = _split2(w_small)

    big, small = _inproj(x2, norm_mix.reshape(1, D_MODEL), w_big, ws_hi, ws_lo)
    big3 = big.reshape(b, s, BIG_COLS)
    small3 = small.reshape(b, s, LANES)

    qt, kf, vt = _fox_prep(big3, small3, _pad_lanes(fox_f_bias, FF_LANE),
                           jnp.tile(fox_q_norm, FOX_HEADS).reshape(1, MIX_W),
                           jnp.tile(fox_k_norm, FOX_HEADS).reshape(1, MIX_W), ts=min(FOX_TILE, s))
    logit_bound = ((FOX_HD ** 0.5) * 1.01 * jnp.max(jnp.abs(fox_q_norm))
                   * jnp.max(jnp.abs(fox_k_norm)))
    o_a = _fox_attn(qt, kf, vt, logit_bound)

    wl = jnp.zeros((LANES, GLA_HEADS * GLA_DK), F32).at[GLR_LANE:GLR_LANE + GLA_RANK].set(gla_w_lr)
    wl_hi, wl_lo = _split2(wl)
    o_b = _gla(big3, small3, wl_hi, wl_lo, gla_b_gate.reshape(1, -1),
               gla_out_norm.reshape(1, GLA_DV))

    o_c = _ssd(big3, small3, ssm_conv_w, ssm_conv_b.reshape(1, -1),
               _pad_lanes(ssm_dt_bias, SDT_LANE),
               jnp.repeat(ssm_a_log, SSM_HD).reshape(1, MIX_W),
               jnp.broadcast_to(ssm_dt_bias[:, None], (SSM_HEADS, LANES)),
               jnp.broadcast_to(ssm_a_log[:, None], (SSM_HEADS, LANES)),
               jnp.repeat(ssm_d, SSM_HD).reshape(1, MIX_W),
               ssm_out_norm.reshape(1, MIX_W))

    w_r = jnp.concatenate([w_router_grp, jnp.zeros((D_MODEL, RE_LANE - N_EGROUPS), F32), w_router_exp,
                           jnp.zeros((D_MODEL, LANES - RE_LANE - N_EXPERTS), F32)], axis=1)
    wr_hi, wr_lo = _split2(w_r)
    b_r = jnp.concatenate([b_router_grp, jnp.zeros((RE_LANE - N_EGROUPS,), F32), b_router_exp,
                           jnp.zeros((LANES - RE_LANE - N_EXPERTS,), F32)]).reshape(1, LANES)
    nffn = norm_ffn.reshape(1, D_MODEL)
    xe, route, counts = _merge(o_a.reshape(t, MIX_W), o_b.reshape(t, MIX_W), o_c.reshape(t, MIX_W),
                               big, x2, w_branch.astype(BF16), b_branch_gate, w_out.astype(BF16),
                               nffn, wr_hi, wr_lo, b_r)

    return _moe(xe, route, counts, nffn, w_exp_gate.astype(BF16), w_exp_up.astype(BF16),
                w_exp_down.astype(BF16))


def kernel(x, norm_mix, w_in, fox_f_bias, fox_q_norm, fox_k_norm, gla_w_lr, gla_b_gate, gla_out_norm, ssm_conv_w, ssm_conv_b, ssm_dt_bias, ssm_a_log, ssm_d, ssm_out_norm, w_branch, b_branch_gate, w_out, norm_ffn, w_router_grp, b_router_grp, w_router_exp, b_router_exp, w_exp_gate, w_exp_up, w_exp_down):
    b, s, d = x.shape
    x2 = x.reshape(b * s, d)
    per_layer = (norm_mix, w_in, fox_f_bias, fox_q_norm, fox_k_norm, gla_w_lr, gla_b_gate,
                 gla_out_norm, ssm_conv_w, ssm_conv_b, ssm_dt_bias, ssm_a_log, ssm_d,
                 ssm_out_norm, w_branch, b_branch_gate, w_out, norm_ffn, w_router_grp,
                 b_router_grp, w_router_exp, b_router_exp, w_exp_gate, w_exp_up, w_exp_down)
    for l in range(norm_mix.shape[0]):
        x2 = _layer(x2, b, s, *[p[l] for p in per_layer])
    return x2.reshape(b, s, d)
```

```python
import functools

import numpy as np
import jax
import jax.numpy as jnp
from jax import lax
from jax.experimental import pallas as pl
from jax.experimental.pallas import tpu as pltpu

F32 = jnp.float32
BF16 = jnp.bfloat16

D_MODEL = 1024
MIX_W = 512
EPS = 1e-6
FOX_HEADS = 8
FOX_HD = 64
FOX_PAIRS = FOX_HEADS // 2
GLA_HEADS = 4
GLA_DK = 64
GLA_DV = 128
GLA_RANK = 16
GLA_GATE_NORM = 16.0
GLA_CHUNK = 64
SSM_HEADS = 8
SSM_HD = 64
SSM_GROUPS = 2
SSM_STATE = 64
SSM_CONV = 4
N_EGROUPS = 4
EXP_PER_GROUP = 4
N_EXPERTS = 16
PAIRS_PER_GROUP = 6
D_EXPERT = 512

LANES = 128
NEG_BIG = -1e30
LOG2E = 1.4426950408889634
VMEM_LIMIT = 56 * 1024 * 1024

GATE_OFF, FQ_OFF, FK_OFF, FV_OFF = 0, 3072, 3584, 4096
GV_OFF, GR_OFF, SZ_OFF, SX_OFF = 4608, 5120, 5632, 6144
GQ_OFF, GK_OFF, SB_OFF, SC_OFF = 6656, 6912, 7168, 7296
BIG_COLS = 7424
FF_LANE, GLR_LANE, SDT_LANE = 0, 8, 24
_O_FQ, _O_FK, _O_FV, _O_FF = 0, 512, 1024, 1536
_O_GQ, _O_GK, _O_GV, _O_GR, _O_GLR = 1544, 1800, 2056, 2568, 3080
_O_SZ, _O_SX, _O_SB, _O_SC, _O_SDT, _O_GATE = 3096, 3608, 4120, 4248, 4376, 4384
RG_LANE, RE_LANE = 0, 8
ROUTE_ROWS = 32
XE_COLS = D_MODEL + LANES


def _split2(x):
    hi = x.astype(BF16)
    lo = (x - hi.astype(F32)).astype(BF16)
    return hi, lo


def _split3(x):
    x1 = x.astype(BF16)
    r = x - x1.astype(F32)
    x2 = r.astype(BF16)
    x3 = (r - x2.astype(F32)).astype(BF16)
    return x1, x2, x3


def _dot(a, b):
    return jnp.dot(a, b, preferred_element_type=F32)


def _dot_nt(a, b):
    return lax.dot_general(a, b, (((1,), (1,)), ((), ())), preferred_element_type=F32)


def _dot_tn(a, b):
    return lax.dot_general(a, b, (((0,), (0,)), ((), ())), preferred_element_type=F32)


def _dot3_left(m_bf16, x_f32):
    x1, x2, x3 = _split3(x_f32)
    return _dot(m_bf16, x1) + _dot(m_bf16, x2) + _dot(m_bf16, x3)


def _dot3_right(x_f32, m_bf16):
    x1, x2, x3 = _split3(x_f32)
    return _dot(x1, m_bf16) + _dot(x2, m_bf16) + _dot(x3, m_bf16)


def _dot_f32w(x_f32, w_hi, w_lo):
    x_hi, x_lo = _split2(x_f32)
    return _dot(x_hi, w_hi) + _dot(x_lo, w_hi) + _dot(x_hi, w_lo)


def _shr(x, pow2):
    return jnp.right_shift(x, pow2.bit_length() - 1)


def _log_sigmoid(x):
    return jnp.minimum(x, 0.0) - jnp.log(1.0 + jnp.exp(-jnp.abs(x)))


def _softplus(x):
    return jnp.maximum(x, 0.0) + jnp.log(1.0 + jnp.exp(-jnp.abs(x)))


def _sigmoid(x):
    return 0.5 * jnp.tanh(0.5 * x) + 0.5


def _silu(x):
    return x * _sigmoid(x)


def _params(sem):
    return pltpu.CompilerParams(dimension_semantics=sem, vmem_limit_bytes=VMEM_LIMIT)


def _inproj_kernel(x_ref, g_ref, w_ref, wsh_ref, wsl_ref, big_ref, small_ref, *, tn):
    x = x_ref[...]
    ms = jnp.mean(x * x, axis=-1, keepdims=True)
    h = x * lax.rsqrt(ms + EPS) * g_ref[...]
    hb = h.astype(BF16)
    for c in range(BIG_COLS // tn):
        cols = slice(c * tn, (c + 1) * tn)
        big_ref[:, cols] = _dot(hb, w_ref[:, cols]).astype(BF16)
    h_lo = (h - hb.astype(F32)).astype(BF16)
    wsh = wsh_ref[...]
    small_ref[...] = _dot(hb, wsh) + _dot(h_lo, wsh) + _dot(hb, wsl_ref[...])


def _inproj(x2, gain, w_big, ws_hi, ws_lo, *, tm=512, tn=256):
    t = x2.shape[0]
    return pl.pallas_call(
        functools.partial(_inproj_kernel, tn=tn),
        grid=(t // tm,),
        in_specs=[
            pl.BlockSpec((tm, D_MODEL), lambda i: (i, 0)),
            pl.BlockSpec((1, D_MODEL), lambda i: (0, 0)),
            pl.BlockSpec((D_MODEL, BIG_COLS), lambda i: (0, 0), pipeline_mode=pl.Buffered(1)),
            pl.BlockSpec((D_MODEL, LANES), lambda i: (0, 0)),
            pl.BlockSpec((D_MODEL, LANES), lambda i: (0, 0)),
        ],
        out_specs=[
            pl.BlockSpec((tm, BIG_COLS), lambda i: (i, 0)),
            pl.BlockSpec((tm, LANES), lambda i: (i, 0)),
        ],
        out_shape=[
            jax.ShapeDtypeStruct((t, BIG_COLS), BF16),
            jax.ShapeDtypeStruct((t, LANES), F32),
        ],
        compiler_params=_params(("arbitrary",)),
        name="inproj",
    )(x2, gain, w_big, ws_hi, ws_lo)


def _fox_consts(ts):
    ltri = np.tril(np.ones((ts, ts), np.float32))
    hsum = np.kron(np.eye(HSUM_W // FOX_HD, dtype=np.float32), np.ones((FOX_HD, FOX_HD), np.float32))
    sq = np.zeros((3, LANES, MIX_W), np.float32)
    sk = np.zeros((3, LANES, MIX_W), np.float32)
    oneq = np.zeros((1, MIX_W), np.float32)
    onek = np.zeros((1, MIX_W), np.float32)
    for h in range(FOX_HEADS):
        base = LANES * (h // 2) + 6 * (h % 2)
        for j in range(3):
            sq[j, FF_LANE + h, base + j] = 1.0
            sk[j, FF_LANE + h, base + 3 + j] = -1.0
            oneq[0, base + 3 + j] = 1.0
            onek[0, base + j] = 1.0
    saug = np.concatenate([sq.reshape(3 * LANES, MIX_W), sk.reshape(3 * LANES, MIX_W)], axis=1)
    return (jnp.asarray(ltri, BF16), jnp.asarray(hsum, BF16), jnp.asarray(saug, BF16),
            jnp.asarray(oneq), jnp.asarray(onek))


def _fox_prep_kernel(fq_ref, fk_ref, fv_ref, small_ref, fbias_ref, qg_ref, kg_ref, ltri_ref,
                     hsum_ref, saug_ref, oneq_ref, onek_ref, qt_ref, kf_ref, vt_ref, carry_ref,
                     *, ts):
    @pl.when(pl.program_id(1) == 0)
    def _():
        carry_ref[...] = jnp.zeros_like(carry_ref)

    lane = lax.broadcasted_iota(jnp.int32, (ts, LANES), 1)
    f = small_ref[0] + fbias_ref[...]
    ls = jnp.where(lane < FOX_HEADS, _log_sigmoid(f) * LOG2E, 0.0)
    parts = _dot(ltri_ref[...], jnp.concatenate(_split3(ls), axis=1))
    c = parts[:, 0:LANES] + parts[:, LANES:2 * LANES] + parts[:, 2 * LANES:] + carry_ref[0:1, :]
    carry_ref[...] = jnp.broadcast_to(c[ts - 1:ts, :], carry_ref.shape)
    aug = _dot(jnp.concatenate(_split3(c), axis=1), saug_ref[...])
    qaug = aug[:, 0:MIX_W] + oneq_ref[...]
    kaug = aug[:, MIX_W:] + onek_ref[...]

    hsum = hsum_ref[...]

    def head_norm(xb, gain):
        x = xb.astype(F32)
        s_hi, s_lo = _split2(x * x)
        halves = []
        for half in range(MIX_W // HSUM_W):
            cols = slice(HSUM_W * half, HSUM_W * (half + 1))
            halves.append(_dot(s_hi[:, cols], hsum) + _dot(s_lo[:, cols], hsum))
        ss = jnp.concatenate(halves, axis=1)
        return x * lax.rsqrt(ss * (1.0 / FOX_HD) + EPS) * gain

    qn = head_norm(fq_ref[0], qg_ref[...]) * (FOX_HD ** -0.5 * LOG2E)
    kn = head_norm(fk_ref[0], kg_ref[...])
    v = fv_ref[0].astype(F32)
    for p in range(FOX_PAIRS):
        src = slice(LANES * p, LANES * (p + 1))
        dst_x = slice(2 * LANES * p, 2 * LANES * p + LANES)
        dst_a = slice(2 * LANES * p + LANES, 2 * LANES * (p + 1))
        kf_ref[0, :, dst_x] = kn[:, src].astype(BF16)
        kf_ref[0, :, dst_a] = kaug[:, src].astype(BF16)
        qt_ref[0, p, 0, 0:LANES, :] = qn[:, src].T.astype(BF16)
        qt_ref[0, p, 0, LANES:2 * LANES, :] = qaug[:, src].T.astype(BF16)
        vt_ref[0, p, 0] = v[:, src].T.astype(BF16)


def _fox_prep(big3, small3, fbias, qgain, kgain, *, ts):
    b, s, _ = big3.shape
    consts = _fox_consts(ts)
    const_specs = [
        pl.BlockSpec((ts, ts), lambda bi, i: (0, 0)),
        pl.BlockSpec((HSUM_W, HSUM_W), lambda bi, i: (0, 0)),
        pl.BlockSpec((3 * LANES, 2 * MIX_W), lambda bi, i: (0, 0)),
        pl.BlockSpec((1, MIX_W), lambda bi, i: (0, 0)),
        pl.BlockSpec((1, MIX_W), lambda bi, i: (0, 0)),
    ]
    return pl.pallas_call(
        functools.partial(_fox_prep_kernel, ts=ts),
        grid=(b, s // ts),
        in_specs=[
            pl.BlockSpec((1, ts, MIX_W), lambda bi, i: (bi, i, FQ_OFF // MIX_W)),
            pl.BlockSpec((1, ts, MIX_W), lambda bi, i: (bi, i, FK_OFF // MIX_W)),
            pl.BlockSpec((1, ts, MIX_W), lambda bi, i: (bi, i, FV_OFF // MIX_W)),
            pl.BlockSpec((1, ts, LANES), lambda bi, i: (bi, i, 0)),
            pl.BlockSpec((1, LANES), lambda bi, i: (0, 0)),
            pl.BlockSpec((1, MIX_W), lambda bi, i: (0, 0)),
            pl.BlockSpec((1, MIX_W), lambda bi, i: (0, 0)),
        ] + const_specs,
        out_specs=[
            pl.BlockSpec((1, FOX_PAIRS, 1, 2 * LANES, ts), lambda bi, i: (bi, 0, i, 0, 0)),
            pl.BlockSpec((1, ts, 2 * MIX_W), lambda bi, i: (bi, i, 0)),
            pl.BlockSpec((1, FOX_PAIRS, 1, LANES, ts), lambda bi, i: (bi, 0, i, 0, 0)),
        ],
        out_shape=[
            jax.ShapeDtypeStruct((b, FOX_PAIRS, s // ts, 2 * LANES, ts), BF16),
            jax.ShapeDtypeStruct((b, s, 2 * MIX_W), BF16),
            jax.ShapeDtypeStruct((b, FOX_PAIRS, s // ts, LANES, ts), BF16),
        ],
        scratch_shapes=[pltpu.VMEM((8, LANES), F32)],
        compiler_params=_params(("arbitrary", "arbitrary")),
        name="fox_prep",
    )(big3, big3, big3, small3, fbias, qgain, kgain, *consts)


FOX_NOSHIFT_BOUND = 40.0
FOX_TILE = 512
HSUM_W = 256
FOX_UNROLL = 8


def _fox_attn_kernel(qt_ref, k_ref, vt_ref, o_ref, acc_ref, l_ref, m_ref, *, tq, online):
    i = pl.program_id(2)
    qt = qt_ref[0, 0, 0]
    qrow = lax.broadcasted_iota(jnp.int32, (2 * LANES, 1), 0)
    in_a = (qrow < FOX_HD) | ((qrow >= LANES) & (qrow < LANES + 6))
    in_b = ((qrow >= FOX_HD) & (qrow < LANES)) | ((qrow >= LANES + 6) & (qrow < LANES + 12))
    zero = jnp.zeros_like(qt)
    qt_heads = (jnp.where(in_a, qt, zero), jnp.where(in_b, qt, zero))

    acc_ref[...] = jnp.zeros_like(acc_ref)
    l_ref[...] = jnp.zeros_like(l_ref)
    if online:
        m_ref[...] = jnp.full_like(m_ref, NEG_BIG)

    def scores(j, a, diag):
        k = k_ref[0, pl.ds(pl.multiple_of(j * tq, tq), tq), :]
        st = _dot(k, qt_heads[a])
        if diag:
            krow = lax.broadcasted_iota(jnp.int32, (tq, tq), 0)
            qcol = lax.broadcasted_iota(jnp.int32, (tq, tq), 1)
            st = jnp.where(krow <= qcol, st, NEG_BIG)
        return st

    def accumulate(j, a, st):
        vt_a = vt_ref[0, 0, j, FOX_HD * a:FOX_HD * (a + 1), :]
        if online:
            m_prev = m_ref[a]
            m_new = jnp.maximum(m_prev, jnp.max(st, axis=0, keepdims=True))
            alpha = jnp.exp2(m_prev - m_new)
            m_ref[a] = m_new
            pt = jnp.exp2(st - m_new)
            l_ref[a] = alpha * l_ref[a] + jnp.sum(pt.reshape(tq // 8, 8, tq), axis=0)
            acc_ref[a] = alpha * acc_ref[a] + _dot(vt_a, pt.astype(BF16))
        else:
            pt = jnp.exp2(st)
            l_ref[a] += jnp.sum(pt.reshape(tq // 8, 8, tq), axis=0)
            acc_ref[a] += _dot(vt_a, pt.astype(BF16))

    def run(units):
        st = scores(*units[0])
        for u, unit in enumerate(units):
            st_next = scores(*units[u + 1]) if u + 1 < len(units) else None
            accumulate(unit[0], unit[1], st)
            st = st_next

    def units(first_block, n_regular, with_diag):
        blocks = [(first_block + d, False) for d in range(n_regular)]
        if with_diag:
            blocks.append((first_block + n_regular, True))
        return [(j, a, diag) for j, diag in blocks for a in range(2)]

    def body(jj, carry):
        run(units(FOX_UNROLL * jj, FOX_UNROLL, False))
        return carry

    lax.fori_loop(0, i // FOX_UNROLL, body, 0)
    for rem in range(FOX_UNROLL):
        @pl.when(i % FOX_UNROLL == rem)
        def _():
            run(units(i - rem, rem, True))

    halves = [acc_ref[a] * (1.0 / jnp.sum(l_ref[a], axis=0, keepdims=True)) for a in range(2)]
    o_ref[0] = jnp.concatenate(halves, axis=0).T.astype(BF16)


def _fox_attn(qt, kf, vt, logit_bound):
    b, _, nq, _, tq = qt.shape
    s = nq * tq

    def call(online, name):
        return pl.pallas_call(
            functools.partial(_fox_attn_kernel, tq=tq, online=online),
            grid=(b, FOX_PAIRS, nq),
            in_specs=[
                pl.BlockSpec((1, 1, 1, 2 * LANES, tq), lambda bi, p, i: (bi, p, i, 0, 0)),
                pl.BlockSpec((1, s, 2 * LANES), lambda bi, p, i: (bi, 0, p)),
                pl.BlockSpec((1, 1, nq, LANES, tq), lambda bi, p, i: (bi, p, 0, 0, 0)),
            ],
            out_specs=pl.BlockSpec((1, tq, LANES), lambda bi, p, i: (bi, i, p)),
            out_shape=jax.ShapeDtypeStruct((b, s, MIX_W), BF16),
            scratch_shapes=[
                pltpu.VMEM((2, FOX_HD, tq), F32),
                pltpu.VMEM((2, 8, tq), F32),
                pltpu.VMEM((2, 1, tq), F32),
            ],
            compiler_params=_params(("arbitrary", "arbitrary", "arbitrary")),
            name=name,
        )(qt, kf, vt)

    return lax.cond(logit_bound < FOX_NOSHIFT_BOUND,
                    lambda: call(False, "fox_attn"), lambda: call(True, "fox_attn_online"))


def _gla_consts(r):
    idx = np.arange(r)
    same = (idx[:, None] // GLA_CHUNK) == (idx[None, :] // GLA_CHUNK)
    lblk = (same & (idx[None, :] <= idx[:, None])).astype(np.float32)
    ablk = same.astype(np.float32)
    return jnp.asarray(lblk, BF16), jnp.asarray(ablk, BF16)


def _gla_kernel(q_ref, k_ref, v_ref, r_ref, small_ref, wlh_ref, wll_ref, bg_ref, lblk_ref,
                ablk_ref, gain_ref, o_ref, st_ref, oacc_ref, *, r):
    @pl.when(pl.program_id(1) == 0)
    def _():
        st_ref[...] = jnp.zeros_like(st_ref)

    kw = GLA_HEADS * GLA_DK
    gate = _dot_f32w(small_ref[0], wlh_ref[...], wll_ref[...]) + bg_ref[...]
    log_a = _log_sigmoid(gate) * (1.0 / GLA_GATE_NORM)
    a1, a2, a3 = _split3(log_a)
    lblk = lblk_ref[...]
    ablk = ablk_ref[...]
    bcum = _dot(lblk, a1) + _dot(lblk, a2) + _dot(lblk, a3)
    btot = _dot(ablk, a1) + _dot(ablk, a2) + _dot(ablk, a3)
    q = q_ref[0].astype(F32) * (GLA_DK ** -0.5)
    k = k_ref[0].astype(F32)
    q_dec = (q * jnp.exp(bcum)).astype(BF16)
    k_dec = (k * jnp.exp(-bcum)).astype(BF16)
    k_end_t = (k * jnp.exp(btot - bcum)).T.astype(BF16)
    d_tot_t = jnp.exp(btot).T
    v = v_ref[0]

    row = lax.broadcasted_iota(jnp.int32, (r, r), 0)
    col = lax.broadcasted_iota(jnp.int32, (r, r), 1)
    keep = (_shr(row, GLA_CHUNK) == _shr(col, GLA_CHUNK)) & (col <= row)
    klane = lax.broadcasted_iota(jnp.int32, (1, kw), 1)
    qzero = jnp.zeros_like(q_dec)
    q_heads = []
    for h in range(GLA_HEADS):
        in_h = (klane >= GLA_DK * h) & (klane < GLA_DK * (h + 1))
        q_heads.append(jnp.where(in_h, q_dec, qzero))
        att = _dot_nt(q_heads[h], k_dec)
        att = jnp.where(keep, att, 0.0).astype(BF16)
        vcols = slice(GLA_DV * h, GLA_DV * (h + 1))
        oacc_ref[:, vcols] = _dot(att, v[:, vcols])

    tlane = lax.broadcasted_iota(jnp.int32, (1, r), 1)
    kzero = jnp.zeros_like(k_end_t)
    for c in range(r // GLA_CHUNK):
        rows = slice(GLA_CHUNK * c, GLA_CHUNK * (c + 1))
        st = st_ref[...]
        st_b = st.astype(BF16)
        k_chunk = jnp.where(_shr(tlane, GLA_CHUNK) == c, k_end_t, kzero)
        updates = []
        for h in range(GLA_HEADS):
            vcols = slice(GLA_DV * h, GLA_DV * (h + 1))
            oacc_ref[rows, vcols] += _dot(q_heads[h][rows], st_b)
            updates.append(_dot(k_chunk[GLA_DK * h:GLA_DK * (h + 1), :], v[:, vcols]))
        decay = d_tot_t[:, GLA_CHUNK * c:GLA_CHUNK * c + 1]
        st_ref[...] = decay * st + jnp.concatenate(updates, axis=0)

    gain = gain_ref[...]
    gr = r_ref[0].astype(F32)
    for h in range(GLA_HEADS):
        vcols = slice(GLA_DV * h, GLA_DV * (h + 1))
        o = oacc_ref[:, vcols]
        ms = jnp.mean(o * o, axis=-1, keepdims=True)
        o_ref[0, :, vcols] = (o * lax.rsqrt(ms + EPS) * gain * _silu(gr[:, vcols])).astype(BF16)


def _gla(big3, small3, wl_hi, wl_lo, bgate, gain, *, r=256):
    b, s, _ = big3.shape
    r = min(r, s)
    kw = GLA_HEADS * GLA_DK
    lblk, ablk = _gla_consts(r)
    return pl.pallas_call(
        functools.partial(_gla_kernel, r=r),
        grid=(b, s // r),
        in_specs=[
            pl.BlockSpec((1, r, kw), lambda bi, i: (bi, i, GQ_OFF // kw)),
            pl.BlockSpec((1, r, kw), lambda bi, i: (bi, i, GK_OFF // kw)),
            pl.BlockSpec((1, r, MIX_W), lambda bi, i: (bi, i, GV_OFF // MIX_W)),
            pl.BlockSpec((1, r, MIX_W), lambda bi, i: (bi, i, GR_OFF // MIX_W)),
            pl.BlockSpec((1, r, LANES), lambda bi, i: (bi, i, 0)),
            pl.BlockSpec((LANES, kw), lambda bi, i: (0, 0)),
            pl.BlockSpec((LANES, kw), lambda bi, i: (0, 0)),
            pl.BlockSpec((1, kw), lambda bi, i: (0, 0)),
            pl.BlockSpec((r, r), lambda bi, i: (0, 0)),
            pl.BlockSpec((r, r), lambda bi, i: (0, 0)),
            pl.BlockSpec((1, GLA_DV), lambda bi, i: (0, 0)),
        ],
        out_specs=pl.BlockSpec((1, r, MIX_W), lambda bi, i: (bi, i, 0)),
        out_shape=jax.ShapeDtypeStruct((b, s, MIX_W), BF16),
        scratch_shapes=[pltpu.VMEM((kw, GLA_DV), F32), pltpu.VMEM((r, MIX_W), F32)],
        compiler_params=_params(("arbitrary", "arbitrary")),
        name="gla",
    )(big3, big3, big3, big3, small3, wl_hi, wl_lo, bgate, lblk, ablk, gain)


SSD_HALO = 16
SSD_CONV_W = MIX_W + 2 * SSM_GROUPS * SSM_STATE
SSD_STATE_W = 256


def _ssd_consts(r):
    expand = np.zeros((LANES, MIX_W), np.float32)
    for h in range(SSM_HEADS):
        expand[SDT_LANE + h, SSM_HD * h:SSM_HD * (h + 1)] = 1.0
    rep = np.zeros((LANES, MIX_W), np.float32)
    for h in range(SSM_HEADS):
        g = h // (SSM_HEADS // SSM_GROUPS)
        for n in range(SSM_STATE):
            rep[SSM_STATE * g + n, SSM_HD * h + n] = 1.0
    ltri = np.tril(np.ones((r, r), np.float32))
    return (jnp.asarray(expand, BF16), jnp.asarray(rep, BF16), jnp.asarray(ltri, BF16),
            jnp.asarray(ltri.T, BF16))


def _ssd_kernel(z_ref, x_ref, xp_ref, b_ref, bp_ref, c_ref, cp_ref, small_ref, cw_ref, cb_ref,
                dtb_ref, alog_ref, dtbc_ref, alogc_ref, dskip_ref, onorm_ref, expand_ref,
                rep_ref, ltri_ref, utri_ref, o_ref, sw_ref, ext_ref, y_ref, *, r):
    first = pl.program_id(1) == 0

    @pl.when(first)
    def _():
        sw_ref[...] = jnp.zeros_like(sw_ref)

    keep_prev = jnp.where(first, 0.0, 1.0)
    ext_ref[0:SSD_HALO, 0:MIX_W] = xp_ref[0].astype(F32) * keep_prev
    ext_ref[0:SSD_HALO, MIX_W:MIX_W + LANES] = bp_ref[0].astype(F32) * keep_prev
    ext_ref[0:SSD_HALO, MIX_W + LANES:SSD_CONV_W] = cp_ref[0].astype(F32) * keep_prev
    ext_ref[SSD_HALO:, 0:MIX_W] = x_ref[0].astype(F32)
    ext_ref[SSD_HALO:, MIX_W:MIX_W + LANES] = b_ref[0].astype(F32)
    ext_ref[SSD_HALO:, MIX_W + LANES:SSD_CONV_W] = c_ref[0].astype(F32)
    conv = cb_ref[...] + cw_ref[SSM_CONV - 1:SSM_CONV, :] * ext_ref[SSD_HALO:, :]
    for back in range(1, SSM_CONV):
        tap = SSM_CONV - 1 - back
        conv = conv + cw_ref[tap:tap + 1, :] * ext_ref[pl.ds(SSD_HALO - back, r), :]
    xbc = _silu(conv)
    xs = xbc[:, 0:MIX_W]
    bm = xbc[:, MIX_W:MIX_W + LANES].astype(BF16)
    cm = xbc[:, MIX_W + LANES:SSD_CONV_W].astype(BF16)

    sm = small_ref[0]
    dt = _dot3_right(_softplus(sm + dtb_ref[...]), expand_ref[...])
    a_neg = -jnp.exp(alog_ref[...])
    acs = _dot3_left(ltri_ref[...], dt * a_neg)
    acs_last = acs[r - 1:r, :]
    sm_t = sm.T
    dt_t = _softplus(sm_t[SDT_LANE:SDT_LANE + SSM_HEADS, :] + dtbc_ref[:, 0:1])
    acs_t = _dot3_right(dt_t * (-jnp.exp(alogc_ref[:, 0:1])), utri_ref[...])

    xdt = (xs * dt).astype(BF16)
    row = lax.broadcasted_iota(jnp.int32, (r, r), 0)
    col = lax.broadcasted_iota(jnp.int32, (r, r), 1)
    causal = col <= row
    glane = lax.broadcasted_iota(jnp.int32, (1, LANES), 1)
    first_half = glane < SSM_STATE
    czero = jnp.zeros_like(cm)
    hpg = SSM_HEADS // SSM_GROUPS
    for g in range(SSM_GROUPS):
        cg = jnp.where(first_half if g == 0 else ~first_half, cm, czero)
        cb_g = _dot_nt(cg, bm)
        for pair in range(hpg // 2):
            p = g * (hpg // 2) + pair
            xp = xdt[:, LANES * p:LANES * (p + 1)]
            xzero = jnp.zeros_like(xp)
            acc = None
            for a in range(2):
                h = 2 * p + a
                dmat = acs[:, SSM_HD * h:SSM_HD * h + 1] - acs_t[h:h + 1, :]
                sc = (cb_g * jnp.exp(jnp.where(causal, dmat, NEG_BIG))).astype(BF16)
                xh = jnp.where(first_half if a == 0 else ~first_half, xp, xzero)
                contrib = _dot(sc, xh)
                acc = contrib if acc is None else acc + contrib
            y_ref[:, LANES * p:LANES * (p + 1)] = acc

    rep = rep_ref[...]
    cw = (_dot(cm, rep) * jnp.exp(acs)).astype(BF16)
    bw = (_dot(bm, rep) * jnp.exp(acs_last - acs)).astype(BF16)
    srow = lax.broadcasted_iota(jnp.int32, (SSD_STATE_W, SSD_STATE_W), 0)
    scol = lax.broadcasted_iota(jnp.int32, (SSD_STATE_W, SSD_STATE_W), 1)
    same_head = _shr(srow, SSM_STATE) == _shr(scol, SSM_HD)
    y_inter = []
    for half in range(MIX_W // SSD_STATE_W):
        cols = slice(SSD_STATE_W * half, SSD_STATE_W * (half + 1))
        sw = sw_ref[half]
        y_inter.append(_dot(cw[:, cols], sw.astype(BF16)))
        upd = _dot_tn(bw[:, cols], xdt[:, cols])
        sw_ref[half] = sw * jnp.exp(acs_last[:, cols]) + jnp.where(same_head, upd, 0.0)
    y = y_ref[...] + jnp.concatenate(y_inter, axis=1) + dskip_ref[...] * xs

    y = y * _silu(z_ref[0].astype(F32))
    gw = MIX_W // SSM_GROUPS
    for g in range(SSM_GROUPS):
        cols = slice(gw * g, gw * (g + 1))
        yg = y[:, cols]
        ms = jnp.mean(yg * yg, axis=-1, keepdims=True)
        o_ref[0, :, cols] = (yg * lax.rsqrt(ms + EPS) * onorm_ref[:, cols]).astype(BF16)


def _ssd(big3, small3, conv_w, conv_b, dtb_w, alog_w, dtb_c, alog_c, dskip_w, onorm, *, r=256):
    b, s, _ = big3.shape
    r = min(r, s)
    consts = _ssd_consts(r)
    hb = r // SSD_HALO

    def cur(width, off):
        return pl.BlockSpec((1, r, width), lambda bi, i: (bi, i, off // width))

    def prev(width, off):
        return pl.BlockSpec((1, SSD_HALO, width),
                            lambda bi, i: (bi, jnp.maximum(i * hb - 1, 0), off // width))

    def whole(shape):
        return pl.BlockSpec(shape, lambda bi, i: (0,) * len(shape))

    return pl.pallas_call(
        functools.partial(_ssd_kernel, r=r),
        grid=(b, s // r),
        in_specs=[
            cur(MIX_W, SZ_OFF),
            cur(MIX_W, SX_OFF), prev(MIX_W, SX_OFF),
            cur(LANES, SB_OFF), prev(LANES, SB_OFF),
            cur(LANES, SC_OFF), prev(LANES, SC_OFF),
            pl.BlockSpec((1, r, LANES), lambda bi, i: (bi, i, 0)),
            whole((SSM_CONV, SSD_CONV_W)), whole((1, SSD_CONV_W)),
            whole((1, LANES)), whole((1, MIX_W)),
            whole((SSM_HEADS, LANES)), whole((SSM_HEADS, LANES)),
            whole((1, MIX_W)), whole((1, MIX_W)),
            whole((LANES, MIX_W)), whole((LANES, MIX_W)), whole((r, r)), whole((r, r)),
        ],
        out_specs=pl.BlockSpec((1, r, MIX_W), lambda bi, i: (bi, i, 0)),
        out_shape=jax.ShapeDtypeStruct((b, s, MIX_W), BF16),
        scratch_shapes=[
            pltpu.VMEM((MIX_W // SSD_STATE_W, SSD_STATE_W, SSD_STATE_W), F32),
            pltpu.VMEM((r + SSD_HALO, SSD_CONV_W), F32),
            pltpu.VMEM((r, MIX_W), F32),
        ],
        compiler_params=_params(("arbitrary", "arbitrary")),
        name="ssd",
    )(big3, big3, big3, big3, big3, big3, big3, small3, conv_w, conv_b, dtb_w, alog_w,
      dtb_c, alog_c, dskip_w, onorm, *consts)


def _merge_kernel(oa_ref, ob_ref, oc_ref, gate_ref, x_ref, wb_ref, bgate_ref, wo_ref, nffn_ref,
                  wrh_ref, wrl_ref, br_ref, ustrict_ref, xe_ref, route_ref, counts_ref, *, tm):
    @pl.when(pl.program_id(0) == 0)
    def _():
        counts_ref[...] = jnp.zeros_like(counts_ref)

    mixed = None
    for ridx, o_ref in enumerate((oa_ref, ob_ref, oc_ref)):
        cols = slice(D_MODEL * ridx, D_MODEL * (ridx + 1))
        gate = _sigmoid(gate_ref[:, cols].astype(F32) + bgate_ref[ridx:ridx + 1, :])
        term = gate * _dot(o_ref[...], wb_ref[ridx])
        mixed = term if mixed is None else mixed + term
    xn = x_ref[...] + _dot(mixed.astype(BF16), wo_ref[...])
    xe_ref[:, 0:D_MODEL] = xn
    ms = jnp.mean(xn * xn, axis=-1, keepdims=True)
    h = xn * lax.rsqrt(ms + EPS) * nffn_ref[...]

    lt = (_dot_f32w(h, wrh_ref[...], wrl_ref[...]) + br_ref[...]).T
    grow = lax.broadcasted_iota(jnp.int32, (8, tm), 0)
    grow_f = grow.astype(F32)
    far_row = float(LANES)
    gl = jnp.where(grow < N_EGROUPS, lt[RG_LANE:RG_LANE + 8, :], NEG_BIG)
    gmax = jnp.max(gl, axis=0, keepdims=True)
    g_w = 1.0 / jnp.sum(jnp.exp(gl - gmax), axis=0, keepdims=True)
    g_sel = jnp.min(jnp.where(gl == gmax, grow_f, far_row), axis=0, keepdims=True)
    e16 = lt[RE_LANE:RE_LANE + N_EXPERTS, :]
    erow = lax.broadcasted_iota(jnp.int32, (N_EXPERTS, tm), 0)
    erow_f = erow.astype(F32)
    in_grp = _shr(erow, EXP_PER_GROUP).astype(F32) == g_sel
    el = jnp.where(in_grp, e16, NEG_BIG)
    e1 = jnp.max(el, axis=0, keepdims=True)
    i1 = jnp.min(jnp.where(in_grp & (el == e1), erow_f, far_row), axis=0, keepdims=True)
    rest = in_grp & (erow_f != i1)
    el2 = jnp.where(rest, e16, NEG_BIG)
    e2 = jnp.max(el2, axis=0, keepdims=True)
    i2 = jnp.min(jnp.where(rest & (el2 == e2), erow_f, far_row), axis=0, keepdims=True)
    ratio = jnp.exp(e2 - e1)
    w1 = g_w / (1.0 + ratio)
    w2 = w1 * ratio
    comb_t = jnp.where(erow_f == i1, w1, 0.0) + jnp.where(erow_f == i2, w2, 0.0)
    comb_t = jnp.concatenate([jnp.zeros((RE_LANE, tm), F32), comb_t,
                              jnp.zeros((LANES - RE_LANE - N_EXPERTS, tm), F32)], axis=0)
    xe_ref[:, D_MODEL:] = comb_t.T

    first_row = EXP_PER_GROUP * g_sel
    lo = jnp.minimum(i1, i2) - first_row
    hi = jnp.maximum(i1, i2) - first_row
    cls = PAIRS_PER_GROUP * g_sel + lo * (7.0 - lo) * 0.5 + (hi - lo - 1.0)
    crow_f = lax.broadcasted_iota(jnp.int32, (ROUTE_ROWS, tm), 0).astype(F32)
    is_cls = crow_f == cls
    onehot = jnp.where(is_cls, 1.0, 0.0)
    before = _dot(onehot.astype(BF16), ustrict_ref[...]) + counts_ref[:, 0:1]
    rank = jnp.sum(jnp.where(is_cls, before, 0.0), axis=0, keepdims=True)
    counts_ref[...] = counts_ref[...] + jnp.sum(onehot, axis=1, keepdims=True)
    rank_hi = jnp.floor(rank * (1.0 / LANES))
    rank_lo = rank - rank_hi * LANES
    route_ref[...] = jnp.where(grow == 0, cls, jnp.where(grow == 1, rank_hi,
                                                         jnp.where(grow == 2, rank_lo, 0.0)))


def _merge(oa, ob, oc, big, x2, wb, bgate, wo, nffn, wr_hi, wr_lo, br, *, tm=512):
    t = x2.shape[0]
    tm = min(tm, t)
    ustrict = jnp.asarray(np.triu(np.ones((tm, tm), np.float32), 1), BF16)

    def whole(shape):
        return pl.BlockSpec(shape, lambda i: (0,) * len(shape))

    return pl.pallas_call(
        functools.partial(_merge_kernel, tm=tm),
        grid=(t // tm,),
        in_specs=[
            pl.BlockSpec((tm, MIX_W), lambda i: (i, 0)),
            pl.BlockSpec((tm, MIX_W), lambda i: (i, 0)),
            pl.BlockSpec((tm, MIX_W), lambda i: (i, 0)),
            pl.BlockSpec((tm, 3 * D_MODEL), lambda i: (i, 0)),
            pl.BlockSpec((tm, D_MODEL), lambda i: (i, 0)),
            whole((3, MIX_W, D_MODEL)), whole((3, D_MODEL)), whole((D_MODEL, D_MODEL)),
            whole((1, D_MODEL)), whole((D_MODEL, LANES)), whole((D_MODEL, LANES)),
            whole((1, LANES)), whole((tm, tm)),
        ],
        out_specs=[
            pl.BlockSpec((tm, XE_COLS), lambda i: (i, 0)),
            pl.BlockSpec((8, tm), lambda i: (0, i)),
            pl.BlockSpec((ROUTE_ROWS, LANES), lambda i: (0, 0)),
        ],
        out_shape=[
            jax.ShapeDtypeStruct((t, XE_COLS), F32),
            jax.ShapeDtypeStruct((8, t), F32),
            jax.ShapeDtypeStruct((ROUTE_ROWS, LANES), F32),
        ],
        compiler_params=_params(("arbitrary",)),
        name="merge",
    )(oa, ob, oc, big, x2, wb, bgate, wo, nffn, wr_hi, wr_lo, br, ustrict)


MOE_TILE = 256
ROW_DMA_TILE = 512


def _row_dma(pos_ref, near_ref, far_ref, sem, r, to_far):
    near, far = near_ref.at[pl.ds(r, 1)], far_ref.at[pl.ds(pos_ref[0, 0, r], 1)]
    return pltpu.make_async_copy(near, far, sem) if to_far else pltpu.make_async_copy(far, near, sem)


def _row_burst(pos_ref, near_ref, far_ref, sem, tm, to_far):
    for r in range(tm):
        _row_dma(pos_ref, near_ref, far_ref, sem, r, to_far).start(priority=r % 2)

    def drain(r, carry):
        _row_dma(pos_ref, near_ref, far_ref, sem, r, to_far).wait()
        return carry

    lax.fori_loop(0, tm, drain, 0, unroll=8)


def _row_gather_kernel(pos_ref, src_ref, o_ref, sem, *, tm):
    _row_burst(pos_ref, o_ref, src_ref, sem, tm, to_far=False)


def _row_scatter_kernel(pad_ref, pos_ref, src_ref, dst_ref, zeros_ref, sem, *, tm, pad_rows):
    @pl.when(pl.program_id(0) == 0)
    def _():
        zeros_ref[...] = jnp.zeros_like(zeros_ref)
        n_cls = pad_ref.shape[0] - 1

        def blank(c):
            start = pl.multiple_of(pad_ref[c], 8)
            return pltpu.make_async_copy(zeros_ref, dst_ref.at[pl.ds(start, pad_rows)], sem)

        for c in range(n_cls):
            blank(c).start()
        for c in range(n_cls):
            blank(c).wait()

        tile = pad_rows - 8

        def blank_tile(j):
            start = pl.multiple_of(j * tile, tile)
            return pltpu.make_async_copy(zeros_ref.at[pl.ds(0, tile)], dst_ref.at[pl.ds(start, tile)], sem)

        def start_tile(j, carry):
            blank_tile(j).start()
            return carry

        def wait_tile(j, carry):
            blank_tile(j).wait()
            return carry

        first_unused, n_tiles = pad_ref[n_cls] // tile, dst_ref.shape[0] // tile
        lax.fori_loop(first_unused, n_tiles, start_tile, 0)
        lax.fori_loop(first_unused, n_tiles, wait_tile, 0)

    _row_burst(pos_ref, src_ref, dst_ref, sem, tm, to_far=True)


def _row_gather(pos, src, out_rows):
    tm = min(ROW_DMA_TILE, out_rows)
    width = src.shape[1]
    return pl.pallas_call(
        functools.partial(_row_gather_kernel, tm=tm),
        grid=(out_rows // tm,),
        in_specs=[pl.BlockSpec((1, 1, tm), lambda i: (i, 0, 0), memory_space=pltpu.SMEM),
                  pl.BlockSpec(memory_space=pl.ANY)],
        out_specs=pl.BlockSpec((tm, width), lambda i: (i, 0)),
        out_shape=jax.ShapeDtypeStruct((out_rows, width), src.dtype),
        scratch_shapes=[pltpu.SemaphoreType.DMA(())],
        compiler_params=_params(("arbitrary",)),
        name="moe_gather",
    )(pos.reshape(out_rows // tm, 1, tm), src)


def _row_scatter(pos, pad_start, src, out_rows, pad_rows):
    t, width = src.shape
    tm = min(ROW_DMA_TILE, t)
    grid_spec = pltpu.PrefetchScalarGridSpec(
        num_scalar_prefetch=1,
        grid=(t // tm,),
        in_specs=[pl.BlockSpec((1, 1, tm), lambda i, pad: (i, 0, 0), memory_space=pltpu.SMEM),
                  pl.BlockSpec((tm, width), lambda i, pad: (i, 0))],
        out_specs=pl.BlockSpec(memory_space=pl.ANY),
        scratch_shapes=[pltpu.VMEM((pad_rows, width), src.dtype), pltpu.SemaphoreType.DMA(())],
    )
    return pl.pallas_call(
        functools.partial(_row_scatter_kernel, tm=tm, pad_rows=pad_rows),
        grid_spec=grid_spec,
        out_shape=jax.ShapeDtypeStruct((out_rows, width), src.dtype),
        compiler_params=_params(("arbitrary",)),
        name="moe_scatter",
    )(pad_start, pos.reshape(t // tm, 1, tm), src)


def _moe_sorted_kernel(ea_ref, eb_ref, nused_ref, xs_ref, nffn_ref, wga_ref, wua_ref, wda_ref,
                       wgb_ref, wub_ref, wdb_ref, o_ref, *, tm):
    i = pl.program_id(0)

    @pl.when(i >= nused_ref[0])
    def _():
        o_ref[...] = jnp.zeros_like(o_ref)


    @pl.when(i < nused_ref[0])
    def _():
        xn = xs_ref[:, 0:D_MODEL]
        comb = xs_ref[:, D_MODEL:]
        ms = jnp.mean(xn * xn, axis=-1, keepdims=True)
        h = (xn * lax.rsqrt(ms + EPS) * nffn_ref[...]).astype(BF16)
        lane = lax.broadcasted_iota(jnp.int32, (tm, LANES), 1)
        out = xn
        for e_ref, wg_ref, wu_ref, wd_ref in ((ea_ref, wga_ref, wua_ref, wda_ref),
                                              (eb_ref, wgb_ref, wub_ref, wdb_ref)):
            w = jnp.sum(jnp.where(lane == e_ref[i] + RE_LANE, comb, 0.0), axis=-1, keepdims=True)
            hid = _silu(_dot(h, wg_ref[0])) * _dot(h, wu_ref[0])
            out = out + w * _dot(hid.astype(BF16), wd_ref[0])
        o_ref[...] = out


def _moe_sorted(ea, eb, nused, xs, nffn, wg, wu, wd, *, tm):
    n_tiles = xs.shape[0] // tm

    def w_in(which):
        return pl.BlockSpec((1, D_MODEL, D_EXPERT), lambda i, ea, eb, nu: ((ea, eb)[which][i], 0, 0))

    def w_out(which):
        return pl.BlockSpec((1, D_EXPERT, D_MODEL), lambda i, ea, eb, nu: ((ea, eb)[which][i], 0, 0))

    grid_spec = pltpu.PrefetchScalarGridSpec(
        num_scalar_prefetch=3,
        grid=(n_tiles,),
        in_specs=[
            pl.BlockSpec((tm, XE_COLS), lambda i, ea, eb, nu: (jnp.minimum(i, nu[0] - 1), 0)),
            pl.BlockSpec((1, D_MODEL), lambda i, ea, eb, nu: (0, 0)),
            w_in(0), w_in(0), w_out(0), w_in(1), w_in(1), w_out(1),
        ],
        out_specs=pl.BlockSpec((tm, D_MODEL), lambda i, ea, eb, nu: (i, 0)),
    )
    return pl.pallas_call(
        functools.partial(_moe_sorted_kernel, tm=tm),
        grid_spec=grid_spec,
        out_shape=jax.ShapeDtypeStruct((n_tiles * tm, D_MODEL), F32),
        compiler_params=_params(("arbitrary",)),
        name="moe",
    )(ea, eb, nused, xs, nffn, wg, wu, wd, wg, wu, wd)


_PAIR_LO = np.array([0, 0, 0, 1, 1, 2], np.int32)
_PAIR_HI = np.array([1, 2, 3, 2, 3, 3], np.int32)


def _moe(xe, route, counts, nffn, wg, wu, wd):
    t = xe.shape[0]
    tm = min(MOE_TILE, t)
    n_cls = N_EGROUPS * PAIRS_PER_GROUP
    n_tiles = t // tm + n_cls
    cnt = counts[:n_cls, 0].astype(jnp.int32)
    tiles = (cnt + tm - 1) // tm
    tile_end = jnp.cumsum(tiles)
    n_used = tile_end[-1]
    cls_base = (tile_end - tiles) * tm
    tile_idx = jnp.minimum(jnp.arange(n_tiles), n_used - 1)
    tile_cls = jnp.sum((tile_end[None, :] <= tile_idx[:, None]).astype(jnp.int32), axis=1)
    grp, pair = tile_cls // PAIRS_PER_GROUP, tile_cls % PAIRS_PER_GROUP
    ea = EXP_PER_GROUP * grp + jnp.asarray(_PAIR_LO)[pair]
    eb = EXP_PER_GROUP * grp + jnp.asarray(_PAIR_HI)[pair]
    rank = (route[1] * LANES + route[2]).astype(jnp.int32)
    pos = cls_base[route[0].astype(jnp.int32)] + rank

    pad_start = jnp.minimum((cls_base + cnt) // 8 * 8, n_tiles * tm - (tm + 8))
    pad_start = jnp.concatenate([pad_start, (n_used * tm).reshape(1)])
    xs = _row_scatter(pos, pad_start, xe, n_tiles * tm, tm + 8)
    ys = _moe_sorted(ea, eb, n_used.reshape(1), xs, nffn, wg, wu, wd, tm=tm)
    return _row_gather(pos, ys, t)


def _cols(w, off, width):
    return w[:, off:off + width]


def _pad_lanes(v, lane0, width=LANES):
    out = jnp.zeros((1, width), F32)
    return out.at[0, lane0:lane0 + v.shape[0]].set(v.astype(F32))


def _layer(x2, b, s, norm_mix, w_in, fox_f_bias, fox_q_norm, fox_k_norm, gla_w_lr, gla_b_gate,
           gla_out_norm, ssm_conv_w, ssm_conv_b, ssm_dt_bias, ssm_a_log, ssm_d, ssm_out_norm,
           w_branch, b_branch_gate, w_out, norm_ffn, w_router_grp, b_router_grp,
           w_router_exp, b_router_exp, w_exp_gate, w_exp_up, w_exp_down):
    t = b * s
    w_big = jnp.concatenate([
        _cols(w_in, _O_GATE, 3 * D_MODEL), _cols(w_in, _O_FQ, MIX_W), _cols(w_in, _O_FK, MIX_W),
        _cols(w_in, _O_FV, MIX_W), _cols(w_in, _O_GV, MIX_W), _cols(w_in, _O_GR, MIX_W),
        _cols(w_in, _O_SZ, MIX_W), _cols(w_in, _O_SX, MIX_W), _cols(w_in, _O_GQ, 256),
        _cols(w_in, _O_GK, 256), _cols(w_in, _O_SB, LANES), _cols(w_in, _O_SC, LANES),
    ], axis=1).astype(BF16)
    w_small = jnp.concatenate([
        _cols(w_in, _O_FF, FOX_HEADS), _cols(w_in, _O_GLR, GLA_RANK), _cols(w_in, _O_SDT, SSM_HEADS),
        jnp.zeros((D_MODEL, LANES - FOX_HEADS - GLA_RANK - SSM_HEADS), F32),
    ], axis=1)
    ws_hi, ws_lo = _split2(w_small)

    big, small = _inproj(x2, norm_mix.reshape(1, D_MODEL), w_big, ws_hi, ws_lo)
    big3 = big.reshape(b, s, BIG_COLS)
    small3 = small.reshape(b, s, LANES)

    qt, kf, vt = _fox_prep(big3, small3, _pad_lanes(fox_f_bias, FF_LANE),
                           jnp.tile(fox_q_norm, FOX_HEADS).reshape(1, MIX_W),
                           jnp.tile(fox_k_norm, FOX_HEADS).reshape(1, MIX_W), ts=min(FOX_TILE, s))
    logit_bound = ((FOX_HD ** 0.5) * 1.01 * jnp.max(jnp.abs(fox_q_norm))
                   * jnp.max(jnp.abs(fox_k_norm)))
    o_a = _fox_attn(qt, kf, vt, logit_bound)

    wl = jnp.zeros((LANES, GLA_HEADS * GLA_DK), F32).at[GLR_LANE:GLR_LANE + GLA_RANK].set(gla_w_lr)
    wl_hi, wl_lo = _split2(wl)
    o_b = _gla(big3, small3, wl_hi, wl_lo, gla_b_gate.reshape(1, -1),
               gla_out_norm.reshape(1, GLA_DV))

    o_c = _ssd(big3, small3, ssm_conv_w, ssm_conv_b.reshape(1, -1),
               _pad_lanes(ssm_dt_bias, SDT_LANE),
               jnp.repeat(ssm_a_log, SSM_HD).reshape(1, MIX_W),
               jnp.broadcast_to(ssm_dt_bias[:, None], (SSM_HEADS, LANES)),
               jnp.broadcast_to(ssm_a_log[:, None], (SSM_HEADS, LANES)),
               jnp.repeat(ssm_d, SSM_HD).reshape(1, MIX_W),
               ssm_out_norm.reshape(1, MIX_W))

    w_r = jnp.concatenate([w_router_grp, jnp.zeros((D_MODEL, RE_LANE - N_EGROUPS), F32), w_router_exp,
                           jnp.zeros((D_MODEL, LANES - RE_LANE - N_EXPERTS), F32)], axis=1)
    wr_hi, wr_lo = _split2(w_r)
    b_r = jnp.concatenate([b_router_grp, jnp.zeros((RE_LANE - N_EGROUPS,), F32), b_router_exp,
                           jnp.zeros((LANES - RE_LANE - N_EXPERTS,), F32)]).reshape(1, LANES)
    nffn = norm_ffn.reshape(1, D_MODEL)
    xe, route, counts = _merge(o_a.reshape(t, MIX_W), o_b.reshape(t, MIX_W), o_c.reshape(t, MIX_W),
                               big, x2, w_branch.astype(BF16), b_branch_gate, w_out.astype(BF16),
                               nffn, wr_hi, wr_lo, b_r)

    return _moe(xe, route, counts, nffn, w_exp_gate.astype(BF16), w_exp_up.astype(BF16),
                w_exp_down.astype(BF16))


def kernel(x, norm_mix, w_in, fox_f_bias, fox_q_norm, fox_k_norm, gla_w_lr, gla_b_gate, gla_out_norm, ssm_conv_w, ssm_conv_b, ssm_dt_bias, ssm_a_log, ssm_d, ssm_out_norm, w_branch, b_branch_gate, w_out, norm_ffn, w_router_grp, b_router_grp, w_router_exp, b_router_exp, w_exp_gate, w_exp_up, w_exp_down):
    b, s, d = x.shape
    x2 = x.reshape(b * s, d)
    per_layer = (norm_mix, w_in, fox_f_bias, fox_q_norm, fox_k_norm, gla_w_lr, gla_b_gate,
                 gla_out_norm, ssm_conv_w, ssm_conv_b, ssm_dt_bias, ssm_a_log, ssm_d,
                 ssm_out_norm, w_branch, b_branch_gate, w_out, norm_ffn, w_router_grp,
                 b_router_grp, w_router_exp, b_router_exp, w_exp_gate, w_exp_up, w_exp_down)
    for l in range(norm_mix.shape[0]):
        x2 = _layer(x2, b, s, *[p[l] for p in per_layer])
    return x2.reshape(b, s, d)
```

```python
import functools

import numpy as np
import jax
import jax.numpy as jnp
from jax import lax
from jax.experimental import pallas as pl
from jax.experimental.pallas import tpu as pltpu

F32 = jnp.float32
BF16 = jnp.bfloat16

D_MODEL = 1024
MIX_W = 512
EPS = 1e-6
FOX_HEADS = 8
FOX_HD = 64
FOX_PAIRS = FOX_HEADS // 2
GLA_HEADS = 4
GLA_DK = 64
GLA_DV = 128
GLA_RANK = 16
GLA_GATE_NORM = 16.0
GLA_CHUNK = 64
SSM_HEADS = 8
SSM_HD = 64
SSM_GROUPS = 2
SSM_STATE = 64
SSM_CONV = 4
N_EGROUPS = 4
EXP_PER_GROUP = 4
N_EXPERTS = 16
PAIRS_PER_GROUP = 6
D_EXPERT = 512

LANES = 128
NEG_BIG = -1e30
LOG2E = 1.4426950408889634
VMEM_LIMIT = 56 * 1024 * 1024

GATE_OFF, FQ_OFF, FK_OFF, FV_OFF = 0, 3072, 3584, 4096
GV_OFF, GR_OFF, SZ_OFF, SX_OFF = 4608, 5120, 5632, 6144
GQ_OFF, GK_OFF, SB_OFF, SC_OFF = 6656, 6912, 7168, 7296
BIG_COLS = 7424
FF_LANE, GLR_LANE, SDT_LANE = 0, 8, 24
_O_FQ, _O_FK, _O_FV, _O_FF = 0, 512, 1024, 1536
_O_GQ, _O_GK, _O_GV, _O_GR, _O_GLR = 1544, 1800, 2056, 2568, 3080
_O_SZ, _O_SX, _O_SB, _O_SC, _O_SDT, _O_GATE = 3096, 3608, 4120, 4248, 4376, 4384
RG_LANE, RE_LANE = 0, 8
ROUTE_ROWS = 32
XE_COLS = D_MODEL + LANES


def _split2(x):
    hi = x.astype(BF16)
    lo = (x - hi.astype(F32)).astype(BF16)
    return hi, lo


def _split3(x):
    x1 = x.astype(BF16)
    r = x - x1.astype(F32)
    x2 = r.astype(BF16)
    x3 = (r - x2.astype(F32)).astype(BF16)
    return x1, x2, x3


def _dot(a, b):
    return jnp.dot(a, b, preferred_element_type=F32)


def _dot_nt(a, b):
    return lax.dot_general(a, b, (((1,), (1,)), ((), ())), preferred_element_type=F32)


def _dot_tn(a, b):
    return lax.dot_general(a, b, (((0,), (0,)), ((), ())), preferred_element_type=F32)


def _dot3_left(m_bf16, x_f32):
    x1, x2, x3 = _split3(x_f32)
    return _dot(m_bf16, x1) + _dot(m_bf16, x2) + _dot(m_bf16, x3)


def _dot3_right(x_f32, m_bf16):
    x1, x2, x3 = _split3(x_f32)
    return _dot(x1, m_bf16) + _dot(x2, m_bf16) + _dot(x3, m_bf16)


def _dot_f32w(x_f32, w_hi, w_lo):
    x_hi, x_lo = _split2(x_f32)
    return _dot(x_hi, w_hi) + _dot(x_lo, w_hi) + _dot(x_hi, w_lo)


def _shr(x, pow2):
    return jnp.right_shift(x, pow2.bit_length() - 1)


def _log_sigmoid(x):
    return jnp.minimum(x, 0.0) - jnp.log(1.0 + jnp.exp(-jnp.abs(x)))


def _softplus(x):
    return jnp.maximum(x, 0.0) + jnp.log(1.0 + jnp.exp(-jnp.abs(x)))


def _sigmoid(x):
    return 0.5 * jnp.tanh(0.5 * x) + 0.5


def _silu(x):
    return x * _sigmoid(x)


def _params(sem):
    return pltpu.CompilerParams(dimension_semantics=sem, vmem_limit_bytes=VMEM_LIMIT)


def _inproj_kernel(x_ref, g_ref, w_ref, wsh_ref, wsl_ref, big_ref, small_ref, *, tn):
    x = x_ref[...]
    ms = jnp.mean(x * x, axis=-1, keepdims=True)
    h = x * lax.rsqrt(ms + EPS) * g_ref[...]
    hb = h.astype(BF16)
    for c in range(BIG_COLS // tn):
        cols = slice(c * tn, (c + 1) * tn)
        big_ref[:, cols] = _dot(hb, w_ref[:, cols]).astype(BF16)
    h_lo = (h - hb.astype(F32)).astype(BF16)
    wsh = wsh_ref[...]
    small_ref[...] = _dot(hb, wsh) + _dot(h_lo, wsh) + _dot(hb, wsl_ref[...])


def _inproj(x2, gain, w_big, ws_hi, ws_lo, *, tm=512, tn=256):
    t = x2.shape[0]
    return pl.pallas_call(
        functools.partial(_inproj_kernel, tn=tn),
        grid=(t // tm,),
        in_specs=[
            pl.BlockSpec((tm, D_MODEL), lambda i: (i, 0)),
            pl.BlockSpec((1, D_MODEL), lambda i: (0, 0)),
            pl.BlockSpec((D_MODEL, BIG_COLS), lambda i: (0, 0), pipeline_mode=pl.Buffered(1)),
            pl.BlockSpec((D_MODEL, LANES), lambda i: (0, 0)),
            pl.BlockSpec((D_MODEL, LANES), lambda i: (0, 0)),
        ],
        out_specs=[
            pl.BlockSpec((tm, BIG_COLS), lambda i: (i, 0)),
            pl.BlockSpec((tm, LANES), lambda i: (i, 0)),
        ],
        out_shape=[
            jax.ShapeDtypeStruct((t, BIG_COLS), BF16),
            jax.ShapeDtypeStruct((t, LANES), F32),
        ],
        compiler_params=_params(("arbitrary",)),
        name="inproj",
    )(x2, gain, w_big, ws_hi, ws_lo)


def _fox_consts(ts):
    ltri = np.tril(np.ones((ts, ts), np.float32))
    hsum = np.kron(np.eye(HSUM_W // FOX_HD, dtype=np.float32), np.ones((FOX_HD, FOX_HD), np.float32))
    sq = np.zeros((3, LANES, MIX_W), np.float32)
    sk = np.zeros((3, LANES, MIX_W), np.float32)
    oneq = np.zeros((1, MIX_W), np.float32)
    onek = np.zeros((1, MIX_W), np.float32)
    for h in range(FOX_HEADS):
        base = LANES * (h // 2) + 6 * (h % 2)
        for j in range(3):
            sq[j, FF_LANE + h, base + j] = 1.0
            sk[j, FF_LANE + h, base + 3 + j] = -1.0
            oneq[0, base + 3 + j] = 1.0
            onek[0, base + j] = 1.0
    saug = np.concatenate([sq.reshape(3 * LANES, MIX_W), sk.reshape(3 * LANES, MIX_W)], axis=1)
    return (jnp.asarray(ltri, BF16), jnp.asarray(hsum, BF16), jnp.asarray(saug, BF16),
            jnp.asarray(oneq), jnp.asarray(onek))


def _fox_prep_kernel(fq_ref, fk_ref, fv_ref, small_ref, fbias_ref, qg_ref, kg_ref, ltri_ref,
                     hsum_ref, saug_ref, oneq_ref, onek_ref, qt_ref, kf_ref, vt_ref, carry_ref,
                     *, ts):
    @pl.when(pl.program_id(1) == 0)
    def _():
        carry_ref[...] = jnp.zeros_like(carry_ref)

    lane = lax.broadcasted_iota(jnp.int32, (ts, LANES), 1)
    f = small_ref[0] + fbias_ref[...]
    ls = jnp.where(lane < FOX_HEADS, _log_sigmoid(f) * LOG2E, 0.0)
    parts = _dot(ltri_ref[...], jnp.concatenate(_split3(ls), axis=1))
    c = parts[:, 0:LANES] + parts[:, LANES:2 * LANES] + parts[:, 2 * LANES:] + carry_ref[0:1, :]
    carry_ref[...] = jnp.broadcast_to(c[ts - 1:ts, :], carry_ref.shape)
    aug = _dot(jnp.concatenate(_split3(c), axis=1), saug_ref[...])
    qaug = aug[:, 0:MIX_W] + oneq_ref[...]
    kaug = aug[:, MIX_W:] + onek_ref[...]

    hsum = hsum_ref[...]

    def head_norm(xb, gain):
        x = xb.astype(F32)
        s_hi, s_lo = _split2(x * x)
        halves = []
        for half in range(MIX_W // HSUM_W):
            cols = slice(HSUM_W * half, HSUM_W * (half + 1))
            halves.append(_dot(s_hi[:, cols], hsum) + _dot(s_lo[:, cols], hsum))
        ss = jnp.concatenate(halves, axis=1)
        return x * lax.rsqrt(ss * (1.0 / FOX_HD) + EPS) * gain

    qn = head_norm(fq_ref[0], qg_ref[...]) * (FOX_HD ** -0.5 * LOG2E)
    kn = head_norm(fk_ref[0], kg_ref[...])
    v = fv_ref[0].astype(F32)
    for p in range(FOX_PAIRS):
        src = slice(LANES * p, LANES * (p + 1))
        dst_x = slice(2 * LANES * p, 2 * LANES * p + LANES)
        dst_a = slice(2 * LANES * p + LANES, 2 * LANES * (p + 1))
        kf_ref[0, :, dst_x] = kn[:, src].astype(BF16)
        kf_ref[0, :, dst_a] = kaug[:, src].astype(BF16)
        qt_ref[0, p, 0, 0:LANES, :] = qn[:, src].T.astype(BF16)
        qt_ref[0, p, 0, LANES:2 * LANES, :] = qaug[:, src].T.astype(BF16)
        vt_ref[0, p, 0] = v[:, src].T.astype(BF16)


def _fox_prep(big3, small3, fbias, qgain, kgain, *, ts):
    b, s, _ = big3.shape
    consts = _fox_consts(ts)
    const_specs = [
        pl.BlockSpec((ts, ts), lambda bi, i: (0, 0)),
        pl.BlockSpec((HSUM_W, HSUM_W), lambda bi, i: (0, 0)),
        pl.BlockSpec((3 * LANES, 2 * MIX_W), lambda bi, i: (0, 0)),
        pl.BlockSpec((1, MIX_W), lambda bi, i: (0, 0)),
        pl.BlockSpec((1, MIX_W), lambda bi, i: (0, 0)),
    ]
    return pl.pallas_call(
        functools.partial(_fox_prep_kernel, ts=ts),
        grid=(b, s // ts),
        in_specs=[
            pl.BlockSpec((1, ts, MIX_W), lambda bi, i: (bi, i, FQ_OFF // MIX_W)),
            pl.BlockSpec((1, ts, MIX_W), lambda bi, i: (bi, i, FK_OFF // MIX_W)),
            pl.BlockSpec((1, ts, MIX_W), lambda bi, i: (bi, i, FV_OFF // MIX_W)),
            pl.BlockSpec((1, ts, LANES), lambda bi, i: (bi, i, 0)),
            pl.BlockSpec((1, LANES), lambda bi, i: (0, 0)),
            pl.BlockSpec((1, MIX_W), lambda bi, i: (0, 0)),
            pl.BlockSpec((1, MIX_W), lambda bi, i: (0, 0)),
        ] + const_specs,
        out_specs=[
            pl.BlockSpec((1, FOX_PAIRS, 1, 2 * LANES, ts), lambda bi, i: (bi, 0, i, 0, 0)),
            pl.BlockSpec((1, ts, 2 * MIX_W), lambda bi, i: (bi, i, 0)),
            pl.BlockSpec((1, FOX_PAIRS, 1, LANES, ts), lambda bi, i: (bi, 0, i, 0, 0)),
        ],
        out_shape=[
            jax.ShapeDtypeStruct((b, FOX_PAIRS, s // ts, 2 * LANES, ts), BF16),
            jax.ShapeDtypeStruct((b, s, 2 * MIX_W), BF16),
            jax.ShapeDtypeStruct((b, FOX_PAIRS, s // ts, LANES, ts), BF16),
        ],
        scratch_shapes=[pltpu.VMEM((8, LANES), F32)],
        compiler_params=_params(("arbitrary", "arbitrary")),
        name="fox_prep",
    )(big3, big3, big3, small3, fbias, qgain, kgain, *consts)


FOX_NOSHIFT_BOUND = 40.0
FOX_TILE = 512
HSUM_W = 256
FOX_UNROLL = 8


def _fox_attn_kernel(qt_ref, k_ref, vt_ref, o_ref, acc_ref, l_ref, m_ref, *, tq, online):
    i = pl.program_id(2)
    qt = qt_ref[0, 0, 0]
    qrow = lax.broadcasted_iota(jnp.int32, (2 * LANES, 1), 0)
    in_a = (qrow < FOX_HD) | ((qrow >= LANES) & (qrow < LANES + 6))
    in_b = ((qrow >= FOX_HD) & (qrow < LANES)) | ((qrow >= LANES + 6) & (qrow < LANES + 12))
    zero = jnp.zeros_like(qt)
    qt_heads = (jnp.where(in_a, qt, zero), jnp.where(in_b, qt, zero))

    acc_ref[...] = jnp.zeros_like(acc_ref)
    l_ref[...] = jnp.zeros_like(l_ref)
    if online:
        m_ref[...] = jnp.full_like(m_ref, NEG_BIG)

    def scores(j, a, diag):
        k = k_ref[0, pl.ds(pl.multiple_of(j * tq, tq), tq), :]
        st = _dot(k, qt_heads[a])
        if diag:
            krow = lax.broadcasted_iota(jnp.int32, (tq, tq), 0)
            qcol = lax.broadcasted_iota(jnp.int32, (tq, tq), 1)
            st = jnp.where(krow <= qcol, st, NEG_BIG)
        return st

    def accumulate(j, a, st):
        vt_a = vt_ref[0, 0, j, FOX_HD * a:FOX_HD * (a + 1), :]
        if online:
            m_prev = m_ref[a]
            m_new = jnp.maximum(m_prev, jnp.max(st, axis=0, keepdims=True))
            alpha = jnp.exp2(m_prev - m_new)
            m_ref[a] = m_new
            pt = jnp.exp2(st - m_new)
            l_ref[a] = alpha * l_ref[a] + jnp.sum(pt.reshape(tq // 8, 8, tq), axis=0)
            acc_ref[a] = alpha * acc_ref[a] + _dot(vt_a, pt.astype(BF16))
        else:
            pt = jnp.exp2(st)
            l_ref[a] += jnp.sum(pt.reshape(tq // 8, 8, tq), axis=0)
            acc_ref[a] += _dot(vt_a, pt.astype(BF16))

    def run(units):
        st = scores(*units[0])
        for u, unit in enumerate(units):
            st_next = scores(*units[u + 1]) if u + 1 < len(units) else None
            accumulate(unit[0], unit[1], st)
            st = st_next

    def units(first_block, n_regular, with_diag):
        blocks = [(first_block + d, False) for d in range(n_regular)]
        if with_diag:
            blocks.append((first_block + n_regular, True))
        return [(j, a, diag) for j, diag in blocks for a in range(2)]

    def body(jj, carry):
        run(units(FOX_UNROLL * jj, FOX_UNROLL, False))
        return carry

    lax.fori_loop(0, i // FOX_UNROLL, body, 0)
    for rem in range(FOX_UNROLL):
        @pl.when(i % FOX_UNROLL == rem)
        def _():
            run(units(i - rem, rem, True))

    halves = [acc_ref[a] * (1.0 / jnp.sum(l_ref[a], axis=0, keepdims=True)) for a in range(2)]
    o_ref[0] = jnp.concatenate(halves, axis=0).T.astype(BF16)


def _fox_attn(qt, kf, vt, logit_bound):
    b, _, nq, _, tq = qt.shape
    s = nq * tq

    def call(online, name):
        return pl.pallas_call(
            functools.partial(_fox_attn_kernel, tq=tq, online=online),
            grid=(b, FOX_PAIRS, nq),
            in_specs=[
                pl.BlockSpec((1, 1, 1, 2 * LANES, tq), lambda bi, p, i: (bi, p, i, 0, 0)),
                pl.BlockSpec((1, s, 2 * LANES), lambda bi, p, i: (bi, 0, p)),
                pl.BlockSpec((1, 1, nq, LANES, tq), lambda bi, p, i: (bi, p, 0, 0, 0)),
            ],
            out_specs=pl.BlockSpec((1, tq, LANES), lambda bi, p, i: (bi, i, p)),
            out_shape=jax.ShapeDtypeStruct((b, s, MIX_W), BF16),
            scratch_shapes=[
                pltpu.VMEM((2, FOX_HD, tq), F32),
                pltpu.VMEM((2, 8, tq), F32),
                pltpu.VMEM((2, 1, tq), F32),
            ],
            compiler_params=_params(("arbitrary", "arbitrary", "arbitrary")),
            name=name,
        )(qt, kf, vt)

    return lax.cond(logit_bound < FOX_NOSHIFT_BOUND,
                    lambda: call(False, "fox_attn"), lambda: call(True, "fox_attn_online"))


def _gla_consts(r):
    idx = np.arange(r)
    same = (idx[:, None] // GLA_CHUNK) == (idx[None, :] // GLA_CHUNK)
    lblk = (same & (idx[None, :] <= idx[:, None])).astype(np.float32)
    ablk = same.astype(np.float32)
    return jnp.asarray(lblk, BF16), jnp.asarray(ablk, BF16)


def _gla_kernel(q_ref, k_ref, v_ref, r_ref, small_ref, wlh_ref, wll_ref, bg_ref, lblk_ref,
                ablk_ref, gain_ref, o_ref, st_ref, oacc_ref, *, r):
    @pl.when(pl.program_id(1) == 0)
    def _():
        st_ref[...] = jnp.zeros_like(st_ref)

    kw = GLA_HEADS * GLA_DK
    gate = _dot_f32w(small_ref[0], wlh_ref[...], wll_ref[...]) + bg_ref[...]
    log_a = _log_sigmoid(gate) * (1.0 / GLA_GATE_NORM)
    a1, a2, a3 = _split3(log_a)
    lblk = lblk_ref[...]
    ablk = ablk_ref[...]
    bcum = _dot(lblk, a1) + _dot(lblk, a2) + _dot(lblk, a3)
    btot = _dot(ablk, a1) + _dot(ablk, a2) + _dot(ablk, a3)
    q = q_ref[0].astype(F32) * (GLA_DK ** -0.5)
    k = k_ref[0].astype(F32)
    q_dec = (q * jnp.exp(bcum)).astype(BF16)
    k_dec = (k * jnp.exp(-bcum)).astype(BF16)
    k_end_t = (k * jnp.exp(btot - bcum)).T.astype(BF16)
    d_tot_t = jnp.exp(btot).T
    v = v_ref[0]

    row = lax.broadcasted_iota(jnp.int32, (r, r), 0)
    col = lax.broadcasted_iota(jnp.int32, (r, r), 1)
    keep = (_shr(row, GLA_CHUNK) == _shr(col, GLA_CHUNK)) & (col <= row)
    klane = lax.broadcasted_iota(jnp.int32, (1, kw), 1)
    qzero = jnp.zeros_like(q_dec)
    q_heads = []
    for h in range(GLA_HEADS):
        in_h = (klane >= GLA_DK * h) & (klane < GLA_DK * (h + 1))
        q_heads.append(jnp.where(in_h, q_dec, qzero))
        att = _dot_nt(q_heads[h], k_dec)
        att = jnp.where(keep, att, 0.0).astype(BF16)
        vcols = slice(GLA_DV * h, GLA_DV * (h + 1))
        oacc_ref[:, vcols] = _dot(att, v[:, vcols])

    tlane = lax.broadcasted_iota(jnp.int32, (1, r), 1)
    kzero = jnp.zeros_like(k_end_t)
    for c in range(r // GLA_CHUNK):
        rows = slice(GLA_CHUNK * c, GLA_CHUNK * (c + 1))
        st = st_ref[...]
        st_b = st.astype(BF16)
        k_chunk = jnp.where(_shr(tlane, GLA_CHUNK) == c, k_end_t, kzero)
        updates = []
        for h in range(GLA_HEADS):
            vcols = slice(GLA_DV * h, GLA_DV * (h + 1))
            oacc_ref[rows, vcols] += _dot(q_heads[h][rows], st_b)
            updates.append(_dot(k_chunk[GLA_DK * h:GLA_DK * (h + 1), :], v[:, vcols]))
        decay = d_tot_t[:, GLA_CHUNK * c:GLA_CHUNK * c + 1]
        st_ref[...] = decay * st + jnp.concatenate(updates, axis=0)

    gain = gain_ref[...]
    gr = r_ref[0].astype(F32)
    for h in range(GLA_HEADS):
        vcols = slice(GLA_DV * h, GLA_DV * (h + 1))
        o = oacc_ref[:, vcols]
        ms = jnp.mean(o * o, axis=-1, keepdims=True)
        o_ref[0, :, vcols] = (o * lax.rsqrt(ms + EPS) * gain * _silu(gr[:, vcols])).astype(BF16)


def _gla(big3, small3, wl_hi, wl_lo, bgate, gain, *, r=256):
    b, s, _ = big3.shape
    r = min(r, s)
    kw = GLA_HEADS * GLA_DK
    lblk, ablk = _gla_consts(r)
    return pl.pallas_call(
        functools.partial(_gla_kernel, r=r),
        grid=(b, s // r),
        in_specs=[
            pl.BlockSpec((1, r, kw), lambda bi, i: (bi, i, GQ_OFF // kw)),
            pl.BlockSpec((1, r, kw), lambda bi, i: (bi, i, GK_OFF // kw)),
            pl.BlockSpec((1, r, MIX_W), lambda bi, i: (bi, i, GV_OFF // MIX_W)),
            pl.BlockSpec((1, r, MIX_W), lambda bi, i: (bi, i, GR_OFF // MIX_W)),
            pl.BlockSpec((1, r, LANES), lambda bi, i: (bi, i, 0)),
            pl.BlockSpec((LANES, kw), lambda bi, i: (0, 0)),
            pl.BlockSpec((LANES, kw), lambda bi, i: (0, 0)),
            pl.BlockSpec((1, kw), lambda bi, i: (0, 0)),
            pl.BlockSpec((r, r), lambda bi, i: (0, 0)),
            pl.BlockSpec((r, r), lambda bi, i: (0, 0)),
            pl.BlockSpec((1, GLA_DV), lambda bi, i: (0, 0)),
        ],
        out_specs=pl.BlockSpec((1, r, MIX_W), lambda bi, i: (bi, i, 0)),
        out_shape=jax.ShapeDtypeStruct((b, s, MIX_W), BF16),
        scratch_shapes=[pltpu.VMEM((kw, GLA_DV), F32), pltpu.VMEM((r, MIX_W), F32)],
        compiler_params=_params(("arbitrary", "arbitrary")),
        name="gla",
    )(big3, big3, big3, big3, small3, wl_hi, wl_lo, bgate, lblk, ablk, gain)


SSD_HALO = 16
SSD_CONV_W = MIX_W + 2 * SSM_GROUPS * SSM_STATE
SSD_STATE_W = 256


def _ssd_consts(r):
    expand = np.zeros((LANES, MIX_W), np.float32)
    for h in range(SSM_HEADS):
        expand[SDT_LANE + h, SSM_HD * h:SSM_HD * (h + 1)] = 1.0
    rep = np.zeros((LANES, MIX_W), np.float32)
    for h in range(SSM_HEADS):
        g = h // (SSM_HEADS // SSM_GROUPS)
        for n in range(SSM_STATE):
            rep[SSM_STATE * g + n, SSM_HD * h + n] = 1.0
    ltri = np.tril(np.ones((r, r), np.float32))
    return (jnp.asarray(expand, BF16), jnp.asarray(rep, BF16), jnp.asarray(ltri, BF16),
            jnp.asarray(ltri.T, BF16))


def _ssd_kernel(z_ref, x_ref, xp_ref, b_ref, bp_ref, c_ref, cp_ref, small_ref, cw_ref, cb_ref,
                dtb_ref, alog_ref, dtbc_ref, alogc_ref, dskip_ref, onorm_ref, expand_ref,
                rep_ref, ltri_ref, utri_ref, o_ref, sw_ref, ext_ref, y_ref, *, r):
    first = pl.program_id(1) == 0

    @pl.when(first)
    def _():
        sw_ref[...] = jnp.zeros_like(sw_ref)

    keep_prev = jnp.where(first, 0.0, 1.0)
    ext_ref[0:SSD_HALO, 0:MIX_W] = xp_ref[0].astype(F32) * keep_prev
    ext_ref[0:SSD_HALO, MIX_W:MIX_W + LANES] = bp_ref[0].astype(F32) * keep_prev
    ext_ref[0:SSD_HALO, MIX_W + LANES:SSD_CONV_W] = cp_ref[0].astype(F32) * keep_prev
    ext_ref[SSD_HALO:, 0:MIX_W] = x_ref[0].astype(F32)
    ext_ref[SSD_HALO:, MIX_W:MIX_W + LANES] = b_ref[0].astype(F32)
    ext_ref[SSD_HALO:, MIX_W + LANES:SSD_CONV_W] = c_ref[0].astype(F32)
    conv = cb_ref[...] + cw_ref[SSM_CONV - 1:SSM_CONV, :] * ext_ref[SSD_HALO:, :]
    for back in range(1, SSM_CONV):
        tap = SSM_CONV - 1 - back
        conv = conv + cw_ref[tap:tap + 1, :] * ext_ref[pl.ds(SSD_HALO - back, r), :]
    xbc = _silu(conv)
    xs = xbc[:, 0:MIX_W]
    bm = xbc[:, MIX_W:MIX_W + LANES].astype(BF16)
    cm = xbc[:, MIX_W + LANES:SSD_CONV_W].astype(BF16)

    sm = small_ref[0]
    dt = _dot3_right(_softplus(sm + dtb_ref[...]), expand_ref[...])
    a_neg = -jnp.exp(alog_ref[...])
    acs = _dot3_left(ltri_ref[...], dt * a_neg)
    acs_last = acs[r - 1:r, :]
    sm_t = sm.T
    dt_t = _softplus(sm_t[SDT_LANE:SDT_LANE + SSM_HEADS, :] + dtbc_ref[:, 0:1])
    acs_t = _dot3_right(dt_t * (-jnp.exp(alogc_ref[:, 0:1])), utri_ref[...])

    xdt = (xs * dt).astype(BF16)
    row = lax.broadcasted_iota(jnp.int32, (r, r), 0)
    col = lax.broadcasted_iota(jnp.int32, (r, r), 1)
    causal = col <= row
    glane = lax.broadcasted_iota(jnp.int32, (1, LANES), 1)
    first_half = glane < SSM_STATE
    czero = jnp.zeros_like(cm)
    hpg = SSM_HEADS // SSM_GROUPS
    for g in range(SSM_GROUPS):
        cg = jnp.where(first_half if g == 0 else ~first_half, cm, czero)
        cb_g = _dot_nt(cg, bm)
        for pair in range(hpg // 2):
            p = g * (hpg // 2) + pair
            xp = xdt[:, LANES * p:LANES * (p + 1)]
            xzero = jnp.zeros_like(xp)
            acc = None
            for a in range(2):
                h = 2 * p + a
                dmat = acs[:, SSM_HD * h:SSM_HD * h + 1] - acs_t[h:h + 1, :]
                sc = (cb_g * jnp.exp(jnp.where(causal, dmat, NEG_BIG))).astype(BF16)
                xh = jnp.where(first_half if a == 0 else ~first_half, xp, xzero)
                contrib = _dot(sc, xh)
                acc = contrib if acc is None else acc + contrib
            y_ref[:, LANES * p:LANES * (p + 1)] = acc

    rep = rep_ref[...]
    cw = (_dot(cm, rep) * jnp.exp(acs)).astype(BF16)
    bw = (_dot(bm, rep) * jnp.exp(acs_last - acs)).astype(BF16)
    srow = lax.broadcasted_iota(jnp.int32, (SSD_STATE_W, SSD_STATE_W), 0)
    scol = lax.broadcasted_iota(jnp.int32, (SSD_STATE_W, SSD_STATE_W), 1)
    same_head = _shr(srow, SSM_STATE) == _shr(scol, SSM_HD)
    y_inter = []
    for half in range(MIX_W // SSD_STATE_W):
        cols = slice(SSD_STATE_W * half, SSD_STATE_W * (half + 1))
        sw = sw_ref[half]
        y_inter.append(_dot(cw[:, cols], sw.astype(BF16)))
        upd = _dot_tn(bw[:, cols], xdt[:, cols])
        sw_ref[half] = sw * jnp.exp(acs_last[:, cols]) + jnp.where(same_head, upd, 0.0)
    y = y_ref[...] + jnp.concatenate(y_inter, axis=1) + dskip_ref[...] * xs

    y = y * _silu(z_ref[0].astype(F32))
    gw = MIX_W // SSM_GROUPS
    for g in range(SSM_GROUPS):
        cols = slice(gw * g, gw * (g + 1))
        yg = y[:, cols]
        ms = jnp.mean(yg * yg, axis=-1, keepdims=True)
        o_ref[0, :, cols] = (yg * lax.rsqrt(ms + EPS) * onorm_ref[:, cols]).astype(BF16)


def _ssd(big3, small3, conv_w, conv_b, dtb_w, alog_w, dtb_c, alog_c, dskip_w, onorm, *, r=256):
    b, s, _ = big3.shape
    r = min(r, s)
    consts = _ssd_consts(r)
    hb = r // SSD_HALO

    def cur(width, off):
        return pl.BlockSpec((1, r, width), lambda bi, i: (bi, i, off // width))

    def prev(width, off):
        return pl.BlockSpec((1, SSD_HALO, width),
                            lambda bi, i: (bi, jnp.maximum(i * hb - 1, 0), off // width))

    def whole(shape):
        return pl.BlockSpec(shape, lambda bi, i: (0,) * len(shape))

    return pl.pallas_call(
        functools.partial(_ssd_kernel, r=r),
        grid=(b, s // r),
        in_specs=[
            cur(MIX_W, SZ_OFF),
            cur(MIX_W, SX_OFF), prev(MIX_W, SX_OFF),
            cur(LANES, SB_OFF), prev(LANES, SB_OFF),
            cur(LANES, SC_OFF), prev(LANES, SC_OFF),
            pl.BlockSpec((1, r, LANES), lambda bi, i: (bi, i, 0)),
            whole((SSM_CONV, SSD_CONV_W)), whole((1, SSD_CONV_W)),
            whole((1, LANES)), whole((1, MIX_W)),
            whole((SSM_HEADS, LANES)), whole((SSM_HEADS, LANES)),
            whole((1, MIX_W)), whole((1, MIX_W)),
            whole((LANES, MIX_W)), whole((LANES, MIX_W)), whole((r, r)), whole((r, r)),
        ],
        out_specs=pl.BlockSpec((1, r, MIX_W), lambda bi, i: (bi, i, 0)),
        out_shape=jax.ShapeDtypeStruct((b, s, MIX_W), BF16),
        scratch_shapes=[
            pltpu.VMEM((MIX_W // SSD_STATE_W, SSD_STATE_W, SSD_STATE_W), F32),
            pltpu.VMEM((r + SSD_HALO, SSD_CONV_W), F32),
            pltpu.VMEM((r, MIX_W), F32),
        ],
        compiler_params=_params(("arbitrary", "arbitrary")),
        name="ssd",
    )(big3, big3, big3, big3, big3, big3, big3, small3, conv_w, conv_b, dtb_w, alog_w,
      dtb_c, alog_c, dskip_w, onorm, *consts)


def _merge_kernel(oa_ref, ob_ref, oc_ref, gate_ref, x_ref, wb_ref, bgate_ref, wo_ref, nffn_ref,
                  wrh_ref, wrl_ref, br_ref, ustrict_ref, xe_ref, route_ref, counts_ref, *, tm):
    @pl.when(pl.program_id(0) == 0)
    def _():
        counts_ref[...] = jnp.zeros_like(counts_ref)

    mixed = None
    for ridx, o_ref in enumerate((oa_ref, ob_ref, oc_ref)):
        cols = slice(D_MODEL * ridx, D_MODEL * (ridx + 1))
        gate = _sigmoid(gate_ref[:, cols].astype(F32) + bgate_ref[ridx:ridx + 1, :])
        term = gate * _dot(o_ref[...], wb_ref[ridx])
        mixed = term if mixed is None else mixed + term
    xn = x_ref[...] + _dot(mixed.astype(BF16), wo_ref[...])
    xe_ref[:, 0:D_MODEL] = xn
    ms = jnp.mean(xn * xn, axis=-1, keepdims=True)
    h = xn * lax.rsqrt(ms + EPS) * nffn_ref[...]

    lt = (_dot_f32w(h, wrh_ref[...], wrl_ref[...]) + br_ref[...]).T
    grow = lax.broadcasted_iota(jnp.int32, (8, tm), 0)
    grow_f = grow.astype(F32)
    far_row = float(LANES)
    gl = jnp.where(grow < N_EGROUPS, lt[RG_LANE:RG_LANE + 8, :], NEG_BIG)
    gmax = jnp.max(gl, axis=0, keepdims=True)
    g_w = 1.0 / jnp.sum(jnp.exp(gl - gmax), axis=0, keepdims=True)
    g_sel = jnp.min(jnp.where(gl == gmax, grow_f, far_row), axis=0, keepdims=True)
    e16 = lt[RE_LANE:RE_LANE + N_EXPERTS, :]
    erow = lax.broadcasted_iota(jnp.int32, (N_EXPERTS, tm), 0)
    erow_f = erow.astype(F32)
    in_grp = _shr(erow, EXP_PER_GROUP).astype(F32) == g_sel
    el = jnp.where(in_grp, e16, NEG_BIG)
    e1 = jnp.max(el, axis=0, keepdims=True)
    i1 = jnp.min(jnp.where(in_grp & (el == e1), erow_f, far_row), axis=0, keepdims=True)
    rest = in_grp & (erow_f != i1)
    el2 = jnp.where(rest, e16, NEG_BIG)
    e2 = jnp.max(el2, axis=0, keepdims=True)
    i2 = jnp.min(jnp.where(rest & (el2 == e2), erow_f, far_row), axis=0, keepdims=True)
    ratio = jnp.exp(e2 - e1)
    w1 = g_w / (1.0 + ratio)
    w2 = w1 * ratio
    comb_t = jnp.where(erow_f == i1, w1, 0.0) + jnp.where(erow_f == i2, w2, 0.0)
    comb_t = jnp.concatenate([jnp.zeros((RE_LANE, tm), F32), comb_t,
                              jnp.zeros((LANES - RE_LANE - N_EXPERTS, tm), F32)], axis=0)
    xe_ref[:, D_MODEL:] = comb_t.T

    first_row = EXP_PER_GROUP * g_sel
    lo = jnp.minimum(i1, i2) - first_row
    hi = jnp.maximum(i1, i2) - first_row
    cls = PAIRS_PER_GROUP * g_sel + lo * (7.0 - lo) * 0.5 + (hi - lo - 1.0)
    crow_f = lax.broadcasted_iota(jnp.int32, (ROUTE_ROWS, tm), 0).astype(F32)
    is_cls = crow_f == cls
    onehot = jnp.where(is_cls, 1.0, 0.0)
    before = _dot(onehot.astype(BF16), ustrict_ref[...]) + counts_ref[:, 0:1]
    rank = jnp.sum(jnp.where(is_cls, before, 0.0), axis=0, keepdims=True)
    counts_ref[...] = counts_ref[...] + jnp.sum(onehot, axis=1, keepdims=True)
    rank_hi = jnp.floor(rank * (1.0 / LANES))
    rank_lo = rank - rank_hi * LANES
    route_ref[...] = jnp.where(grow == 0, cls, jnp.where(grow == 1, rank_hi,
                                                         jnp.where(grow == 2, rank_lo, 0.0)))


def _merge(oa, ob, oc, big, x2, wb, bgate, wo, nffn, wr_hi, wr_lo, br, *, tm=512):
    t = x2.shape[0]
    tm = min(tm, t)
    ustrict = jnp.asarray(np.triu(np.ones((tm, tm), np.float32), 1), BF16)

    def whole(shape):
        return pl.BlockSpec(shape, lambda i: (0,) * len(shape))

    return pl.pallas_call(
        functools.partial(_merge_kernel, tm=tm),
        grid=(t // tm,),
        in_specs=[
            pl.BlockSpec((tm, MIX_W), lambda i: (i, 0)),
            pl.BlockSpec((tm, MIX_W), lambda i: (i, 0)),
            pl.BlockSpec((tm, MIX_W), lambda i: (i, 0)),
            pl.BlockSpec((tm, 3 * D_MODEL), lambda i: (i, 0)),
            pl.BlockSpec((tm, D_MODEL), lambda i: (i, 0)),
            whole((3, MIX_W, D_MODEL)), whole((3, D_MODEL)), whole((D_MODEL, D_MODEL)),
            whole((1, D_MODEL)), whole((D_MODEL, LANES)), whole((D_MODEL, LANES)),
            whole((1, LANES)), whole((tm, tm)),
        ],
        out_specs=[
            pl.BlockSpec((tm, XE_COLS), lambda i: (i, 0)),
            pl.BlockSpec((8, tm), lambda i: (0, i)),
            pl.BlockSpec((ROUTE_ROWS, LANES), lambda i: (0, 0)),
        ],
        out_shape=[
            jax.ShapeDtypeStruct((t, XE_COLS), F32),
            jax.ShapeDtypeStruct((8, t), F32),
            jax.ShapeDtypeStruct((ROUTE_ROWS, LANES), F32),
        ],
        compiler_params=_params(("arbitrary",)),
        name="merge",
    )(oa, ob, oc, big, x2, wb, bgate, wo, nffn, wr_hi, wr_lo, br, ustrict)


MOE_TILE = 256
ROW_DMA_TILE = 1024


def _row_dma(pos_ref, near_ref, far_ref, sem, r, to_far):
    near, far = near_ref.at[pl.ds(r, 1)], far_ref.at[pl.ds(pos_ref[0, 0, r], 1)]
    return pltpu.make_async_copy(near, far, sem) if to_far else pltpu.make_async_copy(far, near, sem)


def _row_burst(pos_ref, near_ref, far_ref, sem, tm, to_far):
    for r in range(tm):
        _row_dma(pos_ref, near_ref, far_ref, sem, r, to_far).start(priority=r % 2)

    def drain(r, carry):
        _row_dma(pos_ref, near_ref, far_ref, sem, r, to_far).wait()
        return carry

    lax.fori_loop(0, tm, drain, 0, unroll=8)


def _row_gather_kernel(pos_ref, src_ref, o_ref, sem, *, tm):
    _row_burst(pos_ref, o_ref, src_ref, sem, tm, to_far=False)


def _row_scatter_kernel(pad_ref, pos_ref, src_ref, dst_ref, zeros_ref, sem, *, tm, pad_rows):
    @pl.when(pl.program_id(0) == 0)
    def _():
        zeros_ref[...] = jnp.zeros_like(zeros_ref)
        n_cls = pad_ref.shape[0] - 1

        def blank(c):
            start = pl.multiple_of(pad_ref[c], 8)
            return pltpu.make_async_copy(zeros_ref, dst_ref.at[pl.ds(start, pad_rows)], sem)

        for c in range(n_cls):
            blank(c).start()
        for c in range(n_cls):
            blank(c).wait()

        tile = pad_rows - 8

        def blank_tile(j):
            start = pl.multiple_of(j * tile, tile)
            return pltpu.make_async_copy(zeros_ref.at[pl.ds(0, tile)], dst_ref.at[pl.ds(start, tile)], sem)

        def start_tile(j, carry):
            blank_tile(j).start()
            return carry

        def wait_tile(j, carry):
            blank_tile(j).wait()
            return carry

        first_unused, n_tiles = pad_ref[n_cls] // tile, dst_ref.shape[0] // tile
        lax.fori_loop(first_unused, n_tiles, start_tile, 0)
        lax.fori_loop(first_unused, n_tiles, wait_tile, 0)

    _row_burst(pos_ref, src_ref, dst_ref, sem, tm, to_far=True)


def _row_gather(pos, src, out_rows):
    tm = min(ROW_DMA_TILE, out_rows)
    width = src.shape[1]
    return pl.pallas_call(
        functools.partial(_row_gather_kernel, tm=tm),
        grid=(out_rows // tm,),
        in_specs=[pl.BlockSpec((1, 1, tm), lambda i: (i, 0, 0), memory_space=pltpu.SMEM),
                  pl.BlockSpec(memory_space=pl.ANY)],
        out_specs=pl.BlockSpec((tm, width), lambda i: (i, 0)),
        out_shape=jax.ShapeDtypeStruct((out_rows, width), src.dtype),
        scratch_shapes=[pltpu.SemaphoreType.DMA(())],
        compiler_params=_params(("arbitrary",)),
        name="moe_gather",
    )(pos.reshape(out_rows // tm, 1, tm), src)


def _row_scatter(pos, pad_start, src, out_rows, pad_rows):
    t, width = src.shape
    tm = min(ROW_DMA_TILE, t)
    grid_spec = pltpu.PrefetchScalarGridSpec(
        num_scalar_prefetch=1,
        grid=(t // tm,),
        in_specs=[pl.BlockSpec((1, 1, tm), lambda i, pad: (i, 0, 0), memory_space=pltpu.SMEM),
                  pl.BlockSpec((tm, width), lambda i, pad: (i, 0))],
        out_specs=pl.BlockSpec(memory_space=pl.ANY),
        scratch_shapes=[pltpu.VMEM((pad_rows, width), src.dtype), pltpu.SemaphoreType.DMA(())],
    )
    return pl.pallas_call(
        functools.partial(_row_scatter_kernel, tm=tm, pad_rows=pad_rows),
        grid_spec=grid_spec,
        out_shape=jax.ShapeDtypeStruct((out_rows, width), src.dtype),
        compiler_params=_params(("arbitrary",)),
        name="moe_scatter",
    )(pad_start, pos.reshape(t // tm, 1, tm), src)


def _moe_sorted_kernel(ea_ref, eb_ref, nused_ref, xs_ref, nffn_ref, wga_ref, wua_ref, wda_ref,
                       wgb_ref, wub_ref, wdb_ref, o_ref, *, tm):
    i = pl.program_id(0)

    @pl.when(i >= nused_ref[0])
    def _():
        o_ref[...] = jnp.zeros_like(o_ref)


    @pl.when(i < nused_ref[0])
    def _():
        xn = xs_ref[:, 0:D_MODEL]
        comb = xs_ref[:, D_MODEL:]
        ms = jnp.mean(xn * xn, axis=-1, keepdims=True)
        h = (xn * lax.rsqrt(ms + EPS) * nffn_ref[...]).astype(BF16)
        lane = lax.broadcasted_iota(jnp.int32, (tm, LANES), 1)
        out = xn
        for e_ref, wg_ref, wu_ref, wd_ref in ((ea_ref, wga_ref, wua_ref, wda_ref),
                                              (eb_ref, wgb_ref, wub_ref, wdb_ref)):
            w = jnp.sum(jnp.where(lane == e_ref[i] + RE_LANE, comb, 0.0), axis=-1, keepdims=True)
            hid = _silu(_dot(h, wg_ref[0])) * _dot(h, wu_ref[0])
            out = out + w * _dot(hid.astype(BF16), wd_ref[0])
        o_ref[...] = out


def _moe_sorted(ea, eb, nused, xs, nffn, wg, wu, wd, *, tm):
    n_tiles = xs.shape[0] // tm

    def w_in(which):
        return pl.BlockSpec((1, D_MODEL, D_EXPERT), lambda i, ea, eb, nu: ((ea, eb)[which][i], 0, 0))

    def w_out(which):
        return pl.BlockSpec((1, D_EXPERT, D_MODEL), lambda i, ea, eb, nu: ((ea, eb)[which][i], 0, 0))

    grid_spec = pltpu.PrefetchScalarGridSpec(
        num_scalar_prefetch=3,
        grid=(n_tiles,),
        in_specs=[
            pl.BlockSpec((tm, XE_COLS), lambda i, ea, eb, nu: (jnp.minimum(i, nu[0] - 1), 0)),
            pl.BlockSpec((1, D_MODEL), lambda i, ea, eb, nu: (0, 0)),
            w_in(0), w_in(0), w_out(0), w_in(1), w_in(1), w_out(1),
        ],
        out_specs=pl.BlockSpec((tm, D_MODEL), lambda i, ea, eb, nu: (i, 0)),
    )
    return pl.pallas_call(
        functools.partial(_moe_sorted_kernel, tm=tm),
        grid_spec=grid_spec,
        out_shape=jax.ShapeDtypeStruct((n_tiles * tm, D_MODEL), F32),
        compiler_params=_params(("arbitrary",)),
        name="moe",
    )(ea, eb, nused, xs, nffn, wg, wu, wd, wg, wu, wd)


_PAIR_LO = np.array([0, 0, 0, 1, 1, 2], np.int32)
_PAIR_HI = np.array([1, 2, 3, 2, 3, 3], np.int32)


def _moe(xe, route, counts, nffn, wg, wu, wd):
    t = xe.shape[0]
    tm = min(MOE_TILE, t)
    n_cls = N_EGROUPS * PAIRS_PER_GROUP
    n_tiles = t // tm + n_cls
    cnt = counts[:n_cls, 0].astype(jnp.int32)
    tiles = (cnt + tm - 1) // tm
    tile_end = jnp.cumsum(tiles)
    n_used = tile_end[-1]
    cls_base = (tile_end - tiles) * tm
    tile_idx = jnp.minimum(jnp.arange(n_tiles), n_used - 1)
    tile_cls = jnp.sum((tile_end[None, :] <= tile_idx[:, None]).astype(jnp.int32), axis=1)
    grp, pair = tile_cls // PAIRS_PER_GROUP, tile_cls % PAIRS_PER_GROUP
    ea = EXP_PER_GROUP * grp + jnp.asarray(_PAIR_LO)[pair]
    eb = EXP_PER_GROUP * grp + jnp.asarray(_PAIR_HI)[pair]
    rank = (route[1] * LANES + route[2]).astype(jnp.int32)
    pos = cls_base[route[0].astype(jnp.int32)] + rank

    pad_start = jnp.minimum((cls_base + cnt) // 8 * 8, n_tiles * tm - (tm + 8))
    pad_start = jnp.concatenate([pad_start, (n_used * tm).reshape(1)])
    xs = _row_scatter(pos, pad_start, xe, n_tiles * tm, tm + 8)
    ys = _moe_sorted(ea, eb, n_used.reshape(1), xs, nffn, wg, wu, wd, tm=tm)
    return _row_gather(pos, ys, t)


def _cols(w, off, width):
    return w[:, off:off + width]


def _pad_lanes(v, lane0, width=LANES):
    out = jnp.zeros((1, width), F32)
    return out.at[0, lane0:lane0 + v.shape[0]].set(v.astype(F32))


def _layer(x2, b, s, norm_mix, w_in, fox_f_bias, fox_q_norm, fox_k_norm, gla_w_lr, gla_b_gate,
           gla_out_norm, ssm_conv_w, ssm_conv_b, ssm_dt_bias, ssm_a_log, ssm_d, ssm_out_norm,
           w_branch, b_branch_gate, w_out, norm_ffn, w_router_grp, b_router_grp,
           w_router_exp, b_router_exp, w_exp_gate, w_exp_up, w_exp_down):
    t = b * s
    w_big = jnp.concatenate([
        _cols(w_in, _O_GATE, 3 * D_MODEL), _cols(w_in, _O_FQ, MIX_W), _cols(w_in, _O_FK, MIX_W),
        _cols(w_in, _O_FV, MIX_W), _cols(w_in, _O_GV, MIX_W), _cols(w_in, _O_GR, MIX_W),
        _cols(w_in, _O_SZ, MIX_W), _cols(w_in, _O_SX, MIX_W), _cols(w_in, _O_GQ, 256),
        _cols(w_in, _O_GK, 256), _cols(w_in, _O_SB, LANES), _cols(w_in, _O_SC, LANES),
    ], axis=1).astype(BF16)
    w_small = jnp.concatenate([
        _cols(w_in, _O_FF, FOX_HEADS), _cols(w_in, _O_GLR, GLA_RANK), _cols(w_in, _O_SDT, SSM_HEADS),
        jnp.zeros((D_MODEL, LANES - FOX_HEADS - GLA_RANK - SSM_HEADS), F32),
    ], axis=1)
    ws_hi, ws_lo = _split2(w_small)

    big, small = _inproj(x2, norm_mix.reshape(1, D_MODEL), w_big, ws_hi, ws_lo)
    big3 = big.reshape(b, s, BIG_COLS)
    small3 = small.reshape(b, s, LANES)

    qt, kf, vt = _fox_prep(big3, small3, _pad_lanes(fox_f_bias, FF_LANE),
                           jnp.tile(fox_q_norm, FOX_HEADS).reshape(1, MIX_W),
                           jnp.tile(fox_k_norm, FOX_HEADS).reshape(1, MIX_W), ts=min(FOX_TILE, s))
    logit_bound = ((FOX_HD ** 0.5) * 1.01 * jnp.max(jnp.abs(fox_q_norm))
                   * jnp.max(jnp.abs(fox_k_norm)))
    o_a = _fox_attn(qt, kf, vt, logit_bound)

    wl = jnp.zeros((LANES, GLA_HEADS * GLA_DK), F32).at[GLR_LANE:GLR_LANE + GLA_RANK].set(gla_w_lr)
    wl_hi, wl_lo = _split2(wl)
    o_b = _gla(big3, small3, wl_hi, wl_lo, gla_b_gate.reshape(1, -1),
               gla_out_norm.reshape(1, GLA_DV))

    o_c = _ssd(big3, small3, ssm_conv_w, ssm_conv_b.reshape(1, -1),
               _pad_lanes(ssm_dt_bias, SDT_LANE),
               jnp.repeat(ssm_a_log, SSM_HD).reshape(1, MIX_W),
               jnp.broadcast_to(ssm_dt_bias[:, None], (SSM_HEADS, LANES)),
               jnp.broadcast_to(ssm_a_log[:, None], (SSM_HEADS, LANES)),
               jnp.repeat(ssm_d, SSM_HD).reshape(1, MIX_W),
               ssm_out_norm.reshape(1, MIX_W))

    w_r = jnp.concatenate([w_router_grp, jnp.zeros((D_MODEL, RE_LANE - N_EGROUPS), F32), w_router_exp,
                           jnp.zeros((D_MODEL, LANES - RE_LANE - N_EXPERTS), F32)], axis=1)
    wr_hi, wr_lo = _split2(w_r)
    b_r = jnp.concatenate([b_router_grp, jnp.zeros((RE_LANE - N_EGROUPS,), F32), b_router_exp,
                           jnp.zeros((LANES - RE_LANE - N_EXPERTS,), F32)]).reshape(1, LANES)
    nffn = norm_ffn.reshape(1, D_MODEL)
    xe, route, counts = _merge(o_a.reshape(t, MIX_W), o_b.reshape(t, MIX_W), o_c.reshape(t, MIX_W),
                               big, x2, w_branch.astype(BF16), b_branch_gate, w_out.astype(BF16),
                               nffn, wr_hi, wr_lo, b_r)

    return _moe(xe, route, counts, nffn, w_exp_gate.astype(BF16), w_exp_up.astype(BF16),
                w_exp_down.astype(BF16))


def kernel(x, norm_mix, w_in, fox_f_bias, fox_q_norm, fox_k_norm, gla_w_lr, gla_b_gate, gla_out_norm, ssm_conv_w, ssm_conv_b, ssm_dt_bias, ssm_a_log, ssm_d, ssm_out_norm, w_branch, b_branch_gate, w_out, norm_ffn, w_router_grp, b_router_grp, w_router_exp, b_router_exp, w_exp_gate, w_exp_up, w_exp_down):
    b, s, d = x.shape
    x2 = x.reshape(b * s, d)
    per_layer = (norm_mix, w_in, fox_f_bias, fox_q_norm, fox_k_norm, gla_w_lr, gla_b_gate,
                 gla_out_norm, ssm_conv_w, ssm_conv_b, ssm_dt_bias, ssm_a_log, ssm_d,
                 ssm_out_norm, w_branch, b_branch_gate, w_out, norm_ffn, w_router_grp,
                 b_router_grp, w_router_exp, b_router_exp, w_exp_gate, w_exp_up, w_exp_down)
    for l in range(norm_mix.shape[0]):
        x2 = _layer(x2, b, s, *[p[l] for p in per_layer])
    return x2.reshape(b, s, d)
```
